```python
import math
import jax, jax.numpy as jnp
from jax import lax
import numpy as np

D_MODEL = 2048
BATCH = 1
SEQ = 8192
DEPTH = 2

GRID_W = 64
CTX_LEN = 256
D_MIX = D_MODEL
D_S5 = D_MIX // 2
D_CONV = D_MIX - D_S5
S5_GROUP = 16
S5_GROUPS = D_S5 // S5_GROUP
S5_STATE = 64
N_DIR = 2
CONV_WIDTH = 3
D_IN = 2 * D_S5 + 4 * D_CONV
DEEPNORM_ALPHA = (2.0 * DEPTH) ** 0.25
DEEPNORM_BETA = (8.0 * DEPTH) ** -0.25
LN_EPS = 1e-6
DT_MIN = 1e-3
DT_MAX = 1e-1

kernel_name = 'hybrid_s5_shortconv_deepnorm_dit'


def _layernorm(v):
    v32 = v.astype(jnp.float32)
    mu = jnp.mean(v32, axis=-1, keepdims=True)
    var = jnp.mean(jnp.square(v32 - mu), axis=-1, keepdims=True)
    return (v32 - mu) * lax.rsqrt(var + LN_EPS)


def _adaln(cond, w_ada, b_ada):
    mod = jax.nn.silu(cond) @ w_ada + b_ada
    return mod[..., :D_MODEL], mod[..., D_MODEL:2 * D_MODEL], mod[..., 2 * D_MODEL:]


def _modulate(v, shift, scale):
    return (_layernorm(v) * (1.0 + scale) + shift).astype(v.dtype)


def _split_proj(p):
    o1, o2, o3, o4, o5 = D_S5, 2 * D_S5, 2 * D_S5 + D_CONV, 2 * D_S5 + 2 * D_CONV, 2 * D_S5 + 3 * D_CONV
    return p[..., :o1], p[..., o1:o2], p[..., o2:o3], p[..., o3:o4], p[..., o4:o5], p[..., o5:]


def _s5_discretize(a_re, a_im, log_dt, b_re, b_im):
    lam = lax.complex(a_re.astype(jnp.float32), a_im.astype(jnp.float32))
    dt = jnp.exp(log_dt.astype(jnp.float32))[:, None]
    a_bar = jnp.exp(lam * dt)
    b = lax.complex(b_re.astype(jnp.float32), b_im.astype(jnp.float32))
    b_bar = ((a_bar - 1.0) / lam)[..., None] * b
    return a_bar, b_bar


def _scan_op(e1, e2):
    a1, b1 = e1
    a2, b2 = e2
    return a2 * a1, a2 * b1 + b2


def _s5_scan(a_bar, b_bar, u, h0, reverse):
    bu = jnp.einsum('gph,bngh->bngp', b_bar, u.astype(jnp.complex64))
    if h0 is not None:
        edge = -1 if reverse else 0
        bu = bu.at[:, edge].add(a_bar * h0)
    a = jnp.broadcast_to(a_bar, bu.shape)
    _, states = lax.associative_scan(_scan_op, (a, bu), axis=1, reverse=reverse)
    return states


def _s5_glu(y, w_glu, b_glu):
    g = jax.nn.gelu(y)
    return g * jax.nn.sigmoid(g @ w_glu.astype(jnp.float32) + b_glu.astype(jnp.float32))


def _s5_mixer(u_lat, u_ctx, a_re, a_im, log_dt, b_re, b_im, c_re, c_im, d_skip, w_glu, b_glu, with_ctx_out):
    bsz, n_lat, _ = u_lat.shape
    n_ctx = u_ctx.shape[1]
    ul = u_lat.astype(jnp.float32).reshape(bsz, n_lat, S5_GROUPS, S5_GROUP)
    uc = u_ctx.astype(jnp.float32).reshape(bsz, n_ctx, S5_GROUPS, S5_GROUP)
    d_g = d_skip.astype(jnp.float32).reshape(S5_GROUPS, S5_GROUP)
    y_lat = ul * d_g
    y_ctx = uc * d_g
    for d in range(N_DIR):
        reverse = d == 1
        a_bar, b_bar = _s5_discretize(a_re[d], a_im[d], log_dt[d], b_re[d], b_im[d])
        c_mat = lax.complex(c_re[d].astype(jnp.float32), c_im[d].astype(jnp.float32))
        h_ctx = _s5_scan(a_bar, b_bar, uc, None, reverse)
        h_last = h_ctx[:, 0] if reverse else h_ctx[:, -1]
        h_lat = _s5_scan(a_bar, b_bar, ul, h_last, reverse)
        y_lat = y_lat + jnp.einsum('ghp,bngp->bngh', c_mat, h_lat).real
        if with_ctx_out:
            y_ctx = y_ctx + jnp.einsum('ghp,bngp->bngh', c_mat, h_ctx).real
    out_lat = _s5_glu(y_lat.reshape(bsz, n_lat, D_S5), w_glu, b_glu)
    if not with_ctx_out:
        return out_lat, None
    return out_lat, _s5_glu(y_ctx.reshape(bsz, n_ctx, D_S5), w_glu, b_glu)


def _dwconv3(v, w, b):
    vp = jnp.pad(v, [(0, 0)] * (v.ndim - 2) + [(1, 1), (0, 0)])
    return vp[..., :-2, :] * w[0] + vp[..., 1:-1, :] * w[1] + vp[..., 2:, :] * w[2] + b


def _conv_branch(v, bg, cg, conv_w, conv_b, rows):
    s = cg * v
    if rows is None:
        conv = _dwconv3(s, conv_w, conv_b)
    else:
        bsz, n, ch = s.shape
        conv = _dwconv3(s.reshape(bsz, rows, GRID_W, ch), conv_w, conv_b).reshape(bsz, n, ch)
    return bg * conv


def _merge_out(y_s5, z_s5, y_cv, z_cv, w_out):
    o = jnp.concatenate([y_s5 * jax.nn.silu(z_s5), y_cv * jax.nn.silu(z_cv)], axis=-1)
    return o @ w_out


def _post_norm(res, sub, gate, ln_g, ln_b):
    v = DEEPNORM_ALPHA * res.astype(jnp.float32) + gate * sub.astype(jnp.float32)
    return (_layernorm(v) * ln_g + ln_b).astype(res.dtype)


def _layer(x, ctx, c, c_ctx, w_ada, b_ada, w_in, a_re, a_im, log_dt, b_re, b_im, c_re, c_im,
           d_skip, w_glu, b_glu, conv_w, conv_b, w_out, ln_g, ln_b, update_ctx):
    n_lat = x.shape[1]
    rows = n_lat // GRID_W
    shift, scale, gate = _adaln(c, w_ada, b_ada)
    shift_c, scale_c, gate_c = _adaln(c_ctx, w_ada, b_ada)
    h = _modulate(x, shift[:, None], scale[:, None])
    hc = _modulate(ctx, shift_c, scale_c)
    u, z_s5, v, bg, cg, z_cv = _split_proj(h @ w_in)
    if update_ctx:
        u_c, z_s5_c, v_c, bg_c, cg_c, z_cv_c = _split_proj(hc @ w_in)
    else:
        u_c = hc @ w_in[:, :D_S5]
    y_s5, y_s5_c = _s5_mixer(u, u_c, a_re, a_im, log_dt, b_re, b_im, c_re, c_im,
                             d_skip, w_glu, b_glu, update_ctx)
    y_cv = _conv_branch(v, bg, cg, conv_w, conv_b, rows)
    x_new = _post_norm(x, _merge_out(y_s5, z_s5, y_cv, z_cv, w_out), gate[:, None], ln_g, ln_b)
    if not update_ctx:
        return x_new, ctx
    y_cv_c = _conv_branch(v_c, bg_c, cg_c, conv_w, conv_b, None)
    ctx_new = _post_norm(ctx, _merge_out(y_s5_c, z_s5_c, y_cv_c, z_cv_c, w_out), gate_c, ln_g, ln_b)
    return x_new, ctx_new


def setup_inputs(seed: int = 0) -> dict:
    key = jax.random.key(seed)
    ks = jax.random.split(key, 24)
    f32 = jnp.float32

    def nrm(k, shape, s):
        return s * jax.random.normal(k, shape, f32)

    s5_a = (DEPTH, N_DIR, S5_GROUPS, S5_STATE)
    s5_b = (DEPTH, N_DIR, S5_GROUPS, S5_STATE, S5_GROUP)
    s5_c = (DEPTH, N_DIR, S5_GROUPS, S5_GROUP, S5_STATE)
    n_idx = jnp.arange(S5_STATE, dtype=f32)
    return {
        'x': nrm(ks[0], (BATCH, SEQ, D_MODEL), 1.0),
        'c': nrm(ks[1], (BATCH, D_MODEL), 1.0),
        'ctx': nrm(ks[2], (BATCH, CTX_LEN, D_MODEL), 1.0),
        'c_ctx': nrm(ks[3], (D_MODEL,), 1.0),
        'w_ada': nrm(ks[4], (DEPTH, D_MODEL, 3 * D_MODEL), 0.5 * D_MODEL ** -0.5),
        'b_ada': nrm(ks[5], (DEPTH, 3 * D_MODEL), 0.02),
        'w_in': nrm(ks[6], (DEPTH, D_MODEL, D_IN), D_MODEL ** -0.5),
        's5_a_re': -0.5 + nrm(ks[7], s5_a, 0.01),
        's5_a_im': math.pi * n_idx + nrm(ks[8], s5_a, 0.01),
        's5_log_dt': jax.random.uniform(ks[9], (DEPTH, N_DIR, S5_GROUPS), f32,
                                        math.log(DT_MIN), math.log(DT_MAX)),
        's5_b_re': nrm(ks[10], s5_b, (2 * S5_GROUP) ** -0.5),
        's5_b_im': nrm(ks[11], s5_b, (2 * S5_GROUP) ** -0.5),
        's5_c_re': nrm(ks[12], s5_c, S5_STATE ** -0.5),
        's5_c_im': nrm(ks[13], s5_c, S5_STATE ** -0.5),
        's5_d': nrm(ks[14], (DEPTH, D_S5), 1.0),
        'w_glu': nrm(ks[15], (DEPTH, D_S5, D_S5), D_S5 ** -0.5),
        'b_glu': nrm(ks[16], (DEPTH, D_S5), 0.02),
        'conv_w': nrm(ks[17], (DEPTH, CONV_WIDTH, D_CONV), CONV_WIDTH ** -0.5),
        'conv_b': nrm(ks[18], (DEPTH, D_CONV), 0.02),
        'w_out': nrm(ks[19], (DEPTH, D_MIX, D_MODEL), DEEPNORM_BETA * D_MIX ** -0.5),
        'ln_g': 1.0 + nrm(ks[20], (DEPTH, D_MODEL), 0.02),
        'ln_b': nrm(ks[21], (DEPTH, D_MODEL), 0.02),
    }


def reference(x, c, ctx, c_ctx, w_ada, b_ada, w_in, s5_a_re, s5_a_im, s5_log_dt, s5_b_re, s5_b_im,
              s5_c_re, s5_c_im, s5_d, w_glu, b_glu, conv_w, conv_b, w_out, ln_g, ln_b):
    for i in range(DEPTH):
        x, ctx = _layer(x, ctx, c, c_ctx, w_ada[i], b_ada[i], w_in[i], s5_a_re[i], s5_a_im[i],
                        s5_log_dt[i], s5_b_re[i], s5_b_im[i], s5_c_re[i], s5_c_im[i], s5_d[i],
                        w_glu[i], b_glu[i], conv_w[i], conv_b[i], w_out[i], ln_g[i], ln_b[i],
                        i < DEPTH - 1)
    return x
```

```python
import functools

import jax
import jax.numpy as jnp
from jax import lax
from jax.experimental import pallas as pl
from jax.experimental.pallas import tpu as pltpu

D = 2048
DS = 1024
G = 64
H = 16
P = 64
T = 16
TH = T * H
GRID_W = 64
LN_EPS = 1e-6
LANES = 128
VMEM_LIMIT = 60 * 1024 * 1024

F32 = jnp.float32
BF16 = jnp.bfloat16
NT_DIMS = (((1,), (1,)), ((), ()))
TN_DIMS = (((0,), (0,)), ((), ()))


def _cparams(sem):
    return pltpu.CompilerParams(dimension_semantics=sem, vmem_limit_bytes=VMEM_LIMIT)


def _layer_spec(shape, *idx):
    return pl.BlockSpec((None,) + shape, lambda *_: idx, pipeline_mode=pl.Buffered(1))


def _silu(v):
    return v * jax.nn.sigmoid(v)


def _adaln_kernel(condT_ref, w_ref, b_ref, o_ref):
    w = w_ref[0]
    for i in range(2):
        s = _silu(condT_ref[:, i:i + 1])
        o_ref[0, i:i + 1, :] = jnp.sum(w * s, axis=0, keepdims=True) + b_ref[0]


def _adaln(condT, w_ada, b_ada):
    depth, _, n3 = w_ada.shape
    tn = 512
    return pl.pallas_call(
        _adaln_kernel,
        grid=(depth, n3 // tn),
        in_specs=[pl.BlockSpec((D, 2), lambda l, j: (0, 0)),
                  pl.BlockSpec((1, D, tn), lambda l, j: (l, 0, j)),
                  pl.BlockSpec((1, 1, tn), lambda l, j: (l, 0, j))],
        out_specs=pl.BlockSpec((1, 2, tn), lambda l, j: (l, 0, j)),
        out_shape=jax.ShapeDtypeStruct((depth, 2, n3), F32),
        compiler_params=_cparams(("parallel", "parallel")),
        name="adaln",
    )(condT, w_ada, b_ada.reshape(depth, 1, n3))


def _cast_split_kernel(w_ref, wrow_ref, wt_ref, *, nt):
    w = w_ref[0]
    wt_ref[0] = w[:, 0:nt].T.astype(BF16)
    wrow_ref[0] = w[:, nt:].astype(BF16)


def _cast_split(w, nt, tr=256):
    depth, rows, cols = w.shape
    return pl.pallas_call(
        functools.partial(_cast_split_kernel, nt=nt),
        grid=(depth, rows // tr),
        in_specs=[pl.BlockSpec((1, tr, cols), lambda l, r: (l, r, 0))],
        out_specs=[pl.BlockSpec((1, tr, cols - nt), lambda l, r: (l, r, 0)),
                   pl.BlockSpec((1, nt, tr), lambda l, r: (l, 0, r))],
        out_shape=[jax.ShapeDtypeStruct((depth, rows, cols - nt), BF16),
                   jax.ShapeDtypeStruct((depth, nt, rows), BF16)],
        compiler_params=_cparams(("parallel", "parallel")),
        name="cast_split",
    )(w)


def _cast_kernel(w_ref, o_ref, *, transpose):
    w = w_ref[0]
    o_ref[0] = (w.T if transpose else w).astype(BF16)


def _cast(w, transpose, tr):
    depth, rows, cols = w.shape
    out_block = (1, cols, tr) if transpose else (1, tr, cols)
    out_map = (lambda l, r: (l, 0, r)) if transpose else (lambda l, r: (l, r, 0))
    return pl.pallas_call(
        functools.partial(_cast_kernel, transpose=transpose),
        grid=(depth, rows // tr),
        in_specs=[pl.BlockSpec((1, tr, cols), lambda l, r: (l, r, 0))],
        out_specs=pl.BlockSpec(out_block, out_map),
        out_shape=jax.ShapeDtypeStruct((depth, cols, rows) if transpose else (depth, rows, cols), BF16),
        compiler_params=_cparams(("parallel", "parallel")),
        name="cast_t" if transpose else "cast",
    )(w)


GP = 8
PRM_ROWS = 72


def _s5prep_kernel(prm_ref, w1t_ref, w2_ref, acoef_ref, kt_scr):
    lane = lax.broadcasted_iota(jnp.int32, (1, LANES), 1)
    fwd_lane = lane < P
    krow = lax.broadcasted_iota(jnp.int32, (T, 1), 0)
    fwd_row = lax.broadcasted_iota(jnp.int32, (2 * P, 1), 0) < P
    col256 = lax.broadcasted_iota(jnp.int32, (T, TH), 1)
    r16 = lax.broadcasted_iota(jnp.int32, (T, TH), 0)
    e_rep = (col256 % H == r16).astype(BF16)
    e_blk = (col256 // H == r16).astype(BF16)
    colblk = lax.broadcasted_iota(jnp.int32, (TH, TH), 1) // H

    def rows16(tab):
        return jnp.broadcast_to(tab[:, None, :], (T, H, tab.shape[1])).reshape(TH, tab.shape[1])

    def tile16(mat):
        return jnp.broadcast_to(mat[None], (T, H, mat.shape[1])).reshape(TH, mat.shape[1])

    def bdot(a, b):
        return jnp.dot(a.astype(BF16), b.astype(BF16), preferred_element_type=F32)

    def expand(tab, e):
        return lax.dot_general(tab.astype(BF16), e, TN_DIMS, preferred_element_type=F32)

    for gi in range(GP):
        ar, ai = prm_ref[0, gi, 0:1, :], prm_ref[0, gi, 1:2, :]
        dt = jnp.exp(prm_ref[0, gi, 2:3, :])
        la, th = ar * dt, ai * dt
        btr, bti = prm_ref[0, gi, 8:24, :], prm_ref[0, gi, 24:40, :]
        cr, ci = tile16(prm_ref[0, gi, 40:56, :]), tile16(prm_ref[0, gi, 56:72, :])

        mag1 = jnp.exp(la)
        a1r, a1i = mag1 * jnp.cos(th), mag1 * jnp.sin(th)
        squares = [(a1r, a1i)]
        for _ in range(4):
            sr, si = squares[-1]
            squares.append((sr * sr - si * si, 2.0 * sr * si))

        def powrow(kf, kb):
            k = jnp.where(fwd_lane, kf, kb)
            pr, pi = jnp.ones(k.shape, F32), jnp.zeros(k.shape, F32)
            for b, (sr, si) in enumerate(squares):
                bit = ((k >> b) & 1) == 1
                pr, pi = jnp.where(bit, pr * sr - pi * si, pr), jnp.where(bit, pr * si + pi * sr, pi)
            return pr, pi

        def ctab(kf, kb):
            pr, pi = powrow(kf, kb)
            pr, pi = rows16(pr), rows16(pi)
            return cr * pr - ci * pi, cr * pi + ci * pr

        zr, zi = a1r - 1.0, a1i
        den = ar * ar + ai * ai
        fr, fi = (zr * ar + zi * ai) / den, (zi * ar - zr * ai) / den
        bbr_x = expand(fr * btr - fi * bti, e_rep)
        bbi_x = expand(fr * bti + fi * btr, e_rep)

        ckr, cki = ctab(krow, (T - 1) - krow)
        zero = jnp.zeros_like(bbr_x)
        kf = bdot(ckr, jnp.where(fwd_row, bbr_x, zero)) - bdot(cki, jnp.where(fwd_row, bbi_x, zero))
        kb = bdot(ckr, jnp.where(fwd_row, zero, bbr_x)) - bdot(cki, jnp.where(fwd_row, zero, bbi_x))
        kt_scr[0:TH - H, :] = kb[0:TH - H]
        kt_scr[TH - H:TH, :] = kb[TH - H:TH] + kf[0:H]
        kt_scr[TH:2 * TH - H, :] = kf[H:TH]
        m = jnp.zeros((TH, TH), F32)
        for s in range(T):
            win = kt_scr[(T - 1 - s) * H:(T - 1 - s) * H + TH, :]
            m = jnp.where(colblk == s, win, m)
        w1t_ref[0, gi, 0:TH, :] = m.astype(BF16)

        pir, pii = powrow((T - 1) - krow, krow)
        pxr, pxi = expand(pir, e_blk), expand(pii, e_blk)
        w1t_ref[0, gi, TH:TH + 2 * P, :] = (pxr * bbr_x - pxi * bbi_x).astype(BF16)
        w1t_ref[0, gi, TH + 2 * P:2 * TH, :] = (pxr * bbi_x + pxi * bbr_x).astype(BF16)

        cor, coi = ctab(krow + 1, T - krow)
        w2_ref[0, gi, :, 0:LANES] = cor.astype(BF16)
        w2_ref[0, gi, :, LANES:2 * LANES] = (-coi).astype(BF16)

        acoef_ref[0, 0, gi:gi + 1, :] = squares[4][0]
        acoef_ref[0, 1, gi:gi + 1, :] = squares[4][1]


def _s5prep(a_re, a_im, log_dt, b_re, b_im, c_re, c_im):
    depth = a_re.shape[0]

    def dircat(v):
        return jnp.concatenate([v[:, 0], v[:, 1]], axis=-1)

    ld = jnp.broadcast_to(log_dt[..., None, None], log_dt.shape + (1, P))
    pad = jnp.zeros((depth, G, 5, 2 * P), F32)
    prm = jnp.concatenate([dircat(a_re[:, :, :, None, :]), dircat(a_im[:, :, :, None, :]), dircat(ld), pad,
                           dircat(jnp.swapaxes(b_re, -1, -2)), dircat(jnp.swapaxes(b_im, -1, -2)),
                           dircat(c_re), dircat(c_im)], axis=2)
    return pl.pallas_call(
        _s5prep_kernel,
        grid=(depth, G // GP),
        in_specs=[pl.BlockSpec((1, GP, PRM_ROWS, 2 * P), lambda l, g: (l, g, 0, 0))],
        out_specs=[pl.BlockSpec((1, GP, 2 * TH, TH), lambda l, g: (l, g, 0, 0)),
                   pl.BlockSpec((1, GP, TH, TH), lambda l, g: (l, g, 0, 0)),
                   pl.BlockSpec((1, 2, GP, 2 * P), lambda l, g: (l, 0, g, 0))],
        out_shape=[jax.ShapeDtypeStruct((depth, G, 2 * TH, TH), BF16),
                   jax.ShapeDtypeStruct((depth, G, TH, TH), BF16),
                   jax.ShapeDtypeStruct((depth, 2, G, 2 * P), F32)],
        scratch_shapes=[pltpu.VMEM((2 * TH, TH), F32)],
        compiler_params=_cparams(("parallel", "parallel")),
        name="s5prep",
    )(prm)


def _plane_copies(hbm3, buf, sem, step, slot, *, cb, tb, ntt, to_hbm):
    b, tt = step // ntt, step % ntt
    copies = []
    for t in range(tb):
        hbm = hbm3.at[pl.ds(b * cb, cb), tt * tb + t, :]
        vm = buf.at[slot, t]
        copies.append(pltpu.make_async_copy(vm, hbm, sem.at[slot, t]) if to_hbm
                      else pltpu.make_async_copy(hbm, vm, sem.at[slot, t]))
    return copies


def _fetch_planes(x_hbm, xbuf, xsem, step, nsteps, **kw):
    slot = step % 2
    mk = functools.partial(_plane_copies, x_hbm, xbuf, xsem, to_hbm=False, **kw)

    @pl.when(step == 0)
    def _():
        for cp in mk(step, slot):
            cp.start()

    @pl.when(step + 1 < nsteps)
    def _():
        for cp in mk(step + 1, 1 - slot):
            cp.start()

    for cp in mk(step, slot):
        cp.wait()
    return slot


def _layernorm_rows(v):
    mu = jnp.mean(v, axis=-1, keepdims=True)
    vc = v - mu
    var = jnp.mean(vc * vc, axis=-1, keepdims=True)
    return vc * lax.rsqrt(var + LN_EPS)


def _inproj_kernel(*refs, cb, tb, nb, ntt, u_only):
    if u_only:
        x_hbm, mod_ref, wut_ref, ut_ref, xbuf, xsem, h_scr = refs
    else:
        (x_hbm, mod_ref, wut_ref, wzt_ref, wv_ref, wb_ref, wc_ref, wzc_ref,
         ut_ref, szt_ref, s_ref, bg_ref, szc_ref, xbuf, xsem, h_scr) = refs
    step = pl.program_id(0) * ntt + pl.program_id(1)
    slot = _fetch_planes(x_hbm, xbuf, xsem, step, nb * ntt, cb=cb, tb=tb, ntt=ntt)
    shift, scale = mod_ref[0:1, :], mod_ref[1:2, :]
    for t in range(tb):
        hn = _layernorm_rows(xbuf[slot, t]) * (1.0 + scale) + shift
        h_scr[t * cb:(t + 1) * cb, :] = hn.astype(BF16)
    h = h_scr[...]
    ut_ref[...] = lax.dot_general(wut_ref[...], h, NT_DIMS, preferred_element_type=F32).astype(BF16)
    if u_only:
        return
    szt_ref[...] = _silu(lax.dot_general(wzt_ref[...], h, NT_DIMS, preferred_element_type=F32)).astype(BF16)
    v = jnp.dot(h, wv_ref[...], preferred_element_type=F32)
    cg = jnp.dot(h, wc_ref[...], preferred_element_type=F32)
    s_ref[...] = (cg * v).astype(BF16)
    bg_ref[...] = jnp.dot(h, wb_ref[...], preferred_element_type=F32).astype(BF16)
    szc_ref[...] = _silu(jnp.dot(h, wzc_ref[...], preferred_element_type=F32)).astype(BF16)


def _inproj(x3, mod, wt, wrow, layer, mod_idx, cb, tb, u_only=False):
    nc = x3.shape[0]
    n = nc * T
    nb, ntt = nc // cb, T // tb
    r = cb * tb
    col = lambda b, tt: (0, b * ntt + tt)
    row = lambda b, tt: (b * ntt + tt, 0)
    in_specs = [pl.BlockSpec(memory_space=pl.ANY), _layer_spec((3, D), mod_idx, 0, 0),
                _layer_spec((DS, D), layer, 0, 0)]
    out_specs = [pl.BlockSpec((DS, r), col)]
    out_shape = [jax.ShapeDtypeStruct((DS, n), BF16)]
    args = [x3, mod, wt]
    if not u_only:
        in_specs += [_layer_spec((DS, D), layer, 1, 0)] + [_layer_spec((D, DS), layer, 0, k) for k in range(4)]
        out_specs += [pl.BlockSpec((DS, r), col)] + [pl.BlockSpec((r, DS), row)] * 3
        out_shape += [jax.ShapeDtypeStruct((DS, n), BF16)] + [jax.ShapeDtypeStruct((n, DS), BF16)] * 3
        args += [wt, wrow, wrow, wrow, wrow]
    return pl.pallas_call(
        functools.partial(_inproj_kernel, cb=cb, tb=tb, nb=nb, ntt=ntt, u_only=u_only),
        grid=(nb, ntt),
        in_specs=in_specs,
        out_specs=out_specs,
        out_shape=out_shape,
        scratch_shapes=[pltpu.VMEM((2, tb, cb, D), F32), pltpu.SemaphoreType.DMA((2, tb)),
                        pltpu.VMEM((r, D), BF16)],
        compiler_params=_cparams(("arbitrary", "arbitrary")),
        name="inproj_u" if u_only else "inproj",
    )(*args)


GB = 8
NCL = 512
NCC = 16
NCT = NCL + NCC
NCP = 640
CBL = 128


def _s5_kernel(ut_ref, utc_ref, w1t_ref, w2_ref, acoef_ref, dcol_ref, yt_ref, ytc_ref,
               sre_scr, sim_scr, hre_scr, him_scr, yloc_scr, *, ctx_out):
    nbl = NCL // CBL
    for gl in range(GB):
        rows = slice(gl * H, (gl + 1) * H)
        planes = []
        for t in range(T):
            lat = [ut_ref[rows, (b * T + t) * CBL:(b * T + t + 1) * CBL] for b in range(nbl)]
            planes.append(jnp.concatenate(lat + [utc_ref[t, rows, :]], axis=1))
        z = jnp.concatenate(planes, axis=0)
        r1 = jnp.dot(w1t_ref[gl], z, preferred_element_type=F32)
        yloc_scr[gl] = r1[0:TH] + dcol_ref[gl] * z.astype(F32)
        st = r1[TH:2 * TH].T
        sre_scr[pl.ds(gl, NCP, stride=GB), :] = st[:, 0:LANES]
        sim_scr[pl.ds(gl, NCP, stride=GB), :] = st[:, LANES:2 * LANES]

    ar, ai = acoef_ref[0], acoef_ref[1]
    fwd_lane = lax.broadcasted_iota(jnp.int32, (GB, LANES), 1) < P
    bwd_lane = jnp.logical_not(fwd_lane)
    hre_scr[NCT * GB:NCP * GB, :] = jnp.zeros(((NCP - NCT) * GB, LANES), F32)
    him_scr[NCT * GB:NCP * GB, :] = jnp.zeros(((NCP - NCT) * GB, LANES), F32)

    def step(k, carry, fwd_shift):
        hr, hi = carry
        cf = k + fwd_shift
        cb_ = NCT - 1 - k
        rf = pl.ds(pl.multiple_of(cf * GB, GB), GB)
        rb = pl.ds(pl.multiple_of(cb_ * GB, GB), GB)
        pltpu.store(hre_scr.at[rf, :], hr, mask=fwd_lane)
        pltpu.store(hre_scr.at[rb, :], hr, mask=bwd_lane)
        pltpu.store(him_scr.at[rf, :], hi, mask=fwd_lane)
        pltpu.store(him_scr.at[rb, :], hi, mask=bwd_lane)
        sr = jnp.where(fwd_lane, sre_scr[rf, :], sre_scr[rb, :])
        si = jnp.where(fwd_lane, sim_scr[rf, :], sim_scr[rb, :])
        return ar * hr - ai * hi + sr, ar * hi + ai * hr + si

    zero = jnp.zeros((GB, LANES), F32)
    carry = lax.fori_loop(0, NCC, functools.partial(step, fwd_shift=NCL), (zero, zero))
    lax.fori_loop(NCC, NCT, functools.partial(step, fwd_shift=-NCC), carry, unroll=4)

    for gl in range(GB):
        rows = slice(gl * H, (gl + 1) * H)
        sel = pl.ds(gl, NCP, stride=GB)
        hp = jnp.concatenate([hre_scr[sel, :], him_scr[sel, :]], axis=1)
        y = yloc_scr[gl] + lax.dot_general(w2_ref[gl], hp.astype(BF16), NT_DIMS, preferred_element_type=F32)
        yb = y.astype(BF16)
        for t in range(T):
            for b in range(nbl):
                yt_ref[rows, (b * T + t) * CBL:(b * T + t + 1) * CBL] = \
                    yb[t * H:(t + 1) * H, b * CBL:(b + 1) * CBL]
            if ctx_out:
                ytc_ref[t, rows, :] = yb[t * H:(t + 1) * H, NCL:NCP]
    if not ctx_out:
        ytc_ref[...] = jnp.zeros(ytc_ref.shape, BF16)


def _s5(ut, utc, w1t, w2, acoef, dcol, layer, ctx_out):
    n = ut.shape[1]
    ncx = NCP - NCL
    state = pltpu.VMEM((NCP * GB, LANES), F32)
    return pl.pallas_call(
        functools.partial(_s5_kernel, ctx_out=ctx_out),
        grid=(G // GB,),
        in_specs=[pl.BlockSpec((GB * H, n), lambda i: (i, 0)),
                  pl.BlockSpec((T, GB * H, ncx), lambda i: (0, i, 0)),
                  pl.BlockSpec((None, GB, 2 * TH, TH), lambda i: (layer, i, 0, 0)),
                  pl.BlockSpec((None, GB, TH, TH), lambda i: (layer, i, 0, 0)),
                  pl.BlockSpec((None, 2, GB, LANES), lambda i: (layer, 0, i, 0)),
                  pl.BlockSpec((None, GB, TH, 1), lambda i: (layer, i, 0, 0))],
        out_specs=[pl.BlockSpec((GB * H, n), lambda i: (i, 0)),
                   pl.BlockSpec((T, GB * H, ncx), lambda i: (0, i, 0))],
        out_shape=[jax.ShapeDtypeStruct((DS, n), BF16), jax.ShapeDtypeStruct((T, DS, ncx), BF16)],
        scratch_shapes=[state, state, state, state, pltpu.VMEM((GB, TH, NCP), F32)],
        compiler_params=_cparams(("parallel",)),
        name="s5",
    )(ut, utc, w1t, w2, acoef, dcol)


def _outproj_kernel(x_hbm, mod_ref, yt_ref, szt_ref, s_ref, sprev_ref, snext_ref, bg_ref, szc_ref,
                    wgt_ref, bglu_ref, cw_ref, cb_ref, wo5_ref, wocv_ref, lng_ref, lnb_ref,
                    o_hbm, xbuf, xsem, obuf, osem, *, cb, tb, nb, ntt, row_chunks, alpha):
    tt = pl.program_id(1)
    step = pl.program_id(0) * ntt + tt
    nsteps = nb * ntt
    slot = _fetch_planes(x_hbm, xbuf, xsem, step, nsteps, cb=cb, tb=tb, ntt=ntt)
    g = jax.nn.gelu(yt_ref[...].astype(F32))
    gate = jnp.dot(wgt_ref[...], g.astype(BF16), preferred_element_type=F32) + bglu_ref[...]
    o5t = (g * jax.nn.sigmoid(gate) * szt_ref[...].astype(F32)).astype(BF16)
    sub = lax.dot_general(o5t, wo5_ref[...], TN_DIMS, preferred_element_type=F32)
    crow = lax.broadcasted_iota(jnp.int32, (cb, 1), 0) % row_chunks
    sp = sprev_ref[...].astype(F32)
    sn = snext_ref[...].astype(F32)
    sp_wrap = jnp.where(crow != 0, pltpu.roll(sp, 1, 0), 0.0)
    sn_wrap = jnp.where(crow != row_chunks - 1, pltpu.roll(sn, cb - 1, 0), 0.0)
    sp = jnp.where(tt == 0, sp_wrap, sp)
    sn = jnp.where(tt == ntt - 1, sn_wrap, sn)
    s_all = jnp.concatenate([sp, s_ref[...].astype(F32), sn], axis=0)
    r = cb * tb
    conv = (cw_ref[0:1, :] * s_all[0:r] + cw_ref[1:2, :] * s_all[cb:cb + r]
            + cw_ref[2:3, :] * s_all[2 * cb:2 * cb + r] + cb_ref[...])
    ocv = (bg_ref[...].astype(F32) * conv * szc_ref[...].astype(F32)).astype(BF16)
    sub = sub + jnp.dot(ocv, wocv_ref[...], preferred_element_type=F32)
    gate_vec = mod_ref[2:3, :]
    for t in range(tb):
        v = alpha * xbuf[slot, t] + gate_vec * sub[t * cb:(t + 1) * cb]
        obuf[slot, t] = _layernorm_rows(v) * lng_ref[...] + lnb_ref[...]
    put = functools.partial(_plane_copies, o_hbm, obuf, osem, cb=cb, tb=tb, ntt=ntt, to_hbm=True)

    @pl.when(step > 0)
    def _():
        for cp in put(step - 1, 1 - slot):
            cp.wait()

    for cp in put(step, slot):
        cp.start()

    @pl.when(step == nsteps - 1)
    def _():
        for cp in put(step, slot):
            cp.wait()


def _outproj(x3, mod, yt, szt, s, bg, szc, wts, layer, mod_idx, cb, tb, row_chunks, alpha):
    nc = x3.shape[0]
    nb, ntt = nc // cb, T // tb
    r = cb * tb
    col = lambda b, tt: (0, b * ntt + tt)
    row = lambda b, tt: (b * ntt + tt, 0)
    prev_plane = lambda b, tt: (b * T + (tt * tb + T - 1) % T, 0)
    next_plane = lambda b, tt: (b * T + (tt * tb + tb) % T, 0)
    wgt, bglu, cw, cbias, wo, lng, lnb = wts
    planes = pltpu.VMEM((2, tb, cb, D), F32)
    return pl.pallas_call(
        functools.partial(_outproj_kernel, cb=cb, tb=tb, nb=nb, ntt=ntt, row_chunks=row_chunks, alpha=alpha),
        grid=(nb, ntt),
        in_specs=[pl.BlockSpec(memory_space=pl.ANY),
                  _layer_spec((3, D), mod_idx, 0, 0),
                  pl.BlockSpec((DS, r), col), pl.BlockSpec((DS, r), col),
                  pl.BlockSpec((r, DS), row),
                  pl.BlockSpec((cb, DS), prev_plane), pl.BlockSpec((cb, DS), next_plane),
                  pl.BlockSpec((r, DS), row), pl.BlockSpec((r, DS), row),
                  _layer_spec((DS, DS), layer, 0, 0), _layer_spec((DS, 1), layer, 0, 0),
                  _layer_spec((3, DS), layer, 0, 0), _layer_spec((1, DS), layer, 0, 0),
                  _layer_spec((DS, D), layer, 0, 0), _layer_spec((DS, D), layer, 1, 0),
                  _layer_spec((1, D), layer, 0, 0), _layer_spec((1, D), layer, 0, 0)],
        out_specs=pl.BlockSpec(memory_space=pl.ANY),
        out_shape=jax.ShapeDtypeStruct(x3.shape, F32),
        scratch_shapes=[planes, pltpu.SemaphoreType.DMA((2, tb)), planes, pltpu.SemaphoreType.DMA((2, tb))],
        compiler_params=_cparams(("arbitrary", "arbitrary")),
        name="outproj",
    )(x3, mod, yt, szt, s, s, s, bg, szc, wgt, bglu, cw, cbias, wo, wo, lng, lnb)


def kernel(x, c, ctx, c_ctx, w_ada, b_ada, w_in, s5_a_re, s5_a_im, s5_log_dt, s5_b_re, s5_b_im,
           s5_c_re, s5_c_im, s5_d, w_glu, b_glu, conv_w, conv_b, w_out, ln_g, ln_b):
    bsz, n, d = x.shape
    n_ctx = ctx.shape[1]
    depth = w_in.shape[0]
    assert bsz == 1 and d == D and n == NCL * T and n_ctx == NCC * T
    alpha = (2.0 * depth) ** 0.25

    condT = jnp.stack([c[0], c_ctx], axis=1)
    mod = _adaln(condT, w_ada, b_ada).reshape(depth * 2, 3, D)
    w1t, w2, acoef = _s5prep(s5_a_re, s5_a_im, s5_log_dt, s5_b_re, s5_b_im, s5_c_re, s5_c_im)
    dcol = jnp.tile(s5_d.reshape(depth, G, 1, H), (1, 1, T, 1)).reshape(depth, G, TH, 1)
    w_row, w_t = _cast_split(w_in, 2 * DS)
    out_w = (_cast(w_glu, True, DS), b_glu.reshape(depth, DS, 1), conv_w, conv_b.reshape(depth, 1, DS),
             _cast(w_out, False, 256), ln_g.reshape(depth, 1, D), ln_b.reshape(depth, 1, D))

    x3 = x.reshape(n // T, T, D)
    c3 = ctx.reshape(n_ctx // T, T, D)
    tb = 4
    for i in range(depth):
        last = i == depth - 1
        ut, szt, s, bg, szc = _inproj(x3, mod, w_t, w_row, i, 2 * i, CBL, tb)
        utc2, *ctx_rest = _inproj(c3, mod, w_t, w_row, i, 2 * i + 1, NCC, T, u_only=last)
        utc = jnp.pad(utc2.reshape(DS, T, NCC).transpose(1, 0, 2),
                      ((0, 0), (0, 0), (0, NCP - NCT)))
        yt, ytc = _s5(ut, utc, w1t, w2, acoef, dcol, i, ctx_out=not last)
        x3 = _outproj(x3, mod, yt, szt, s, bg, szc, out_w, i, 2 * i, CBL, tb, GRID_W // T, alpha)
        if not last:
            ytc2 = ytc[:, :, 0:NCC].transpose(1, 0, 2).reshape(DS, n_ctx)
            c3 = _outproj(c3, mod, ytc2, *ctx_rest, out_w, i, 2 * i + 1, NCC, T, NCC, alpha)
    return x3.reshape(bsz, n, D)
```

```python
import functools

import jax
import jax.numpy as jnp
from jax import lax
from jax.experimental import pallas as pl
from jax.experimental.pallas import tpu as pltpu

D = 2048
DS = 1024
G = 64
H = 16
P = 64
T = 16
TH = T * H
GRID_W = 64
LN_EPS = 1e-6
LANES = 128
VMEM_LIMIT = 60 * 1024 * 1024

F32 = jnp.float32
BF16 = jnp.bfloat16
NT_DIMS = (((1,), (1,)), ((), ()))
TN_DIMS = (((0,), (0,)), ((), ()))


def _cparams(sem):
    return pltpu.CompilerParams(dimension_semantics=sem, vmem_limit_bytes=VMEM_LIMIT)


def _layer_spec(shape, *idx):
    return pl.BlockSpec((None,) + shape, lambda *_: idx, pipeline_mode=pl.Buffered(1))


def _silu(v):
    return v * jax.nn.sigmoid(v)


def _adaln_kernel(condT_ref, w_ref, b_ref, o_ref):
    w = w_ref[0]
    for i in range(2):
        s = _silu(condT_ref[:, i:i + 1])
        o_ref[0, i:i + 1, :] = jnp.sum(w * s, axis=0, keepdims=True) + b_ref[0]


def _adaln(condT, w_ada, b_ada):
    depth, _, n3 = w_ada.shape
    tn = 512
    return pl.pallas_call(
        _adaln_kernel,
        grid=(depth, n3 // tn),
        in_specs=[pl.BlockSpec((D, 2), lambda l, j: (0, 0)),
                  pl.BlockSpec((1, D, tn), lambda l, j: (l, 0, j)),
                  pl.BlockSpec((1, 1, tn), lambda l, j: (l, 0, j))],
        out_specs=pl.BlockSpec((1, 2, tn), lambda l, j: (l, 0, j)),
        out_shape=jax.ShapeDtypeStruct((depth, 2, n3), F32),
        compiler_params=_cparams(("parallel", "parallel")),
        name="adaln",
    )(condT, w_ada, b_ada.reshape(depth, 1, n3))


def _cast_split_kernel(w_ref, wrow_ref, wt_ref, *, nt):
    w = w_ref[0]
    wt_ref[0] = w[:, 0:nt].T.astype(BF16)
    wrow_ref[0] = w[:, nt:].astype(BF16)


def _cast_split(w, nt, n_layers, tr=256):
    _, rows, cols = w.shape
    depth = n_layers
    return pl.pallas_call(
        functools.partial(_cast_split_kernel, nt=nt),
        grid=(depth, rows // tr),
        in_specs=[pl.BlockSpec((1, tr, cols), lambda l, r: (l, r, 0))],
        out_specs=[pl.BlockSpec((1, tr, cols - nt), lambda l, r: (l, r, 0)),
                   pl.BlockSpec((1, nt, tr), lambda l, r: (l, 0, r))],
        out_shape=[jax.ShapeDtypeStruct((depth, rows, cols - nt), BF16),
                   jax.ShapeDtypeStruct((depth, nt, rows), BF16)],
        compiler_params=_cparams(("parallel", "parallel")),
        name="cast_split",
    )(w)


def _cast_kernel(w_ref, o_ref, *, transpose):
    w = w_ref[0]
    o_ref[0] = (w.T if transpose else w).astype(BF16)


def _cast(w, transpose, tr):
    depth, rows, cols = w.shape
    out_block = (1, cols, tr) if transpose else (1, tr, cols)
    out_map = (lambda l, r: (l, 0, r)) if transpose else (lambda l, r: (l, r, 0))
    return pl.pallas_call(
        functools.partial(_cast_kernel, transpose=transpose),
        grid=(depth, rows // tr),
        in_specs=[pl.BlockSpec((1, tr, cols), lambda l, r: (l, r, 0))],
        out_specs=pl.BlockSpec(out_block, out_map),
        out_shape=jax.ShapeDtypeStruct((depth, cols, rows) if transpose else (depth, rows, cols), BF16),
        compiler_params=_cparams(("parallel", "parallel")),
        name="cast_t" if transpose else "cast",
    )(w)


GP = 8
PRM_ROWS = 72


def _s5prep_kernel(prm_ref, w1t_ref, w2_ref, acoef_ref, kt_scr):
    lane = lax.broadcasted_iota(jnp.int32, (1, LANES), 1)
    fwd_lane = lane < P
    krow = lax.broadcasted_iota(jnp.int32, (T, 1), 0)
    fwd_row = lax.broadcasted_iota(jnp.int32, (2 * P, 1), 0) < P
    col256 = lax.broadcasted_iota(jnp.int32, (T, TH), 1)
    r16 = lax.broadcasted_iota(jnp.int32, (T, TH), 0)
    e_rep = (col256 % H == r16).astype(BF16)
    e_blk = (col256 // H == r16).astype(BF16)
    colblk = lax.broadcasted_iota(jnp.int32, (TH, TH), 1) // H

    def rows16(tab):
        return jnp.broadcast_to(tab[:, None, :], (T, H, tab.shape[1])).reshape(TH, tab.shape[1])

    def tile16(mat):
        return jnp.broadcast_to(mat[None], (T, H, mat.shape[1])).reshape(TH, mat.shape[1])

    def bdot(a, b):
        return jnp.dot(a.astype(BF16), b.astype(BF16), preferred_element_type=F32)

    def expand(tab, e):
        return lax.dot_general(tab.astype(BF16), e, TN_DIMS, preferred_element_type=F32)

    for gi in range(GP):
        ar, ai = prm_ref[0, gi, 0:1, :], prm_ref[0, gi, 1:2, :]
        dt = jnp.exp(prm_ref[0, gi, 2:3, :])
        la, th = ar * dt, ai * dt
        btr, bti = prm_ref[0, gi, 8:24, :], prm_ref[0, gi, 24:40, :]
        cr, ci = tile16(prm_ref[0, gi, 40:56, :]), tile16(prm_ref[0, gi, 56:72, :])

        mag1 = jnp.exp(la)
        a1r, a1i = mag1 * jnp.cos(th), mag1 * jnp.sin(th)
        squares = [(a1r, a1i)]
        for _ in range(4):
            sr, si = squares[-1]
            squares.append((sr * sr - si * si, 2.0 * sr * si))

        def powrow(kf, kb):
            k = jnp.where(fwd_lane, kf, kb)
            pr, pi = jnp.ones(k.shape, F32), jnp.zeros(k.shape, F32)
            for b, (sr, si) in enumerate(squares):
                bit = ((k >> b) & 1) == 1
                pr, pi = jnp.where(bit, pr * sr - pi * si, pr), jnp.where(bit, pr * si + pi * sr, pi)
            return pr, pi

        def ctab(kf, kb):
            pr, pi = powrow(kf, kb)
            pr, pi = rows16(pr), rows16(pi)
            return cr * pr - ci * pi, cr * pi + ci * pr

        zr, zi = a1r - 1.0, a1i
        den = ar * ar + ai * ai
        fr, fi = (zr * ar + zi * ai) / den, (zi * ar - zr * ai) / den
        bbr_x = expand(fr * btr - fi * bti, e_rep)
        bbi_x = expand(fr * bti + fi * btr, e_rep)

        ckr, cki = ctab(krow, (T - 1) - krow)
        zero = jnp.zeros_like(bbr_x)
        kf = bdot(ckr, jnp.where(fwd_row, bbr_x, zero)) - bdot(cki, jnp.where(fwd_row, bbi_x, zero))
        kb = bdot(ckr, jnp.where(fwd_row, zero, bbr_x)) - bdot(cki, jnp.where(fwd_row, zero, bbi_x))
        kt_scr[0:TH - H, :] = kb[0:TH - H]
        kt_scr[TH - H:TH, :] = kb[TH - H:TH] + kf[0:H]
        kt_scr[TH:2 * TH - H, :] = kf[H:TH]
        m = jnp.zeros((TH, TH), F32)
        for s in range(T):
            win = kt_scr[(T - 1 - s) * H:(T - 1 - s) * H + TH, :]
            m = jnp.where(colblk == s, win, m)
        w1t_ref[0, gi, 0:TH, :] = m.astype(BF16)

        pir, pii = powrow((T - 1) - krow, krow)
        pxr, pxi = expand(pir, e_blk), expand(pii, e_blk)
        w1t_ref[0, gi, TH:TH + 2 * P, :] = (pxr * bbr_x - pxi * bbi_x).astype(BF16)
        w1t_ref[0, gi, TH + 2 * P:2 * TH, :] = (pxr * bbi_x + pxi * bbr_x).astype(BF16)

        cor, coi = ctab(krow + 1, T - krow)
        w2_ref[0, gi, :, 0:LANES] = cor.astype(BF16)
        w2_ref[0, gi, :, LANES:2 * LANES] = (-coi).astype(BF16)

        acoef_ref[0, 0, gi:gi + 1, :] = squares[4][0]
        acoef_ref[0, 1, gi:gi + 1, :] = squares[4][1]


def _s5prep(a_re, a_im, log_dt, b_re, b_im, c_re, c_im):
    depth = a_re.shape[0]

    def dircat(v):
        return jnp.concatenate([v[:, 0], v[:, 1]], axis=-1)

    ld = jnp.broadcast_to(log_dt[..., None, None], log_dt.shape + (1, P))
    pad = jnp.zeros((depth, G, 5, 2 * P), F32)
    prm = jnp.concatenate([dircat(a_re[:, :, :, None, :]), dircat(a_im[:, :, :, None, :]), dircat(ld), pad,
                           dircat(jnp.swapaxes(b_re, -1, -2)), dircat(jnp.swapaxes(b_im, -1, -2)),
                           dircat(c_re), dircat(c_im)], axis=2)
    return pl.pallas_call(
        _s5prep_kernel,
        grid=(depth, G // GP),
        in_specs=[pl.BlockSpec((1, GP, PRM_ROWS, 2 * P), lambda l, g: (l, g, 0, 0))],
        out_specs=[pl.BlockSpec((1, GP, 2 * TH, TH), lambda l, g: (l, g, 0, 0)),
                   pl.BlockSpec((1, GP, TH, TH), lambda l, g: (l, g, 0, 0)),
                   pl.BlockSpec((1, 2, GP, 2 * P), lambda l, g: (l, 0, g, 0))],
        out_shape=[jax.ShapeDtypeStruct((depth, G, 2 * TH, TH), BF16),
                   jax.ShapeDtypeStruct((depth, G, TH, TH), BF16),
                   jax.ShapeDtypeStruct((depth, 2, G, 2 * P), F32)],
        scratch_shapes=[pltpu.VMEM((2 * TH, TH), F32)],
        compiler_params=_cparams(("parallel", "parallel")),
        name="s5prep",
    )(prm)


def _plane_copies(hbm3, buf, sem, step, slot, *, cb, tb, ntt, to_hbm):
    b, tt = step // ntt, step % ntt
    copies = []
    for t in range(tb):
        hbm = hbm3.at[pl.ds(b * cb, cb), tt * tb + t, :]
        vm = buf.at[slot, t]
        copies.append(pltpu.make_async_copy(vm, hbm, sem.at[slot, t]) if to_hbm
                      else pltpu.make_async_copy(hbm, vm, sem.at[slot, t]))
    return copies


def _fetch_planes(x_hbm, xbuf, xsem, step, nsteps, **kw):
    slot = step % 2
    mk = functools.partial(_plane_copies, x_hbm, xbuf, xsem, to_hbm=False, **kw)

    @pl.when(step == 0)
    def _():
        for cp in mk(step, slot):
            cp.start()

    @pl.when(step + 1 < nsteps)
    def _():
        for cp in mk(step + 1, 1 - slot):
            cp.start()

    for cp in mk(step, slot):
        cp.wait()
    return slot


def _layernorm_rows(v):
    mu = jnp.mean(v, axis=-1, keepdims=True)
    vc = v - mu
    var = jnp.mean(vc * vc, axis=-1, keepdims=True)
    return vc * lax.rsqrt(var + LN_EPS)


def _inproj_kernel(*refs, cb, tb, nb, ntt, u_only, cast_next):
    if u_only:
        x_hbm, mod_ref, wut_ref, ut_ref, xbuf, xsem, h_scr = refs
    elif cast_next:
        (x_hbm, mod_ref, wut_ref, wzt_ref, wv_ref, wb_ref, wc_ref, wzc_ref, wnext_ref,
         ut_ref, szt_ref, s_ref, bg_ref, szc_ref, wrow_next_ref, wt_next_ref, xbuf, xsem, h_scr) = refs
        wn = wnext_ref[...]
        wt_next_ref[...] = wn[:, 0:2 * DS].T.astype(BF16)
        wrow_next_ref[...] = wn[:, 2 * DS:].astype(BF16)
    else:
        (x_hbm, mod_ref, wut_ref, wzt_ref, wv_ref, wb_ref, wc_ref, wzc_ref,
         ut_ref, szt_ref, s_ref, bg_ref, szc_ref, xbuf, xsem, h_scr) = refs
    step = pl.program_id(0) * ntt + pl.program_id(1)
    slot = _fetch_planes(x_hbm, xbuf, xsem, step, nb * ntt, cb=cb, tb=tb, ntt=ntt)
    shift, scale = mod_ref[0:1, :], mod_ref[1:2, :]
    for t in range(tb):
        hn = _layernorm_rows(xbuf[slot, t]) * (1.0 + scale) + shift
        h_scr[t * cb:(t + 1) * cb, :] = hn.astype(BF16)
    h = h_scr[...]
    ut_ref[...] = lax.dot_general(wut_ref[...], h, NT_DIMS, preferred_element_type=F32).astype(BF16)
    if u_only:
        return
    szt_ref[...] = _silu(lax.dot_general(wzt_ref[...], h, NT_DIMS, preferred_element_type=F32)).astype(BF16)
    v = jnp.dot(h, wv_ref[...], preferred_element_type=F32)
    cg = jnp.dot(h, wc_ref[...], preferred_element_type=F32)
    s_ref[...] = (cg * v).astype(BF16)
    bg_ref[...] = jnp.dot(h, wb_ref[...], preferred_element_type=F32).astype(BF16)
    szc_ref[...] = _silu(jnp.dot(h, wzc_ref[...], preferred_element_type=F32)).astype(BF16)


def _inproj(x3, mod, wt, wrow, mod_idx, cb, tb, u_only=False, w_next=None, next_layer=0):
    nc = x3.shape[0]
    n = nc * T
    nb, ntt = nc // cb, T // tb
    r = cb * tb
    col = lambda b, tt: (0, b * ntt + tt)
    row = lambda b, tt: (b * ntt + tt, 0)
    in_specs = [pl.BlockSpec(memory_space=pl.ANY), _layer_spec((3, D), mod_idx, 0, 0),
                _layer_spec((DS, D), 0, 0, 0)]
    out_specs = [pl.BlockSpec((DS, r), col)]
    out_shape = [jax.ShapeDtypeStruct((DS, n), BF16)]
    args = [x3, mod, wt]
    if not u_only:
        in_specs += [_layer_spec((DS, D), 0, 1, 0)] + [_layer_spec((D, DS), 0, 0, k) for k in range(4)]
        out_specs += [pl.BlockSpec((DS, r), col)] + [pl.BlockSpec((r, DS), row)] * 3
        out_shape += [jax.ShapeDtypeStruct((DS, n), BF16)] + [jax.ShapeDtypeStruct((n, DS), BF16)] * 3
        args += [wt, wrow, wrow, wrow, wrow]
    if w_next is not None:
        wr = D // (nb * ntt)
        in_specs += [pl.BlockSpec((None, wr, 6 * DS), lambda b, tt: (next_layer, b * ntt + tt, 0))]
        out_specs += [pl.BlockSpec((None, wr, 4 * DS), lambda b, tt: (0, b * ntt + tt, 0)),
                      pl.BlockSpec((None, 2 * DS, wr), lambda b, tt: (0, 0, b * ntt + tt))]
        out_shape += [jax.ShapeDtypeStruct((1, D, 4 * DS), BF16), jax.ShapeDtypeStruct((1, 2 * DS, D), BF16)]
        args += [w_next]
    return pl.pallas_call(
        functools.partial(_inproj_kernel, cb=cb, tb=tb, nb=nb, ntt=ntt, u_only=u_only,
                          cast_next=w_next is not None),
        grid=(nb, ntt),
        in_specs=in_specs,
        out_specs=out_specs,
        out_shape=out_shape,
        scratch_shapes=[pltpu.VMEM((2, tb, cb, D), F32), pltpu.SemaphoreType.DMA((2, tb)),
                        pltpu.VMEM((r, D), BF16)],
        compiler_params=_cparams(("arbitrary", "arbitrary")),
        name="inproj_u" if u_only else "inproj",
    )(*args)


GB = 8
NCL = 512
NCC = 16
NCT = NCL + NCC
NCP = 640
CBL = 128


def _s5_kernel(ut_ref, utc_ref, w1t_ref, w2_ref, acoef_ref, dcol_ref, wo_ref, yt_ref, ytc_ref, wo_bf_ref,
               sre_scr, sim_scr, hre_scr, him_scr, yloc_scr, *, ctx_out):
    nbl = NCL // CBL
    wo_bf_ref[...] = wo_ref[...].astype(BF16)
    for gl in range(GB):
        rows = slice(gl * H, (gl + 1) * H)
        planes = []
        for t in range(T):
            lat = [ut_ref[rows, (b * T + t) * CBL:(b * T + t + 1) * CBL] for b in range(nbl)]
            planes.append(jnp.concatenate(lat + [utc_ref[t, rows, :]], axis=1))
        z = jnp.concatenate(planes, axis=0)
        r1 = jnp.dot(w1t_ref[gl], z, preferred_element_type=F32)
        yloc_scr[gl] = r1[0:TH] + dcol_ref[gl] * z.astype(F32)
        st = r1[TH:2 * TH].T
        sre_scr[pl.ds(gl, NCP, stride=GB), :] = st[:, 0:LANES]
        sim_scr[pl.ds(gl, NCP, stride=GB), :] = st[:, LANES:2 * LANES]

    ar, ai = acoef_ref[0], acoef_ref[1]
    fwd_lane = lax.broadcasted_iota(jnp.int32, (GB, LANES), 1) < P
    bwd_lane = jnp.logical_not(fwd_lane)
    hre_scr[NCT * GB:NCP * GB, :] = jnp.zeros(((NCP - NCT) * GB, LANES), F32)
    him_scr[NCT * GB:NCP * GB, :] = jnp.zeros(((NCP - NCT) * GB, LANES), F32)

    def step(k, carry, fwd_shift):
        hr, hi = carry
        cf = k + fwd_shift
        cb_ = NCT - 1 - k
        rf = pl.ds(pl.multiple_of(cf * GB, GB), GB)
        rb = pl.ds(pl.multiple_of(cb_ * GB, GB), GB)
        pltpu.store(hre_scr.at[rf, :], hr, mask=fwd_lane)
        pltpu.store(hre_scr.at[rb, :], hr, mask=bwd_lane)
        pltpu.store(him_scr.at[rf, :], hi, mask=fwd_lane)
        pltpu.store(him_scr.at[rb, :], hi, mask=bwd_lane)
        sr = jnp.where(fwd_lane, sre_scr[rf, :], sre_scr[rb, :])
        si = jnp.where(fwd_lane, sim_scr[rf, :], sim_scr[rb, :])
        return ar * hr - ai * hi + sr, ar * hi + ai * hr + si

    zero = jnp.zeros((GB, LANES), F32)
    carry = lax.fori_loop(0, NCC, functools.partial(step, fwd_shift=NCL), (zero, zero))
    lax.fori_loop(NCC, NCT, functools.partial(step, fwd_shift=-NCC), carry, unroll=4)

    for gl in range(GB):
        rows = slice(gl * H, (gl + 1) * H)
        sel = pl.ds(gl, NCP, stride=GB)
        hp = jnp.concatenate([hre_scr[sel, :], him_scr[sel, :]], axis=1)
        y = yloc_scr[gl] + lax.dot_general(w2_ref[gl], hp.astype(BF16), NT_DIMS, preferred_element_type=F32)
        yb = y.astype(BF16)
        for t in range(T):
            for b in range(nbl):
                yt_ref[rows, (b * T + t) * CBL:(b * T + t + 1) * CBL] = \
                    yb[t * H:(t + 1) * H, b * CBL:(b + 1) * CBL]
            if ctx_out:
                ytc_ref[t, rows, :] = yb[t * H:(t + 1) * H, NCL:NCP]
    if not ctx_out:
        ytc_ref[...] = jnp.zeros(ytc_ref.shape, BF16)


def _s5(ut, utc, w1t, w2, acoef, dcol, w_out, layer, ctx_out):
    n = ut.shape[1]
    ncx = NCP - NCL
    wr = 2 * DS // (G // GB)
    state = pltpu.VMEM((NCP * GB, LANES), F32)
    return pl.pallas_call(
        functools.partial(_s5_kernel, ctx_out=ctx_out),
        grid=(G // GB,),
        in_specs=[pl.BlockSpec((GB * H, n), lambda i: (i, 0)),
                  pl.BlockSpec((T, GB * H, ncx), lambda i: (0, i, 0)),
                  pl.BlockSpec((None, GB, 2 * TH, TH), lambda i: (layer, i, 0, 0)),
                  pl.BlockSpec((None, GB, TH, TH), lambda i: (layer, i, 0, 0)),
                  pl.BlockSpec((None, 2, GB, LANES), lambda i: (layer, 0, i, 0)),
                  pl.BlockSpec((None, GB, TH, 1), lambda i: (layer, i, 0, 0)),
                  pl.BlockSpec((None, wr, D), lambda i: (layer, i, 0))],
        out_specs=[pl.BlockSpec((GB * H, n), lambda i: (i, 0)),
                   pl.BlockSpec((T, GB * H, ncx), lambda i: (0, i, 0)),
                   pl.BlockSpec((None, wr, D), lambda i: (0, i, 0))],
        out_shape=[jax.ShapeDtypeStruct((DS, n), BF16), jax.ShapeDtypeStruct((T, DS, ncx), BF16),
                   jax.ShapeDtypeStruct((1, 2 * DS, D), BF16)],
        scratch_shapes=[state, state, state, state, pltpu.VMEM((GB, TH, NCP), F32)],
        compiler_params=_cparams(("parallel",)),
        name="s5",
    )(ut, utc, w1t, w2, acoef, dcol, w_out)


def _outproj_kernel(x_hbm, mod_ref, yt_ref, szt_ref, s_ref, sprev_ref, snext_ref, bg_ref, szc_ref,
                    wgt_ref, bglu_ref, cw_ref, cb_ref, wo5_ref, wocv_ref, lng_ref, lnb_ref,
                    o_hbm, xbuf, xsem, obuf, osem, *, cb, tb, nb, ntt, row_chunks, alpha):
    tt = pl.program_id(1)
    step = pl.program_id(0) * ntt + tt
    nsteps = nb * ntt
    slot = _fetch_planes(x_hbm, xbuf, xsem, step, nsteps, cb=cb, tb=tb, ntt=ntt)
    g = jax.nn.gelu(yt_ref[...].astype(F32))
    gate = jnp.dot(wgt_ref[...], g.astype(BF16), preferred_element_type=F32) + bglu_ref[...]
    o5t = (g * jax.nn.sigmoid(gate) * szt_ref[...].astype(F32)).astype(BF16)
    sub = lax.dot_general(o5t, wo5_ref[...], TN_DIMS, preferred_element_type=F32)
    crow = lax.broadcasted_iota(jnp.int32, (cb, 1), 0) % row_chunks
    sp = sprev_ref[...].astype(F32)
    sn = snext_ref[...].astype(F32)
    sp_wrap = jnp.where(crow != 0, pltpu.roll(sp, 1, 0), 0.0)
    sn_wrap = jnp.where(crow != row_chunks - 1, pltpu.roll(sn, cb - 1, 0), 0.0)
    sp = jnp.where(tt == 0, sp_wrap, sp)
    sn = jnp.where(tt == ntt - 1, sn_wrap, sn)
    s_all = jnp.concatenate([sp, s_ref[...].astype(F32), sn], axis=0)
    r = cb * tb
    conv = (cw_ref[0:1, :] * s_all[0:r] + cw_ref[1:2, :] * s_all[cb:cb + r]
            + cw_ref[2:3, :] * s_all[2 * cb:2 * cb + r] + cb_ref[...])
    ocv = (bg_ref[...].astype(F32) * conv * szc_ref[...].astype(F32)).astype(BF16)
    sub = sub + jnp.dot(ocv, wocv_ref[...], preferred_element_type=F32)
    gate_vec = mod_ref[2:3, :]
    for t in range(tb):
        v = alpha * xbuf[slot, t] + gate_vec * sub[t * cb:(t + 1) * cb]
        obuf[slot, t] = _layernorm_rows(v) * lng_ref[...] + lnb_ref[...]
    put = functools.partial(_plane_copies, o_hbm, obuf, osem, cb=cb, tb=tb, ntt=ntt, to_hbm=True)

    @pl.when(step > 0)
    def _():
        for cp in put(step - 1, 1 - slot):
            cp.wait()

    for cp in put(step, slot):
        cp.start()

    @pl.when(step == nsteps - 1)
    def _():
        for cp in put(step, slot):
            cp.wait()


def _outproj(x3, mod, yt, szt, s, bg, szc, wo, wts, layer, mod_idx, cb, tb, row_chunks, alpha):
    nc = x3.shape[0]
    nb, ntt = nc // cb, T // tb
    r = cb * tb
    col = lambda b, tt: (0, b * ntt + tt)
    row = lambda b, tt: (b * ntt + tt, 0)
    prev_plane = lambda b, tt: (b * T + (tt * tb + T - 1) % T, 0)
    next_plane = lambda b, tt: (b * T + (tt * tb + tb) % T, 0)
    wgt, bglu, cw, cbias, lng, lnb = wts
    planes = pltpu.VMEM((2, tb, cb, D), F32)
    return pl.pallas_call(
        functools.partial(_outproj_kernel, cb=cb, tb=tb, nb=nb, ntt=ntt, row_chunks=row_chunks, alpha=alpha),
        grid=(nb, ntt),
        in_specs=[pl.BlockSpec(memory_space=pl.ANY),
                  _layer_spec((3, D), mod_idx, 0, 0),
                  pl.BlockSpec((DS, r), col), pl.BlockSpec((DS, r), col),
                  pl.BlockSpec((r, DS), row),
                  pl.BlockSpec((cb, DS), prev_plane), pl.BlockSpec((cb, DS), next_plane),
                  pl.BlockSpec((r, DS), row), pl.BlockSpec((r, DS), row),
                  _layer_spec((DS, DS), layer, 0, 0), _layer_spec((DS, 1), layer, 0, 0),
                  _layer_spec((3, DS), layer, 0, 0), _layer_spec((1, DS), layer, 0, 0),
                  _layer_spec((DS, D), 0, 0, 0), _layer_spec((DS, D), 0, 1, 0),
                  _layer_spec((1, D), layer, 0, 0), _layer_spec((1, D), layer, 0, 0)],
        out_specs=pl.BlockSpec(memory_space=pl.ANY),
        out_shape=jax.ShapeDtypeStruct(x3.shape, F32),
        scratch_shapes=[planes, pltpu.SemaphoreType.DMA((2, tb)), planes, pltpu.SemaphoreType.DMA((2, tb))],
        compiler_params=_cparams(("arbitrary", "arbitrary")),
        name="outproj",
    )(x3, mod, yt, szt, s, s, s, bg, szc, wgt, bglu, cw, cbias, wo, wo, lng, lnb)


def kernel(x, c, ctx, c_ctx, w_ada, b_ada, w_in, s5_a_re, s5_a_im, s5_log_dt, s5_b_re, s5_b_im,
           s5_c_re, s5_c_im, s5_d, w_glu, b_glu, conv_w, conv_b, w_out, ln_g, ln_b):
    bsz, n, d = x.shape
    n_ctx = ctx.shape[1]
    depth = w_in.shape[0]
    assert bsz == 1 and d == D and n == NCL * T and n_ctx == NCC * T
    alpha = (2.0 * depth) ** 0.25

    condT = jnp.stack([c[0], c_ctx], axis=1)
    mod = _adaln(condT, w_ada, b_ada).reshape(depth * 2, 3, D)
    w1t, w2, acoef = _s5prep(s5_a_re, s5_a_im, s5_log_dt, s5_b_re, s5_b_im, s5_c_re, s5_c_im)
    dcol = jnp.tile(s5_d.reshape(depth, G, 1, H), (1, 1, T, 1)).reshape(depth, G, TH, 1)
    w_row, w_t = _cast_split(w_in, 2 * DS, 1)
    out_w = (_cast(w_glu, True, DS), b_glu.reshape(depth, DS, 1), conv_w, conv_b.reshape(depth, 1, DS),
             ln_g.reshape(depth, 1, D), ln_b.reshape(depth, 1, D))

    x3 = x.reshape(n // T, T, D)
    c3 = ctx.reshape(n_ctx // T, T, D)
    tb = 4
    for i in range(depth):
        last = i == depth - 1
        ut, szt, s, bg, szc, *w_next = _inproj(x3, mod, w_t, w_row, 2 * i, CBL, tb,
                                               w_next=None if last else w_in, next_layer=i + 1)
        utc2, *ctx_rest = _inproj(c3, mod, w_t, w_row, 2 * i + 1, NCC, T, u_only=last)
        utc = jnp.pad(utc2.reshape(DS, T, NCC).transpose(1, 0, 2),
                      ((0, 0), (0, 0), (0, NCP - NCT)))
        yt, ytc, wo = _s5(ut, utc, w1t, w2, acoef, dcol, w_out, i, ctx_out=not last)
        x3 = _outproj(x3, mod, yt, szt, s, bg, szc, wo, out_w, i, 2 * i, CBL, tb, GRID_W // T, alpha)
        if not last:
            ytc2 = ytc[:, :, 0:NCC].transpose(1, 0, 2).reshape(DS, n_ctx)
            c3 = _outproj(c3, mod, ytc2, *ctx_rest, wo, out_w, i, 2 * i + 1, NCC, T, NCC, alpha)
            w_row, w_t = w_next
    return x3.reshape(bsz, n, D)
```

```python
import functools

import jax
import jax.numpy as jnp
from jax import lax
from jax.experimental import pallas as pl
from jax.experimental.pallas import tpu as pltpu

D = 2048
DS = 1024
G = 64
H = 16
P = 64
T = 16
TH = T * H
GRID_W = 64
LN_EPS = 1e-6
LANES = 128
VMEM_LIMIT = 60 * 1024 * 1024

F32 = jnp.float32
BF16 = jnp.bfloat16
NT_DIMS = (((1,), (1,)), ((), ()))
TN_DIMS = (((0,), (0,)), ((), ()))


def _cparams(sem):
    return pltpu.CompilerParams(dimension_semantics=sem, vmem_limit_bytes=VMEM_LIMIT)


def _layer_spec(shape, *idx):
    return pl.BlockSpec((None,) + shape, lambda *_: idx, pipeline_mode=pl.Buffered(1))


def _silu(v):
    return v * jax.nn.sigmoid(v)


def _adaln_kernel(condT_ref, w_ref, b_ref, o_ref, s_scr):
    @pl.when(jnp.logical_and(pl.program_id(0) == 0, pl.program_id(1) == 0))
    def _():
        s_scr[...] = _silu(condT_ref[...])

    w = w_ref[0]
    for i in range(2):
        o_ref[0, i:i + 1, :] = jnp.sum(w * s_scr[:, i:i + 1], axis=0, keepdims=True) + b_ref[0]


def _adaln(condT, w_ada, b_ada):
    depth, _, n3 = w_ada.shape
    tn = 512
    return pl.pallas_call(
        _adaln_kernel,
        grid=(depth, n3 // tn),
        in_specs=[pl.BlockSpec((D, 2), lambda l, j: (0, 0)),
                  pl.BlockSpec((1, D, tn), lambda l, j: (l, 0, j)),
                  pl.BlockSpec((1, 1, tn), lambda l, j: (l, 0, j))],
        out_specs=pl.BlockSpec((1, 2, tn), lambda l, j: (l, 0, j)),
        out_shape=jax.ShapeDtypeStruct((depth, 2, n3), F32),
        scratch_shapes=[pltpu.VMEM((D, 2), F32)],
        compiler_params=_cparams(("arbitrary", "arbitrary")),
        name="adaln",
    )(condT, w_ada, b_ada.reshape(depth, 1, n3))


def _cast_split_kernel(w_ref, wrow_ref, wt_ref, *, nt):
    w = w_ref[0]
    wt_ref[0] = w[:, 0:nt].T.astype(BF16)
    wrow_ref[0] = w[:, nt:].astype(BF16)


def _cast_split(w, nt, n_layers, tr=256):
    _, rows, cols = w.shape
    depth = n_layers
    return pl.pallas_call(
        functools.partial(_cast_split_kernel, nt=nt),
        grid=(depth, rows // tr),
        in_specs=[pl.BlockSpec((1, tr, cols), lambda l, r: (l, r, 0))],
        out_specs=[pl.BlockSpec((1, tr, cols - nt), lambda l, r: (l, r, 0)),
                   pl.BlockSpec((1, nt, tr), lambda l, r: (l, 0, r))],
        out_shape=[jax.ShapeDtypeStruct((depth, rows, cols - nt), BF16),
                   jax.ShapeDtypeStruct((depth, nt, rows), BF16)],
        compiler_params=_cparams(("parallel", "parallel")),
        name="cast_split",
    )(w)


def _cast_kernel(w_ref, o_ref, *, transpose):
    w = w_ref[0]
    o_ref[0] = (w.T if transpose else w).astype(BF16)


def _cast(w, transpose, tr):
    depth, rows, cols = w.shape
    out_block = (1, cols, tr) if transpose else (1, tr, cols)
    out_map = (lambda l, r: (l, 0, r)) if transpose else (lambda l, r: (l, r, 0))
    return pl.pallas_call(
        functools.partial(_cast_kernel, transpose=transpose),
        grid=(depth, rows // tr),
        in_specs=[pl.BlockSpec((1, tr, cols), lambda l, r: (l, r, 0))],
        out_specs=pl.BlockSpec(out_block, out_map),
        out_shape=jax.ShapeDtypeStruct((depth, cols, rows) if transpose else (depth, rows, cols), BF16),
        compiler_params=_cparams(("parallel", "parallel")),
        name="cast_t" if transpose else "cast",
    )(w)


GP = 8
PRM_ROWS = 40


def _s5prep_kernel(prm_ref, bre_ref, bim_ref, w1t_ref, w2_ref, acoef_ref, kt_scr):
    lane = lax.broadcasted_iota(jnp.int32, (1, LANES), 1)
    fwd_lane = lane < P
    krow = lax.broadcasted_iota(jnp.int32, (T, 1), 0)
    fwd_row = lax.broadcasted_iota(jnp.int32, (2 * P, 1), 0) < P
    col256 = lax.broadcasted_iota(jnp.int32, (T, TH), 1)
    r16 = lax.broadcasted_iota(jnp.int32, (T, TH), 0)
    e_rep = (col256 % H == r16).astype(BF16)
    e_blk = (col256 // H == r16).astype(BF16)
    colblk = lax.broadcasted_iota(jnp.int32, (TH, TH), 1) // H

    def rows16(tab):
        return jnp.broadcast_to(tab[:, None, :], (T, H, tab.shape[1])).reshape(TH, tab.shape[1])

    def tile16(mat):
        return jnp.broadcast_to(mat[None], (T, H, mat.shape[1])).reshape(TH, mat.shape[1])

    def bdot(a, b):
        return jnp.dot(a.astype(BF16), b.astype(BF16), preferred_element_type=F32)

    def expand(tab, e):
        return lax.dot_general(tab.astype(BF16), e, TN_DIMS, preferred_element_type=F32)

    for gi in range(GP):
        ar, ai = prm_ref[0, gi, 0:1, :], prm_ref[0, gi, 1:2, :]
        dt = jnp.exp(prm_ref[0, gi, 2:3, :])
        la, th = ar * dt, ai * dt
        cr, ci = tile16(prm_ref[0, gi, 8:24, :]), tile16(prm_ref[0, gi, 24:40, :])
        b_x = [bdot(jnp.concatenate([b_ref[0, 0, gi], b_ref[0, 1, gi]], axis=0), e_rep)
               for b_ref in (bre_ref, bim_ref)]

        mag1 = jnp.exp(la)
        a1r, a1i = mag1 * jnp.cos(th), mag1 * jnp.sin(th)
        squares = [(a1r, a1i)]
        for _ in range(4):
            sr, si = squares[-1]
            squares.append((sr * sr - si * si, 2.0 * sr * si))

        zr, zi = a1r - 1.0, a1i
        den = ar * ar + ai * ai
        fr, fi = (zr * ar + zi * ai) / den, (zi * ar - zr * ai) / den

        def powrow(kf, kb, with_f):
            k = jnp.where(fwd_lane, kf, kb)
            if with_f:
                pr, pi = jnp.broadcast_to(fr, k.shape), jnp.broadcast_to(fi, k.shape)
            else:
                pr, pi = jnp.ones(k.shape, F32), jnp.zeros(k.shape, F32)
            for b, (sr, si) in enumerate(squares):
                bit = ((k >> b) & 1) == 1
                pr, pi = jnp.where(bit, pr * sr - pi * si, pr), jnp.where(bit, pr * si + pi * sr, pi)
            return pr, pi

        def ctab(kf, kb, with_f):
            pr, pi = powrow(kf, kb, with_f)
            pr, pi = rows16(pr), rows16(pi)
            return cr * pr - ci * pi, cr * pi + ci * pr

        bbr_x, bbi_x = b_x
        ckr, cki = ctab(krow, (T - 1) - krow, True)
        zero = jnp.zeros_like(bbr_x)
        kf = bdot(ckr, jnp.where(fwd_row, bbr_x, zero)) - bdot(cki, jnp.where(fwd_row, bbi_x, zero))
        kb = bdot(ckr, jnp.where(fwd_row, zero, bbr_x)) - bdot(cki, jnp.where(fwd_row, zero, bbi_x))
        kt_scr[0:TH - H, :] = kb[0:TH - H]
        kt_scr[TH - H:TH, :] = kb[TH - H:TH] + kf[0:H]
        kt_scr[TH:2 * TH - H, :] = kf[H:TH]
        m = jnp.zeros((TH, TH), F32)
        for s in range(T):
            win = kt_scr[(T - 1 - s) * H:(T - 1 - s) * H + TH, :]
            m = jnp.where(colblk == s, win, m)
        w1t_ref[0, gi, 0:TH, :] = m.astype(BF16)

        pir, pii = powrow((T - 1) - krow, krow, True)
        pxr, pxi = expand(pir, e_blk), expand(pii, e_blk)
        w1t_ref[0, gi, TH:TH + 2 * P, :] = (pxr * bbr_x - pxi * bbi_x).astype(BF16)
        w1t_ref[0, gi, TH + 2 * P:2 * TH, :] = (pxr * bbi_x + pxi * bbr_x).astype(BF16)

        cor, coi = ctab(krow + 1, T - krow, False)
        w2_ref[0, gi, :, 0:LANES] = cor.astype(BF16)
        w2_ref[0, gi, :, LANES:2 * LANES] = (-coi).astype(BF16)

        acoef_ref[0, 0, gi:gi + 1, :] = squares[4][0]
        acoef_ref[0, 1, gi:gi + 1, :] = squares[4][1]


def _s5prep(a_re, a_im, log_dt, b_re, b_im, c_re, c_im):
    depth = a_re.shape[0]

    def dircat(v):
        return jnp.concatenate([v[:, 0], v[:, 1]], axis=-1)

    ld = jnp.broadcast_to(log_dt[..., None, None], log_dt.shape + (1, P))
    pad = jnp.zeros((depth, G, 5, 2 * P), F32)
    prm = jnp.concatenate([dircat(a_re[:, :, :, None, :]), dircat(a_im[:, :, :, None, :]), dircat(ld), pad,
                           dircat(c_re), dircat(c_im)], axis=2)
    b_spec = pl.BlockSpec((1, 2, GP, P, H), lambda l, g: (l, 0, g, 0, 0))
    return pl.pallas_call(
        _s5prep_kernel,
        grid=(depth, G // GP),
        in_specs=[pl.BlockSpec((1, GP, PRM_ROWS, 2 * P), lambda l, g: (l, g, 0, 0)), b_spec, b_spec],
        out_specs=[pl.BlockSpec((1, GP, 2 * TH, TH), lambda l, g: (l, g, 0, 0)),
                   pl.BlockSpec((1, GP, TH, TH), lambda l, g: (l, g, 0, 0)),
                   pl.BlockSpec((1, 2, GP, 2 * P), lambda l, g: (l, 0, g, 0))],
        out_shape=[jax.ShapeDtypeStruct((depth, G, 2 * TH, TH), BF16),
                   jax.ShapeDtypeStruct((depth, G, TH, TH), BF16),
                   jax.ShapeDtypeStruct((depth, 2, G, 2 * P), F32)],
        scratch_shapes=[pltpu.VMEM((2 * TH, TH), F32)],
        compiler_params=_cparams(("parallel", "parallel")),
        name="s5prep",
    )(prm, b_re, b_im)


def _plane_copies(hbm3, buf, sem, step, slot, *, cb, tb, ntt, to_hbm):
    b, tt = step // ntt, step % ntt
    copies = []
    for t in range(tb):
        hbm = hbm3.at[pl.ds(b * cb, cb), tt * tb + t, :]
        vm = buf.at[slot, t]
        copies.append(pltpu.make_async_copy(vm, hbm, sem.at[slot, t]) if to_hbm
                      else pltpu.make_async_copy(hbm, vm, sem.at[slot, t]))
    return copies


def _fetch_planes(x_hbm, xbuf, xsem, step, nsteps, **kw):
    slot = step % 2
    mk = functools.partial(_plane_copies, x_hbm, xbuf, xsem, to_hbm=False, **kw)

    @pl.when(step == 0)
    def _():
        for cp in mk(step, slot):
            cp.start()

    @pl.when(step + 1 < nsteps)
    def _():
        for cp in mk(step + 1, 1 - slot):
            cp.start()

    for cp in mk(step, slot):
        cp.wait()
    return slot


def _layernorm_rows(v):
    mu = jnp.mean(v, axis=-1, keepdims=True)
    vc = v - mu
    var = jnp.mean(vc * vc, axis=-1, keepdims=True)
    return vc * lax.rsqrt(var + LN_EPS)


def _inproj_kernel(*refs, cb, tb, nb, ntt, u_only, cast_next):
    if u_only:
        x_hbm, mod_ref, wut_ref, ut_ref, xbuf, xsem, h_scr = refs
    elif cast_next:
        (x_hbm, mod_ref, wut_ref, wzt_ref, wv_ref, wb_ref, wc_ref, wzc_ref, wnext_ref,
         ut_ref, szt_ref, s_ref, bg_ref, szc_ref, wrow_next_ref, wt_next_ref, xbuf, xsem, h_scr) = refs
        wn = wnext_ref[...]
        wt_next_ref[...] = wn[:, 0:2 * DS].T.astype(BF16)
        wrow_next_ref[...] = wn[:, 2 * DS:].astype(BF16)
    else:
        (x_hbm, mod_ref, wut_ref, wzt_ref, wv_ref, wb_ref, wc_ref, wzc_ref,
         ut_ref, szt_ref, s_ref, bg_ref, szc_ref, xbuf, xsem, h_scr) = refs
    step = pl.program_id(0) * ntt + pl.program_id(1)
    slot = _fetch_planes(x_hbm, xbuf, xsem, step, nb * ntt, cb=cb, tb=tb, ntt=ntt)
    shift, scale = mod_ref[0:1, :], mod_ref[1:2, :]
    for t in range(tb):
        hn = _layernorm_rows(xbuf[slot, t]) * (1.0 + scale) + shift
        h_scr[t * cb:(t + 1) * cb, :] = hn.astype(BF16)
    h = h_scr[...]
    ut_ref[...] = lax.dot_general(wut_ref[...], h, NT_DIMS, preferred_element_type=F32).astype(BF16)
    if u_only:
        return
    szt_ref[...] = _silu(lax.dot_general(wzt_ref[...], h, NT_DIMS, preferred_element_type=F32)).astype(BF16)
    v = jnp.dot(h, wv_ref[...], preferred_element_type=F32)
    cg = jnp.dot(h, wc_ref[...], preferred_element_type=F32)
    s_ref[...] = (cg * v).astype(BF16)
    bg_ref[...] = jnp.dot(h, wb_ref[...], preferred_element_type=F32).astype(BF16)
    szc_ref[...] = _silu(jnp.dot(h, wzc_ref[...], preferred_element_type=F32)).astype(BF16)


def _inproj(x3, mod, wt, wrow, mod_idx, cb, tb, u_only=False, w_next=None, next_layer=0):
    nc = x3.shape[0]
    n = nc * T
    nb, ntt = nc // cb, T // tb
    r = cb * tb
    col = lambda b, tt: (0, b * ntt + tt)
    row = lambda b, tt: (b * ntt + tt, 0)
    in_specs = [pl.BlockSpec(memory_space=pl.ANY), _layer_spec((3, D), mod_idx, 0, 0),
                _layer_spec((DS, D), 0, 0, 0)]
    out_specs = [pl.BlockSpec((DS, r), col)]
    out_shape = [jax.ShapeDtypeStruct((DS, n), BF16)]
    args = [x3, mod, wt]
    if not u_only:
        in_specs += [_layer_spec((DS, D), 0, 1, 0)] + [_layer_spec((D, DS), 0, 0, k) for k in range(4)]
        out_specs += [pl.BlockSpec((DS, r), col)] + [pl.BlockSpec((r, DS), row)] * 3
        out_shape += [jax.ShapeDtypeStruct((DS, n), BF16)] + [jax.ShapeDtypeStruct((n, DS), BF16)] * 3
        args += [wt, wrow, wrow, wrow, wrow]
    if w_next is not None:
        wr = D // (nb * ntt)
        in_specs += [pl.BlockSpec((None, wr, 6 * DS), lambda b, tt: (next_layer, b * ntt + tt, 0))]
        out_specs += [pl.BlockSpec((None, wr, 4 * DS), lambda b, tt: (0, b * ntt + tt, 0)),
                      pl.BlockSpec((None, 2 * DS, wr), lambda b, tt: (0, 0, b * ntt + tt))]
        out_shape += [jax.ShapeDtypeStruct((1, D, 4 * DS), BF16), jax.ShapeDtypeStruct((1, 2 * DS, D), BF16)]
        args += [w_next]
    return pl.pallas_call(
        functools.partial(_inproj_kernel, cb=cb, tb=tb, nb=nb, ntt=ntt, u_only=u_only,
                          cast_next=w_next is not None),
        grid=(nb, ntt),
        in_specs=in_specs,
        out_specs=out_specs,
        out_shape=out_shape,
        scratch_shapes=[pltpu.VMEM((2, tb, cb, D), F32), pltpu.SemaphoreType.DMA((2, tb)),
                        pltpu.VMEM((r, D), BF16)],
        compiler_params=_cparams(("arbitrary", "arbitrary")),
        name="inproj_u" if u_only else "inproj",
    )(*args)


GB = 8
NCL = 512
NCC = 16
NCT = NCL + NCC
NCP = 640
CBL = 128


def _s5_kernel(ut_ref, utc_ref, w1t_ref, w2_ref, acoef_ref, dcol_ref, wo_ref, yt_ref, ytc_ref, wo_bf_ref,
               sre_scr, sim_scr, hre_scr, him_scr, yloc_scr, *, ctx_out):
    nbl = NCL // CBL
    wo_bf_ref[...] = wo_ref[...].astype(BF16)
    for gl in range(GB):
        rows = slice(gl * H, (gl + 1) * H)
        planes = []
        for t in range(T):
            lat = [ut_ref[rows, (b * T + t) * CBL:(b * T + t + 1) * CBL] for b in range(nbl)]
            planes.append(jnp.concatenate(lat + [utc_ref[t, rows, :]], axis=1))
        z = jnp.concatenate(planes, axis=0)
        r1 = jnp.dot(w1t_ref[gl], z, preferred_element_type=F32)
        yloc_scr[gl] = r1[0:TH] + dcol_ref[gl] * z.astype(F32)
        st = r1[TH:2 * TH].T
        sre_scr[pl.ds(gl, NCP, stride=GB), :] = st[:, 0:LANES]
        sim_scr[pl.ds(gl, NCP, stride=GB), :] = st[:, LANES:2 * LANES]

    ar, ai = acoef_ref[0], acoef_ref[1]
    fwd_lane = lax.broadcasted_iota(jnp.int32, (GB, LANES), 1) < P
    bwd_lane = jnp.logical_not(fwd_lane)
    hre_scr[NCT * GB:NCP * GB, :] = jnp.zeros(((NCP - NCT) * GB, LANES), F32)
    him_scr[NCT * GB:NCP * GB, :] = jnp.zeros(((NCP - NCT) * GB, LANES), F32)

    def step(k, carry, fwd_shift):
        hr, hi = carry
        cf = k + fwd_shift
        cb_ = NCT - 1 - k
        rf = pl.ds(pl.multiple_of(cf * GB, GB), GB)
        rb = pl.ds(pl.multiple_of(cb_ * GB, GB), GB)
        pltpu.store(hre_scr.at[rf, :], hr, mask=fwd_lane)
        pltpu.store(hre_scr.at[rb, :], hr, mask=bwd_lane)
        pltpu.store(him_scr.at[rf, :], hi, mask=fwd_lane)
        pltpu.store(him_scr.at[rb, :], hi, mask=bwd_lane)
        sr = jnp.where(fwd_lane, sre_scr[rf, :], sre_scr[rb, :])
        si = jnp.where(fwd_lane, sim_scr[rf, :], sim_scr[rb, :])
        return ar * hr - ai * hi + sr, ar * hi + ai * hr + si

    zero = jnp.zeros((GB, LANES), F32)
    carry = lax.fori_loop(0, NCC, functools.partial(step, fwd_shift=NCL), (zero, zero))
    lax.fori_loop(NCC, NCT, functools.partial(step, fwd_shift=-NCC), carry, unroll=4)

    for gl in range(GB):
        rows = slice(gl * H, (gl + 1) * H)
        sel = pl.ds(gl, NCP, stride=GB)
        hp = jnp.concatenate([hre_scr[sel, :], him_scr[sel, :]], axis=1)
        y = yloc_scr[gl] + lax.dot_general(w2_ref[gl], hp.astype(BF16), NT_DIMS, preferred_element_type=F32)
        yb = y.astype(BF16)
        for t in range(T):
            for b in range(nbl):
                yt_ref[rows, (b * T + t) * CBL:(b * T + t + 1) * CBL] = \
                    yb[t * H:(t + 1) * H, b * CBL:(b + 1) * CBL]
            if ctx_out:
                ytc_ref[t, rows, :] = yb[t * H:(t + 1) * H, NCL:NCP]
    if not ctx_out:
        ytc_ref[...] = jnp.zeros(ytc_ref.shape, BF16)


def _s5(ut, utc, w1t, w2, acoef, dcol, w_out, layer, ctx_out):
    n = ut.shape[1]
    ncx = NCP - NCL
    wr = 2 * DS // (G // GB)
    state = pltpu.VMEM((NCP * GB, LANES), F32)
    return pl.pallas_call(
        functools.partial(_s5_kernel, ctx_out=ctx_out),
        grid=(G // GB,),
        in_specs=[pl.BlockSpec((GB * H, n), lambda i: (i, 0)),
                  pl.BlockSpec((T, GB * H, ncx), lambda i: (0, i, 0)),
                  pl.BlockSpec((None, GB, 2 * TH, TH), lambda i: (layer, i, 0, 0)),
                  pl.BlockSpec((None, GB, TH, TH), lambda i: (layer, i, 0, 0)),
                  pl.BlockSpec((None, 2, GB, LANES), lambda i: (layer, 0, i, 0)),
                  pl.BlockSpec((None, GB, TH, 1), lambda i: (layer, i, 0, 0)),
                  pl.BlockSpec((None, wr, D), lambda i: (layer, i, 0))],
        out_specs=[pl.BlockSpec((GB * H, n), lambda i: (i, 0)),
                   pl.BlockSpec((T, GB * H, ncx), lambda i: (0, i, 0)),
                   pl.BlockSpec((None, wr, D), lambda i: (0, i, 0))],
        out_shape=[jax.ShapeDtypeStruct((DS, n), BF16), jax.ShapeDtypeStruct((T, DS, ncx), BF16),
                   jax.ShapeDtypeStruct((1, 2 * DS, D), BF16)],
        scratch_shapes=[state, state, state, state, pltpu.VMEM((GB, TH, NCP), F32)],
        compiler_params=_cparams(("parallel",)),
        name="s5",
    )(ut, utc, w1t, w2, acoef, dcol, w_out)


def _outproj_kernel(x_hbm, mod_ref, yt_ref, szt_ref, s_ref, sprev_ref, snext_ref, bg_ref, szc_ref,
                    wgt_ref, bglu_ref, cw_ref, cb_ref, wo5_ref, wocv_ref, lng_ref, lnb_ref,
                    o_hbm, xbuf, xsem, obuf, osem, *, cb, tb, nb, ntt, row_chunks, alpha):
    tt = pl.program_id(1)
    step = pl.program_id(0) * ntt + tt
    nsteps = nb * ntt
    slot = _fetch_planes(x_hbm, xbuf, xsem, step, nsteps, cb=cb, tb=tb, ntt=ntt)
    g = jax.nn.gelu(yt_ref[...].astype(F32))
    gate = jnp.dot(wgt_ref[...], g.astype(BF16), preferred_element_type=F32) + bglu_ref[...]
    o5t = (g * jax.nn.sigmoid(gate) * szt_ref[...].astype(F32)).astype(BF16)
    sub = lax.dot_general(o5t, wo5_ref[...], TN_DIMS, preferred_element_type=F32)
    crow = lax.broadcasted_iota(jnp.int32, (cb, 1), 0) % row_chunks
    sp = sprev_ref[...].astype(F32)
    sn = snext_ref[...].astype(F32)
    sp_wrap = jnp.where(crow != 0, pltpu.roll(sp, 1, 0), 0.0)
    sn_wrap = jnp.where(crow != row_chunks - 1, pltpu.roll(sn, cb - 1, 0), 0.0)
    sp = jnp.where(tt == 0, sp_wrap, sp)
    sn = jnp.where(tt == ntt - 1, sn_wrap, sn)
    s_all = jnp.concatenate([sp, s_ref[...].astype(F32), sn], axis=0)
    r = cb * tb
    conv = (cw_ref[0:1, :] * s_all[0:r] + cw_ref[1:2, :] * s_all[cb:cb + r]
            + cw_ref[2:3, :] * s_all[2 * cb:2 * cb + r] + cb_ref[...])
    ocv = (bg_ref[...].astype(F32) * conv * szc_ref[...].astype(F32)).astype(BF16)
    sub = sub + jnp.dot(ocv, wocv_ref[...], preferred_element_type=F32)
    gate_vec = mod_ref[2:3, :]
    for t in range(tb):
        v = alpha * xbuf[slot, t] + gate_vec * sub[t * cb:(t + 1) * cb]
        obuf[slot, t] = _layernorm_rows(v) * lng_ref[...] + lnb_ref[...]
    put = functools.partial(_plane_copies, o_hbm, obuf, osem, cb=cb, tb=tb, ntt=ntt, to_hbm=True)

    @pl.when(step > 0)
    def _():
        for cp in put(step - 1, 1 - slot):
            cp.wait()

    for cp in put(step, slot):
        cp.start()

    @pl.when(step == nsteps - 1)
    def _():
        for cp in put(step, slot):
            cp.wait()


def _outproj(x3, mod, yt, szt, s, bg, szc, wo, wts, layer, mod_idx, cb, tb, row_chunks, alpha):
    nc = x3.shape[0]
    nb, ntt = nc // cb, T // tb
    r = cb * tb
    col = lambda b, tt: (0, b * ntt + tt)
    row = lambda b, tt: (b * ntt + tt, 0)
    prev_plane = lambda b, tt: (b * T + (tt * tb + T - 1) % T, 0)
    next_plane = lambda b, tt: (b * T + (tt * tb + tb) % T, 0)
    wgt, bglu, cw, cbias, lng, lnb = wts
    planes = pltpu.VMEM((2, tb, cb, D), F32)
    return pl.pallas_call(
        functools.partial(_outproj_kernel, cb=cb, tb=tb, nb=nb, ntt=ntt, row_chunks=row_chunks, alpha=alpha),
        grid=(nb, ntt),
        in_specs=[pl.BlockSpec(memory_space=pl.ANY),
                  _layer_spec((3, D), mod_idx, 0, 0),
                  pl.BlockSpec((DS, r), col), pl.BlockSpec((DS, r), col),
                  pl.BlockSpec((r, DS), row),
                  pl.BlockSpec((cb, DS), prev_plane), pl.BlockSpec((cb, DS), next_plane),
                  pl.BlockSpec((r, DS), row), pl.BlockSpec((r, DS), row),
                  _layer_spec((DS, DS), layer, 0, 0), _layer_spec((DS, 1), layer, 0, 0),
                  _layer_spec((3, DS), layer, 0, 0), _layer_spec((1, DS), layer, 0, 0),
                  _layer_spec((DS, D), 0, 0, 0), _layer_spec((DS, D), 0, 1, 0),
                  _layer_spec((1, D), layer, 0, 0), _layer_spec((1, D), layer, 0, 0)],
        out_specs=pl.BlockSpec(memory_space=pl.ANY),
        out_shape=jax.ShapeDtypeStruct(x3.shape, F32),
        scratch_shapes=[planes, pltpu.SemaphoreType.DMA((2, tb)), planes, pltpu.SemaphoreType.DMA((2, tb))],
        compiler_params=_cparams(("arbitrary", "arbitrary")),
        name="outproj",
    )(x3, mod, yt, szt, s, s, s, bg, szc, wgt, bglu, cw, cbias, wo, wo, lng, lnb)


def kernel(x, c, ctx, c_ctx, w_ada, b_ada, w_in, s5_a_re, s5_a_im, s5_log_dt, s5_b_re, s5_b_im,
           s5_c_re, s5_c_im, s5_d, w_glu, b_glu, conv_w, conv_b, w_out, ln_g, ln_b):
    bsz, n, d = x.shape
    n_ctx = ctx.shape[1]
    depth = w_in.shape[0]
    assert bsz == 1 and d == D and n == NCL * T and n_ctx == NCC * T
    alpha = (2.0 * depth) ** 0.25

    condT = jnp.stack([c[0], c_ctx], axis=1)
    mod = _adaln(condT, w_ada, b_ada).reshape(depth * 2, 3, D)
    w1t, w2, acoef = _s5prep(s5_a_re, s5_a_im, s5_log_dt, s5_b_re, s5_b_im, s5_c_re, s5_c_im)
    dcol = jnp.tile(s5_d.reshape(depth, G, 1, H), (1, 1, T, 1)).reshape(depth, G, TH, 1)
    w_row, w_t = _cast_split(w_in, 2 * DS, 1)
    out_w = (_cast(w_glu, True, DS), b_glu.reshape(depth, DS, 1), conv_w, conv_b.reshape(depth, 1, DS),
             ln_g.reshape(depth, 1, D), ln_b.reshape(depth, 1, D))

    x3 = x.reshape(n // T, T, D)
    c3 = ctx.reshape(n_ctx // T, T, D)
    tb = 4
    for i in range(depth):
        last = i == depth - 1
        ut, szt, s, bg, szc, *w_next = _inproj(x3, mod, w_t, w_row, 2 * i, CBL, tb,
                                               w_next=None if last else w_in, next_layer=i + 1)
        utc2, *ctx_rest = _inproj(c3, mod, w_t, w_row, 2 * i + 1, NCC, T, u_only=last)
        utc = jnp.pad(utc2.reshape(DS, T, NCC).transpose(1, 0, 2),
                      ((0, 0), (0, 0), (0, NCP - NCT)))
        yt, ytc, wo = _s5(ut, utc, w1t, w2, acoef, dcol, w_out, i, ctx_out=not last)
        x3 = _outproj(x3, mod, yt, szt, s, bg, szc, wo, out_w, i, 2 * i, CBL, tb, GRID_W // T, alpha)
        if not last:
            ytc2 = ytc[:, :, 0:NCC].transpose(1, 0, 2).reshape(DS, n_ctx)
            c3 = _outproj(c3, mod, ytc2, *ctx_rest, wo, out_w, i, 2 * i + 1, NCC, T, NCC, alpha)
            w_row, w_t = w_next
    return x3.reshape(bsz, n, D)
```

```python
import functools

import jax
import jax.numpy as jnp
from jax import lax
from jax.experimental import pallas as pl
from jax.experimental.pallas import tpu as pltpu

D = 2048
DS = 1024
G = 64
H = 16
P = 64
T = 16
TH = T * H
GRID_W = 64
LN_EPS = 1e-6
LANES = 128
VMEM_LIMIT = 60 * 1024 * 1024

F32 = jnp.float32
BF16 = jnp.bfloat16
NT_DIMS = (((1,), (1,)), ((), ()))
TN_DIMS = (((0,), (0,)), ((), ()))


def _cparams(sem):
    return pltpu.CompilerParams(dimension_semantics=sem, vmem_limit_bytes=VMEM_LIMIT)


def _layer_spec(shape, *idx):
    return pl.BlockSpec((None,) + shape, lambda *_: idx, pipeline_mode=pl.Buffered(1))


def _silu(v):
    return v * jax.nn.sigmoid(v)


def _adaln_kernel(condT_ref, w_ref, b_ref, o_ref, s_scr):
    @pl.when(jnp.logical_and(pl.program_id(0) == 0, pl.program_id(1) == 0))
    def _():
        s_scr[...] = _silu(condT_ref[...])

    w = w_ref[0]
    for i in range(2):
        o_ref[0, i:i + 1, :] = jnp.sum(w * s_scr[:, i:i + 1], axis=0, keepdims=True) + b_ref[0]


def _adaln(condT, w_ada, b_ada):
    depth, _, n3 = w_ada.shape
    tn = 512
    return pl.pallas_call(
        _adaln_kernel,
        grid=(depth, n3 // tn),
        in_specs=[pl.BlockSpec((D, 2), lambda l, j: (0, 0)),
                  pl.BlockSpec((1, D, tn), lambda l, j: (l, 0, j)),
                  pl.BlockSpec((1, 1, tn), lambda l, j: (l, 0, j))],
        out_specs=pl.BlockSpec((1, 2, tn), lambda l, j: (l, 0, j)),
        out_shape=jax.ShapeDtypeStruct((depth, 2, n3), F32),
        scratch_shapes=[pltpu.VMEM((D, 2), F32)],
        compiler_params=_cparams(("arbitrary", "arbitrary")),
        name="adaln",
    )(condT, w_ada, b_ada.reshape(depth, 1, n3))


def _cast_split_kernel(w_ref, wrow_ref, wt_ref, *, nt):
    w = w_ref[0]
    wt_ref[0] = w[:, 0:nt].T.astype(BF16)
    wrow_ref[0] = w[:, nt:].astype(BF16)


def _cast_split(w, nt, n_layers, tr=256):
    _, rows, cols = w.shape
    depth = n_layers
    return pl.pallas_call(
        functools.partial(_cast_split_kernel, nt=nt),
        grid=(depth, rows // tr),
        in_specs=[pl.BlockSpec((1, tr, cols), lambda l, r: (l, r, 0))],
        out_specs=[pl.BlockSpec((1, tr, cols - nt), lambda l, r: (l, r, 0)),
                   pl.BlockSpec((1, nt, tr), lambda l, r: (l, 0, r))],
        out_shape=[jax.ShapeDtypeStruct((depth, rows, cols - nt), BF16),
                   jax.ShapeDtypeStruct((depth, nt, rows), BF16)],
        compiler_params=_cparams(("parallel", "parallel")),
        name="cast_split",
    )(w)


def _cast_kernel(w_ref, o_ref, *, transpose):
    w = w_ref[0]
    o_ref[0] = (w.T if transpose else w).astype(BF16)


def _cast(w, transpose, tr):
    depth, rows, cols = w.shape
    out_block = (1, cols, tr) if transpose else (1, tr, cols)
    out_map = (lambda l, r: (l, 0, r)) if transpose else (lambda l, r: (l, r, 0))
    return pl.pallas_call(
        functools.partial(_cast_kernel, transpose=transpose),
        grid=(depth, rows // tr),
        in_specs=[pl.BlockSpec((1, tr, cols), lambda l, r: (l, r, 0))],
        out_specs=pl.BlockSpec(out_block, out_map),
        out_shape=jax.ShapeDtypeStruct((depth, cols, rows) if transpose else (depth, rows, cols), BF16),
        compiler_params=_cparams(("parallel", "parallel")),
        name="cast_t" if transpose else "cast",
    )(w)


GP = 8
PRM_ROWS = 40


def _s5prep_kernel(prm_ref, bre_ref, bim_ref, d_ref, w1t_ref, w2_ref, acoef_ref, kt_scr):
    lane = lax.broadcasted_iota(jnp.int32, (1, LANES), 1)
    fwd_lane = lane < P
    krow = lax.broadcasted_iota(jnp.int32, (T, 1), 0)
    fwd_row = lax.broadcasted_iota(jnp.int32, (2 * P, 1), 0) < P
    col256 = lax.broadcasted_iota(jnp.int32, (T, TH), 1)
    r16 = lax.broadcasted_iota(jnp.int32, (T, TH), 0)
    diag_mask = (col256 % H == r16).astype(F32)
    e_rep = diag_mask.astype(BF16)
    e_blk = (col256 // H == r16).astype(BF16)

    def rows16(tab):
        return jnp.broadcast_to(tab[:, None, :], (T, H, tab.shape[1])).reshape(TH, tab.shape[1])

    def tile16(mat):
        return jnp.broadcast_to(mat[None], (T, H, mat.shape[1])).reshape(TH, mat.shape[1])

    def bdot(a, b):
        return jnp.dot(a.astype(BF16), b.astype(BF16), preferred_element_type=F32)

    def expand(tab, e):
        return lax.dot_general(tab.astype(BF16), e, TN_DIMS, preferred_element_type=F32)

    for gi in range(GP):
        ar, ai = prm_ref[0, gi, 0:1, :], prm_ref[0, gi, 1:2, :]
        dt = jnp.exp(prm_ref[0, gi, 2:3, :])
        la, th = ar * dt, ai * dt
        cr, ci = tile16(prm_ref[0, gi, 8:24, :]), tile16(prm_ref[0, gi, 24:40, :])
        b_x = [bdot(jnp.concatenate([b_ref[0, 0, gi], b_ref[0, 1, gi]], axis=0), e_rep)
               for b_ref in (bre_ref, bim_ref)]

        mag1 = jnp.exp(la)
        a1r, a1i = mag1 * jnp.cos(th), mag1 * jnp.sin(th)
        squares = [(a1r, a1i)]
        for _ in range(4):
            sr, si = squares[-1]
            squares.append((sr * sr - si * si, 2.0 * sr * si))

        zr, zi = a1r - 1.0, a1i
        den = ar * ar + ai * ai
        fr, fi = (zr * ar + zi * ai) / den, (zi * ar - zr * ai) / den

        def powrow(kf, kb, with_f):
            k = jnp.where(fwd_lane, kf, kb)
            if with_f:
                pr, pi = jnp.broadcast_to(fr, k.shape), jnp.broadcast_to(fi, k.shape)
            else:
                pr, pi = jnp.ones(k.shape, F32), jnp.zeros(k.shape, F32)
            for b, (sr, si) in enumerate(squares):
                bit = ((k >> b) & 1) == 1
                pr, pi = jnp.where(bit, pr * sr - pi * si, pr), jnp.where(bit, pr * si + pi * sr, pi)
            return pr, pi

        def ctab(kf, kb, with_f):
            pr, pi = powrow(kf, kb, with_f)
            pr, pi = rows16(pr), rows16(pi)
            return cr * pr - ci * pi, cr * pi + ci * pr

        bbr_x, bbi_x = b_x
        ckr, cki = ctab(krow, (T - 1) - krow, True)
        zero = jnp.zeros_like(bbr_x)
        kf = bdot(ckr, jnp.where(fwd_row, bbr_x, zero)) - bdot(cki, jnp.where(fwd_row, bbi_x, zero))
        kb = bdot(ckr, jnp.where(fwd_row, zero, bbr_x)) - bdot(cki, jnp.where(fwd_row, zero, bbi_x))
        kt_scr[0:TH - H, :] = kb[0:TH - H]
        kt_scr[TH - H:TH, :] = kb[TH - H:TH] + kf[0:H] + diag_mask * d_ref[0, gi:gi + 1, :]
        kt_scr[TH:2 * TH - H, :] = kf[H:TH]
        for r0 in range(0, TH, LANES):
            for c0 in range(0, TH, LANES):
                blk = (lax.broadcasted_iota(jnp.int32, (LANES, LANES), 1) + c0) // H
                m = jnp.zeros((LANES, LANES), F32)
                for s in range(c0 // H, (c0 + LANES) // H):
                    w0 = (T - 1 - s) * H + r0
                    m = jnp.where(blk == s, kt_scr[w0:w0 + LANES, c0:c0 + LANES], m)
                w1t_ref[0, gi, r0:r0 + LANES, c0:c0 + LANES] = m.astype(BF16)

        pir, pii = powrow((T - 1) - krow, krow, True)
        pxr, pxi = expand(pir, e_blk), expand(pii, e_blk)
        w1t_ref[0, gi, TH:TH + 2 * P, :] = (pxr * bbr_x - pxi * bbi_x).astype(BF16)
        w1t_ref[0, gi, TH + 2 * P:2 * TH, :] = (pxr * bbi_x + pxi * bbr_x).astype(BF16)

        cor, coi = ctab(krow + 1, T - krow, False)
        w2_ref[0, gi, :, 0:LANES] = cor.astype(BF16)
        w2_ref[0, gi, :, LANES:2 * LANES] = (-coi).astype(BF16)

        acoef_ref[0, 0, gi:gi + 1, :] = squares[4][0]
        acoef_ref[0, 1, gi:gi + 1, :] = squares[4][1]


def _s5prep(a_re, a_im, log_dt, b_re, b_im, c_re, c_im, d_skip):
    depth = a_re.shape[0]
    d_cols = jnp.tile(d_skip.reshape(depth, G, H), (1, 1, T))

    def dircat(v):
        return jnp.concatenate([v[:, 0], v[:, 1]], axis=-1)

    ld = jnp.broadcast_to(log_dt[..., None, None], log_dt.shape + (1, P))
    pad = jnp.zeros((depth, G, 5, 2 * P), F32)
    prm = jnp.concatenate([dircat(a_re[:, :, :, None, :]), dircat(a_im[:, :, :, None, :]), dircat(ld), pad,
                           dircat(c_re), dircat(c_im)], axis=2)
    b_spec = pl.BlockSpec((1, 2, GP, P, H), lambda l, g: (l, 0, g, 0, 0))
    return pl.pallas_call(
        _s5prep_kernel,
        grid=(depth, G // GP),
        in_specs=[pl.BlockSpec((1, GP, PRM_ROWS, 2 * P), lambda l, g: (l, g, 0, 0)), b_spec, b_spec,
                  pl.BlockSpec((1, GP, TH), lambda l, g: (l, g, 0))],
        out_specs=[pl.BlockSpec((1, GP, 2 * TH, TH), lambda l, g: (l, g, 0, 0)),
                   pl.BlockSpec((1, GP, TH, TH), lambda l, g: (l, g, 0, 0)),
                   pl.BlockSpec((1, 2, GP, 2 * P), lambda l, g: (l, 0, g, 0))],
        out_shape=[jax.ShapeDtypeStruct((depth, G, 2 * TH, TH), BF16),
                   jax.ShapeDtypeStruct((depth, G, TH, TH), BF16),
                   jax.ShapeDtypeStruct((depth, 2, G, 2 * P), F32)],
        scratch_shapes=[pltpu.VMEM((2 * TH, TH), F32)],
        compiler_params=_cparams(("parallel", "parallel")),
        name="s5prep",
    )(prm, b_re, b_im, d_cols)


def _plane_copies(hbm3, buf, sem, step, slot, *, cb, tb, ntt, to_hbm):
    b, tt = step // ntt, step % ntt
    copies = []
    for t in range(tb):
        hbm = hbm3.at[pl.ds(b * cb, cb), tt * tb + t, :]
        vm = buf.at[slot, t]
        copies.append(pltpu.make_async_copy(vm, hbm, sem.at[slot, t]) if to_hbm
                      else pltpu.make_async_copy(hbm, vm, sem.at[slot, t]))
    return copies


def _fetch_planes(x_hbm, xbuf, xsem, step, nsteps, **kw):
    slot = step % 2
    mk = functools.partial(_plane_copies, x_hbm, xbuf, xsem, to_hbm=False, **kw)

    @pl.when(step == 0)
    def _():
        for cp in mk(step, slot):
            cp.start()

    @pl.when(step + 1 < nsteps)
    def _():
        for cp in mk(step + 1, 1 - slot):
            cp.start()

    for cp in mk(step, slot):
        cp.wait()
    return slot


def _layernorm_rows(v):
    mu = jnp.mean(v, axis=-1, keepdims=True)
    vc = v - mu
    var = jnp.mean(vc * vc, axis=-1, keepdims=True)
    return vc * lax.rsqrt(var + LN_EPS)


def _inproj_kernel(*refs, cb, tb, nb, ntt, u_only, cast_next):
    if u_only:
        x_hbm, mod_ref, wut_ref, ut_ref, xbuf, xsem, h_scr = refs
    elif cast_next:
        (x_hbm, mod_ref, wut_ref, wzt_ref, wv_ref, wb_ref, wc_ref, wzc_ref, wnext_ref,
         ut_ref, szt_ref, s_ref, bg_ref, szc_ref, wrow_next_ref, wt_next_ref, xbuf, xsem, h_scr) = refs
        wn = wnext_ref[...]
        wt_next_ref[...] = wn[:, 0:2 * DS].T.astype(BF16)
        wrow_next_ref[...] = wn[:, 2 * DS:].astype(BF16)
    else:
        (x_hbm, mod_ref, wut_ref, wzt_ref, wv_ref, wb_ref, wc_ref, wzc_ref,
         ut_ref, szt_ref, s_ref, bg_ref, szc_ref, xbuf, xsem, h_scr) = refs
    step = pl.program_id(0) * ntt + pl.program_id(1)
    slot = _fetch_planes(x_hbm, xbuf, xsem, step, nb * ntt, cb=cb, tb=tb, ntt=ntt)
    shift, scale = mod_ref[0:1, :], mod_ref[1:2, :]
    for t in range(tb):
        hn = _layernorm_rows(xbuf[slot, t]) * (1.0 + scale) + shift
        h_scr[t * cb:(t + 1) * cb, :] = hn.astype(BF16)
    h = h_scr[...]
    ut_ref[...] = lax.dot_general(wut_ref[...], h, NT_DIMS, preferred_element_type=F32).astype(BF16)
    if u_only:
        return
    szt_ref[...] = _silu(lax.dot_general(wzt_ref[...], h, NT_DIMS, preferred_element_type=F32)).astype(BF16)
    v = jnp.dot(h, wv_ref[...], preferred_element_type=F32)
    cg = jnp.dot(h, wc_ref[...], preferred_element_type=F32)
    s_ref[...] = (cg * v).astype(BF16)
    bg_ref[...] = jnp.dot(h, wb_ref[...], preferred_element_type=F32).astype(BF16)
    szc_ref[...] = _silu(jnp.dot(h, wzc_ref[...], preferred_element_type=F32)).astype(BF16)


def _inproj(x3, mod, wt, wrow, mod_idx, cb, tb, u_only=False, w_next=None, next_layer=0):
    nc = x3.shape[0]
    n = nc * T
    nb, ntt = nc // cb, T // tb
    r = cb * tb
    col = lambda b, tt: (0, b * ntt + tt)
    row = lambda b, tt: (b * ntt + tt, 0)
    in_specs = [pl.BlockSpec(memory_space=pl.ANY), _layer_spec((3, D), mod_idx, 0, 0),
                _layer_spec((DS, D), 0, 0, 0)]
    out_specs = [pl.BlockSpec((DS, r), col)]
    out_shape = [jax.ShapeDtypeStruct((DS, n), BF16)]
    args = [x3, mod, wt]
    if not u_only:
        in_specs += [_layer_spec((DS, D), 0, 1, 0)] + [_layer_spec((D, DS), 0, 0, k) for k in range(4)]
        out_specs += [pl.BlockSpec((DS, r), col)] + [pl.BlockSpec((r, DS), row)] * 3
        out_shape += [jax.ShapeDtypeStruct((DS, n), BF16)] + [jax.ShapeDtypeStruct((n, DS), BF16)] * 3
        args += [wt, wrow, wrow, wrow, wrow]
    if w_next is not None:
        wr = D // (nb * ntt)
        in_specs += [pl.BlockSpec((None, wr, 6 * DS), lambda b, tt: (next_layer, b * ntt + tt, 0))]
        out_specs += [pl.BlockSpec((None, wr, 4 * DS), lambda b, tt: (0, b * ntt + tt, 0)),
                      pl.BlockSpec((None, 2 * DS, wr), lambda b, tt: (0, 0, b * ntt + tt))]
        out_shape += [jax.ShapeDtypeStruct((1, D, 4 * DS), BF16), jax.ShapeDtypeStruct((1, 2 * DS, D), BF16)]
        args += [w_next]
    return pl.pallas_call(
        functools.partial(_inproj_kernel, cb=cb, tb=tb, nb=nb, ntt=ntt, u_only=u_only,
                          cast_next=w_next is not None),
        grid=(nb, ntt),
        in_specs=in_specs,
        out_specs=out_specs,
        out_shape=out_shape,
        scratch_shapes=[pltpu.VMEM((2, tb, cb, D), F32), pltpu.SemaphoreType.DMA((2, tb)),
                        pltpu.VMEM((r, D), BF16)],
        compiler_params=_cparams(("arbitrary", "arbitrary")),
        name="inproj_u" if u_only else "inproj",
    )(*args)


GB = 8
NCL = 512
NCC = 16
NCT = NCL + NCC
NCP = 640
CBL = 128


def _s5_kernel(ut_ref, utc_ref, w1t_ref, w2_ref, acoef_ref, wo_ref, yt_ref, ytc_ref, wo_bf_ref,
               sre_scr, sim_scr, hre_scr, him_scr, yloc_scr, *, ctx_out):
    nbl = NCL // CBL
    wo_bf_ref[...] = wo_ref[...].astype(BF16)
    for gl in range(GB):
        rows = slice(gl * H, (gl + 1) * H)
        planes = []
        for t in range(T):
            lat = [ut_ref[rows, (b * T + t) * CBL:(b * T + t + 1) * CBL] for b in range(nbl)]
            planes.append(jnp.concatenate(lat + [utc_ref[t, rows, :]], axis=1))
        z = jnp.concatenate(planes, axis=0)
        r1 = jnp.dot(w1t_ref[gl], z, preferred_element_type=F32)
        yloc_scr[gl] = r1[0:TH]
        st = r1[TH:2 * TH].T
        sre_scr[pl.ds(gl, NCP, stride=GB), :] = st[:, 0:LANES]
        sim_scr[pl.ds(gl, NCP, stride=GB), :] = st[:, LANES:2 * LANES]

    ar, ai = acoef_ref[0], acoef_ref[1]
    fwd_lane = lax.broadcasted_iota(jnp.int32, (GB, LANES), 1) < P
    bwd_lane = jnp.logical_not(fwd_lane)
    hre_scr[NCT * GB:NCP * GB, :] = jnp.zeros(((NCP - NCT) * GB, LANES), F32)
    him_scr[NCT * GB:NCP * GB, :] = jnp.zeros(((NCP - NCT) * GB, LANES), F32)

    def step(k, carry, fwd_shift):
        hr, hi = carry
        cf = k + fwd_shift
        cb_ = NCT - 1 - k
        rf = pl.ds(pl.multiple_of(cf * GB, GB), GB)
        rb = pl.ds(pl.multiple_of(cb_ * GB, GB), GB)
        pltpu.store(hre_scr.at[rf, :], hr, mask=fwd_lane)
        pltpu.store(hre_scr.at[rb, :], hr, mask=bwd_lane)
        pltpu.store(him_scr.at[rf, :], hi, mask=fwd_lane)
        pltpu.store(him_scr.at[rb, :], hi, mask=bwd_lane)
        sr = jnp.where(fwd_lane, sre_scr[rf, :], sre_scr[rb, :])
        si = jnp.where(fwd_lane, sim_scr[rf, :], sim_scr[rb, :])
        return ar * hr - ai * hi + sr, ar * hi + ai * hr + si

    zero = jnp.zeros((GB, LANES), F32)
    carry = lax.fori_loop(0, NCC, functools.partial(step, fwd_shift=NCL), (zero, zero))
    lax.fori_loop(NCC, NCT, functools.partial(step, fwd_shift=-NCC), carry, unroll=4)

    for gl in range(GB):
        rows = slice(gl * H, (gl + 1) * H)
        sel = pl.ds(gl, NCP, stride=GB)
        hp = jnp.concatenate([hre_scr[sel, :], him_scr[sel, :]], axis=1)
        y = yloc_scr[gl] + lax.dot_general(w2_ref[gl], hp.astype(BF16), NT_DIMS, preferred_element_type=F32)
        yb = y.astype(BF16)
        for t in range(T):
            for b in range(nbl):
                yt_ref[rows, (b * T + t) * CBL:(b * T + t + 1) * CBL] = \
                    yb[t * H:(t + 1) * H, b * CBL:(b + 1) * CBL]
            if ctx_out:
                ytc_ref[t, rows, :] = yb[t * H:(t + 1) * H, NCL:NCP]
    if not ctx_out:
        ytc_ref[...] = jnp.zeros(ytc_ref.shape, BF16)


def _s5(ut, utc, w1t, w2, acoef, w_out, layer, ctx_out):
    n = ut.shape[1]
    ncx = NCP - NCL
    wr = 2 * DS // (G // GB)
    state = pltpu.VMEM((NCP * GB, LANES), F32)
    return pl.pallas_call(
        functools.partial(_s5_kernel, ctx_out=ctx_out),
        grid=(G // GB,),
        in_specs=[pl.BlockSpec((GB * H, n), lambda i: (i, 0)),
                  pl.BlockSpec((T, GB * H, ncx), lambda i: (0, i, 0)),
                  pl.BlockSpec((None, GB, 2 * TH, TH), lambda i: (layer, i, 0, 0)),
                  pl.BlockSpec((None, GB, TH, TH), lambda i: (layer, i, 0, 0)),
                  pl.BlockSpec((None, 2, GB, LANES), lambda i: (layer, 0, i, 0)),
                  pl.BlockSpec((None, wr, D), lambda i: (layer, i, 0))],
        out_specs=[pl.BlockSpec((GB * H, n), lambda i: (i, 0)),
                   pl.BlockSpec((T, GB * H, ncx), lambda i: (0, i, 0)),
                   pl.BlockSpec((None, wr, D), lambda i: (0, i, 0))],
        out_shape=[jax.ShapeDtypeStruct((DS, n), BF16), jax.ShapeDtypeStruct((T, DS, ncx), BF16),
                   jax.ShapeDtypeStruct((1, 2 * DS, D), BF16)],
        scratch_shapes=[state, state, state, state, pltpu.VMEM((GB, TH, NCP), F32)],
        compiler_params=_cparams(("parallel",)),
        name="s5",
    )(ut, utc, w1t, w2, acoef, w_out)


def _outproj_kernel(x_hbm, mod_ref, yt_ref, szt_ref, s_ref, sprev_ref, snext_ref, bg_ref, szc_ref,
                    wgt_ref, bglu_ref, cw_ref, cb_ref, wo5_ref, wocv_ref, lng_ref, lnb_ref,
                    o_hbm, xbuf, xsem, obuf, osem, *, cb, tb, nb, ntt, row_chunks, alpha):
    tt = pl.program_id(1)
    step = pl.program_id(0) * ntt + tt
    nsteps = nb * ntt
    slot = _fetch_planes(x_hbm, xbuf, xsem, step, nsteps, cb=cb, tb=tb, ntt=ntt)
    g = jax.nn.gelu(yt_ref[...].astype(F32))
    gate = jnp.dot(wgt_ref[...], g.astype(BF16), preferred_element_type=F32) + bglu_ref[...]
    o5t = (g * jax.nn.sigmoid(gate) * szt_ref[...].astype(F32)).astype(BF16)
    sub = lax.dot_general(o5t, wo5_ref[...], TN_DIMS, preferred_element_type=F32)
    crow = lax.broadcasted_iota(jnp.int32, (cb, 1), 0) % row_chunks
    sp = sprev_ref[...].astype(F32)
    sn = snext_ref[...].astype(F32)
    sp_wrap = jnp.where(crow != 0, pltpu.roll(sp, 1, 0), 0.0)
    sn_wrap = jnp.where(crow != row_chunks - 1, pltpu.roll(sn, cb - 1, 0), 0.0)
    sp = jnp.where(tt == 0, sp_wrap, sp)
    sn = jnp.where(tt == ntt - 1, sn_wrap, sn)
    s_all = jnp.concatenate([sp, s_ref[...].astype(F32), sn], axis=0)
    r = cb * tb
    conv = (cw_ref[0:1, :] * s_all[0:r] + cw_ref[1:2, :] * s_all[cb:cb + r]
            + cw_ref[2:3, :] * s_all[2 * cb:2 * cb + r] + cb_ref[...])
    ocv = (bg_ref[...].astype(F32) * conv * szc_ref[...].astype(F32)).astype(BF16)
    sub = sub + jnp.dot(ocv, wocv_ref[...], preferred_element_type=F32)
    gate_vec = mod_ref[2:3, :]
    for t in range(tb):
        v = alpha * xbuf[slot, t] + gate_vec * sub[t * cb:(t + 1) * cb]
        obuf[slot, t] = _layernorm_rows(v) * lng_ref[...] + lnb_ref[...]
    put = functools.partial(_plane_copies, o_hbm, obuf, osem, cb=cb, tb=tb, ntt=ntt, to_hbm=True)

    @pl.when(step > 0)
    def _():
        for cp in put(step - 1, 1 - slot):
            cp.wait()

    for cp in put(step, slot):
        cp.start()

    @pl.when(step == nsteps - 1)
    def _():
        for cp in put(step, slot):
            cp.wait()


def _outproj(x3, mod, yt, szt, s, bg, szc, wo, wts, layer, mod_idx, cb, tb, row_chunks, alpha):
    nc = x3.shape[0]
    nb, ntt = nc // cb, T // tb
    r = cb * tb
    col = lambda b, tt: (0, b * ntt + tt)
    row = lambda b, tt: (b * ntt + tt, 0)
    prev_plane = lambda b, tt: (b * T + (tt * tb + T - 1) % T, 0)
    next_plane = lambda b, tt: (b * T + (tt * tb + tb) % T, 0)
    wgt, bglu, cw, cbias, lng, lnb = wts
    planes = pltpu.VMEM((2, tb, cb, D), F32)
    return pl.pallas_call(
        functools.partial(_outproj_kernel, cb=cb, tb=tb, nb=nb, ntt=ntt, row_chunks=row_chunks, alpha=alpha),
        grid=(nb, ntt),
        in_specs=[pl.BlockSpec(memory_space=pl.ANY),
                  _layer_spec((3, D), mod_idx, 0, 0),
                  pl.BlockSpec((DS, r), col), pl.BlockSpec((DS, r), col),
                  pl.BlockSpec((r, DS), row),
                  pl.BlockSpec((cb, DS), prev_plane), pl.BlockSpec((cb, DS), next_plane),
                  pl.BlockSpec((r, DS), row), pl.BlockSpec((r, DS), row),
                  _layer_spec((DS, DS), layer, 0, 0), _layer_spec((DS, 1), layer, 0, 0),
                  _layer_spec((3, DS), layer, 0, 0), _layer_spec((1, DS), layer, 0, 0),
                  _layer_spec((DS, D), 0, 0, 0), _layer_spec((DS, D), 0, 1, 0),
                  _layer_spec((1, D), layer, 0, 0), _layer_spec((1, D), layer, 0, 0)],
        out_specs=pl.BlockSpec(memory_space=pl.ANY),
        out_shape=jax.ShapeDtypeStruct(x3.shape, F32),
        scratch_shapes=[planes, pltpu.SemaphoreType.DMA((2, tb)), planes, pltpu.SemaphoreType.DMA((2, tb))],
        compiler_params=_cparams(("arbitrary", "arbitrary")),
        name="outproj",
    )(x3, mod, yt, szt, s, s, s, bg, szc, wgt, bglu, cw, cbias, wo, wo, lng, lnb)


def kernel(x, c, ctx, c_ctx, w_ada, b_ada, w_in, s5_a_re, s5_a_im, s5_log_dt, s5_b_re, s5_b_im,
           s5_c_re, s5_c_im, s5_d, w_glu, b_glu, conv_w, conv_b, w_out, ln_g, ln_b):
    bsz, n, d = x.shape
    n_ctx = ctx.shape[1]
    depth = w_in.shape[0]
    assert bsz == 1 and d == D and n == NCL * T and n_ctx == NCC * T
    alpha = (2.0 * depth) ** 0.25

    condT = jnp.stack([c[0], c_ctx], axis=1)
    mod = _adaln(condT, w_ada, b_ada).reshape(depth * 2, 3, D)
    w1t, w2, acoef = _s5prep(s5_a_re, s5_a_im, s5_log_dt, s5_b_re, s5_b_im, s5_c_re, s5_c_im, s5_d)
    w_row, w_t = _cast_split(w_in, 2 * DS, 1)
    out_w = (_cast(w_glu, True, DS), b_glu.reshape(depth, DS, 1), conv_w, conv_b.reshape(depth, 1, DS),
             ln_g.reshape(depth, 1, D), ln_b.reshape(depth, 1, D))

    x3 = x.reshape(n // T, T, D)
    c3 = ctx.reshape(n_ctx // T, T, D)
    tb = 4
    for i in range(depth):
        last = i == depth - 1
        ut, szt, s, bg, szc, *w_next = _inproj(x3, mod, w_t, w_row, 2 * i, CBL, tb,
                                               w_next=None if last else w_in, next_layer=i + 1)
        utc2, *ctx_rest = _inproj(c3, mod, w_t, w_row, 2 * i + 1, NCC, T, u_only=last)
        utc = jnp.pad(utc2.reshape(DS, T, NCC).transpose(1, 0, 2),
                      ((0, 0), (0, 0), (0, NCP - NCT)))
        yt, ytc, wo = _s5(ut, utc, w1t, w2, acoef, w_out, i, ctx_out=not last)
        x3 = _outproj(x3, mod, yt, szt, s, bg, szc, wo, out_w, i, 2 * i, CBL, tb, GRID_W // T, alpha)
        if not last:
            ytc2 = ytc[:, :, 0:NCC].transpose(1, 0, 2).reshape(DS, n_ctx)
            c3 = _outproj(c3, mod, ytc2, *ctx_rest, wo, out_w, i, 2 * i + 1, NCC, T, NCC, alpha)
            w_row, w_t = w_next
    return x3.reshape(bsz, n, D)
```

```python
import functools

import jax
import jax.numpy as jnp
from jax import lax
from jax.experimental import pallas as pl
from jax.experimental.pallas import tpu as pltpu

D = 2048
DS = 1024
G = 64
H = 16
P = 64
T = 16
TH = T * H
GRID_W = 64
LN_EPS = 1e-6
LANES = 128
VMEM_LIMIT = 60 * 1024 * 1024

F32 = jnp.float32
BF16 = jnp.bfloat16
NT_DIMS = (((1,), (1,)), ((), ()))
TN_DIMS = (((0,), (0,)), ((), ()))


def _cparams(sem):
    return pltpu.CompilerParams(dimension_semantics=sem, vmem_limit_bytes=VMEM_LIMIT)


def _layer_spec(shape, *idx):
    return pl.BlockSpec((None,) + shape, lambda *_: idx, pipeline_mode=pl.Buffered(1))


def _silu(v):
    return v * jax.nn.sigmoid(v)


def _adaln_kernel(condT_ref, w_ref, b_ref, o_ref, s_scr):
    @pl.when(jnp.logical_and(pl.program_id(0) == 0, pl.program_id(1) == 0))
    def _():
        s_scr[...] = _silu(condT_ref[...])

    w = w_ref[0]
    for i in range(2):
        o_ref[0, i:i + 1, :] = jnp.sum(w * s_scr[:, i:i + 1], axis=0, keepdims=True) + b_ref[0]


def _adaln(condT, w_ada, b_ada):
    depth, _, n3 = w_ada.shape
    tn = 2048
    return pl.pallas_call(
        _adaln_kernel,
        grid=(depth, n3 // tn),
        in_specs=[pl.BlockSpec((D, 2), lambda l, j: (0, 0)),
                  pl.BlockSpec((1, D, tn), lambda l, j: (l, 0, j)),
                  pl.BlockSpec((1, 1, tn), lambda l, j: (l, 0, j))],
        out_specs=pl.BlockSpec((1, 2, tn), lambda l, j: (l, 0, j)),
        out_shape=jax.ShapeDtypeStruct((depth, 2, n3), F32),
        scratch_shapes=[pltpu.VMEM((D, 2), F32)],
        compiler_params=_cparams(("arbitrary", "arbitrary")),
        name="adaln",
    )(condT, w_ada, b_ada.reshape(depth, 1, n3))


GP = 8
PRM_ROWS = 40


def _s5prep_kernel(prm_ref, bre_ref, bim_ref, d_ref, win_ref, wglu_ref,
                   w1t_ref, w2_ref, acoef_ref, wrow_ref, wt_ref, wgt_ref, kt_scr):
    wn = win_ref[...]
    wt_ref[...] = wn[:, 0:2 * DS].T.astype(BF16)
    wrow_ref[...] = wn[:, 2 * DS:].astype(BF16)
    wgt_ref[...] = wglu_ref[...].T.astype(BF16)
    lane = lax.broadcasted_iota(jnp.int32, (1, LANES), 1)
    fwd_lane = lane < P
    krow = lax.broadcasted_iota(jnp.int32, (T, 1), 0)
    fwd_row = lax.broadcasted_iota(jnp.int32, (2 * P, 1), 0) < P
    col256 = lax.broadcasted_iota(jnp.int32, (T, TH), 1)
    r16 = lax.broadcasted_iota(jnp.int32, (T, TH), 0)
    diag_mask = (col256 % H == r16).astype(F32)
    e_rep = diag_mask.astype(BF16)
    e_blk = (col256 // H == r16).astype(BF16)

    def rows16(tab):
        return jnp.broadcast_to(tab[:, None, :], (T, H, tab.shape[1])).reshape(TH, tab.shape[1])

    def tile16(mat):
        return jnp.broadcast_to(mat[None], (T, H, mat.shape[1])).reshape(TH, mat.shape[1])

    def bdot(a, b):
        return jnp.dot(a.astype(BF16), b.astype(BF16), preferred_element_type=F32)

    def expand(tab, e):
        return lax.dot_general(tab.astype(BF16), e, TN_DIMS, preferred_element_type=F32)

    for gi in range(GP):
        ar, ai = prm_ref[0, gi, 0:1, :], prm_ref[0, gi, 1:2, :]
        dt = jnp.exp(prm_ref[0, gi, 2:3, :])
        la, th = ar * dt, ai * dt
        cr, ci = tile16(prm_ref[0, gi, 8:24, :]), tile16(prm_ref[0, gi, 24:40, :])
        b_x = [bdot(jnp.concatenate([b_ref[0, 0, gi], b_ref[0, 1, gi]], axis=0), e_rep)
               for b_ref in (bre_ref, bim_ref)]

        mag1 = jnp.exp(la)
        a1r, a1i = mag1 * jnp.cos(th), mag1 * jnp.sin(th)
        squares = [(a1r, a1i)]
        for _ in range(4):
            sr, si = squares[-1]
            squares.append((sr * sr - si * si, 2.0 * sr * si))

        zr, zi = a1r - 1.0, a1i
        den = ar * ar + ai * ai
        fr, fi = (zr * ar + zi * ai) / den, (zi * ar - zr * ai) / den

        def powrow(kf, kb, with_f):
            k = jnp.where(fwd_lane, kf, kb)
            if with_f:
                pr, pi = jnp.broadcast_to(fr, k.shape), jnp.broadcast_to(fi, k.shape)
            else:
                pr, pi = jnp.ones(k.shape, F32), jnp.zeros(k.shape, F32)
            for b, (sr, si) in enumerate(squares):
                bit = ((k >> b) & 1) == 1
                pr, pi = jnp.where(bit, pr * sr - pi * si, pr), jnp.where(bit, pr * si + pi * sr, pi)
            return pr, pi

        def ctab(kf, kb, with_f):
            pr, pi = powrow(kf, kb, with_f)
            pr, pi = rows16(pr), rows16(pi)
            return cr * pr - ci * pi, cr * pi + ci * pr

        bbr_x, bbi_x = b_x
        ckr, cki = ctab(krow, (T - 1) - krow, True)
        zero = jnp.zeros_like(bbr_x)
        kf = bdot(ckr, jnp.where(fwd_row, bbr_x, zero)) - bdot(cki, jnp.where(fwd_row, bbi_x, zero))
        kb = bdot(ckr, jnp.where(fwd_row, zero, bbr_x)) - bdot(cki, jnp.where(fwd_row, zero, bbi_x))
        kt_scr[0:TH - H, :] = kb[0:TH - H]
        kt_scr[TH - H:TH, :] = kb[TH - H:TH] + kf[0:H] + diag_mask * d_ref[0, gi:gi + 1, :]
        kt_scr[TH:2 * TH - H, :] = kf[H:TH]
        for r0 in range(0, TH, LANES):
            for c0 in range(0, TH, LANES):
                blk = (lax.broadcasted_iota(jnp.int32, (LANES, LANES), 1) + c0) // H
                m = jnp.zeros((LANES, LANES), F32)
                for s in range(c0 // H, (c0 + LANES) // H):
                    w0 = (T - 1 - s) * H + r0
                    m = jnp.where(blk == s, kt_scr[w0:w0 + LANES, c0:c0 + LANES], m)
                w1t_ref[0, gi, r0:r0 + LANES, c0:c0 + LANES] = m.astype(BF16)

        pir, pii = powrow((T - 1) - krow, krow, True)
        pxr, pxi = expand(pir, e_blk), expand(pii, e_blk)
        w1t_ref[0, gi, TH:TH + 2 * P, :] = (pxr * bbr_x - pxi * bbi_x).astype(BF16)
        w1t_ref[0, gi, TH + 2 * P:2 * TH, :] = (pxr * bbi_x + pxi * bbr_x).astype(BF16)

        cor, coi = ctab(krow + 1, T - krow, False)
        w2_ref[0, gi, :, 0:LANES] = cor.astype(BF16)
        w2_ref[0, gi, :, LANES:2 * LANES] = (-coi).astype(BF16)

        acoef_ref[0, 0, gi:gi + 1, :] = squares[4][0]
        acoef_ref[0, 1, gi:gi + 1, :] = squares[4][1]


def _s5prep(a_re, a_im, log_dt, b_re, b_im, c_re, c_im, d_skip, w_in, w_glu):
    depth = a_re.shape[0]
    nsteps = depth * (G // GP)
    wr = D // nsteps
    gr = DS // (G // GP)
    step = lambda l, g: l * (G // GP) + g
    d_cols = jnp.tile(d_skip.reshape(depth, G, H), (1, 1, T))

    def dircat(v):
        return jnp.concatenate([v[:, 0], v[:, 1]], axis=-1)

    ld = jnp.broadcast_to(log_dt[..., None, None], log_dt.shape + (1, P))
    pad = jnp.zeros((depth, G, 5, 2 * P), F32)
    prm = jnp.concatenate([dircat(a_re[:, :, :, None, :]), dircat(a_im[:, :, :, None, :]), dircat(ld), pad,
                           dircat(c_re), dircat(c_im)], axis=2)
    b_spec = pl.BlockSpec((1, 2, GP, P, H), lambda l, g: (l, 0, g, 0, 0))
    return pl.pallas_call(
        _s5prep_kernel,
        grid=(depth, G // GP),
        in_specs=[pl.BlockSpec((1, GP, PRM_ROWS, 2 * P), lambda l, g: (l, g, 0, 0)), b_spec, b_spec,
                  pl.BlockSpec((1, GP, TH), lambda l, g: (l, g, 0)),
                  pl.BlockSpec((None, wr, 6 * DS), lambda l, g: (0, step(l, g), 0)),
                  pl.BlockSpec((None, gr, DS), lambda l, g: (l, g, 0))],
        out_specs=[pl.BlockSpec((1, GP, 2 * TH, TH), lambda l, g: (l, g, 0, 0)),
                   pl.BlockSpec((1, GP, TH, TH), lambda l, g: (l, g, 0, 0)),
                   pl.BlockSpec((1, 2, GP, 2 * P), lambda l, g: (l, 0, g, 0)),
                   pl.BlockSpec((None, wr, 4 * DS), lambda l, g: (0, step(l, g), 0)),
                   pl.BlockSpec((None, 2 * DS, wr), lambda l, g: (0, 0, step(l, g))),
                   pl.BlockSpec((None, DS, gr), lambda l, g: (l, 0, g))],
        out_shape=[jax.ShapeDtypeStruct((depth, G, 2 * TH, TH), BF16),
                   jax.ShapeDtypeStruct((depth, G, TH, TH), BF16),
                   jax.ShapeDtypeStruct((depth, 2, G, 2 * P), F32),
                   jax.ShapeDtypeStruct((1, D, 4 * DS), BF16),
                   jax.ShapeDtypeStruct((1, 2 * DS, D), BF16),
                   jax.ShapeDtypeStruct((depth, DS, DS), BF16)],
        scratch_shapes=[pltpu.VMEM((2 * TH, TH), F32)],
        compiler_params=_cparams(("parallel", "parallel")),
        name="s5prep",
    )(prm, b_re, b_im, d_cols, w_in, w_glu)


def _plane_copies(hbm3, buf, sem, step, slot, *, cb, tb, ntt, to_hbm):
    b, tt = step // ntt, step % ntt
    copies = []
    for t in range(tb):
        hbm = hbm3.at[pl.ds(b * cb, cb), tt * tb + t, :]
        vm = buf.at[slot, t]
        copies.append(pltpu.make_async_copy(vm, hbm, sem.at[slot, t]) if to_hbm
                      else pltpu.make_async_copy(hbm, vm, sem.at[slot, t]))
    return copies


def _fetch_planes(x_hbm, xbuf, xsem, step, nsteps, **kw):
    slot = step % 2
    mk = functools.partial(_plane_copies, x_hbm, xbuf, xsem, to_hbm=False, **kw)

    @pl.when(step == 0)
    def _():
        for cp in mk(step, slot):
            cp.start()

    @pl.when(step + 1 < nsteps)
    def _():
        for cp in mk(step + 1, 1 - slot):
            cp.start()

    for cp in mk(step, slot):
        cp.wait()
    return slot


def _layernorm_rows(v):
    mu = jnp.mean(v, axis=-1, keepdims=True)
    vc = v - mu
    var = jnp.mean(vc * vc, axis=-1, keepdims=True)
    return vc * lax.rsqrt(var + LN_EPS)


def _inproj_kernel(*refs, cb, tb, nb, ntt, u_only, cast_next):
    if u_only:
        x_hbm, mod_ref, wut_ref, ut_ref, xbuf, xsem, h_scr = refs
    elif cast_next:
        (x_hbm, mod_ref, wut_ref, wzt_ref, wv_ref, wb_ref, wc_ref, wzc_ref, wnext_ref,
         ut_ref, szt_ref, s_ref, bg_ref, szc_ref, wrow_next_ref, wt_next_ref, xbuf, xsem, h_scr) = refs
        wn = wnext_ref[...]
        wt_next_ref[...] = wn[:, 0:2 * DS].T.astype(BF16)
        wrow_next_ref[...] = wn[:, 2 * DS:].astype(BF16)
    else:
        (x_hbm, mod_ref, wut_ref, wzt_ref, wv_ref, wb_ref, wc_ref, wzc_ref,
         ut_ref, szt_ref, s_ref, bg_ref, szc_ref, xbuf, xsem, h_scr) = refs
    step = pl.program_id(0) * ntt + pl.program_id(1)
    slot = _fetch_planes(x_hbm, xbuf, xsem, step, nb * ntt, cb=cb, tb=tb, ntt=ntt)
    shift, scale = mod_ref[0:1, :], mod_ref[1:2, :]
    for t in range(tb):
        hn = _layernorm_rows(xbuf[slot, t]) * (1.0 + scale) + shift
        h_scr[t * cb:(t + 1) * cb, :] = hn.astype(BF16)
    h = h_scr[...]
    ut_ref[...] = lax.dot_general(wut_ref[...], h, NT_DIMS, preferred_element_type=F32).astype(BF16)
    if u_only:
        return
    szt_ref[...] = _silu(lax.dot_general(wzt_ref[...], h, NT_DIMS, preferred_element_type=F32)).astype(BF16)
    v = jnp.dot(h, wv_ref[...], preferred_element_type=F32)
    cg = jnp.dot(h, wc_ref[...], preferred_element_type=F32)
    s_ref[...] = (cg * v).astype(BF16)
    bg_ref[...] = jnp.dot(h, wb_ref[...], preferred_element_type=F32).astype(BF16)
    szc_ref[...] = _silu(jnp.dot(h, wzc_ref[...], preferred_element_type=F32)).astype(BF16)


def _inproj(x3, mod, wt, wrow, mod_idx, cb, tb, u_only=False, w_next=None, next_layer=0):
    nc = x3.shape[0]
    n = nc * T
    nb, ntt = nc // cb, T // tb
    r = cb * tb
    col = lambda b, tt: (0, b * ntt + tt)
    row = lambda b, tt: (b * ntt + tt, 0)
    in_specs = [pl.BlockSpec(memory_space=pl.ANY), _layer_spec((3, D), mod_idx, 0, 0),
                _layer_spec((DS, D), 0, 0, 0)]
    out_specs = [pl.BlockSpec((DS, r), col)]
    out_shape = [jax.ShapeDtypeStruct((DS, n), BF16)]
    args = [x3, mod, wt]
    if not u_only:
        in_specs += [_layer_spec((DS, D), 0, 1, 0)] + [_layer_spec((D, DS), 0, 0, k) for k in range(4)]
        out_specs += [pl.BlockSpec((DS, r), col)] + [pl.BlockSpec((r, DS), row)] * 3
        out_shape += [jax.ShapeDtypeStruct((DS, n), BF16)] + [jax.ShapeDtypeStruct((n, DS), BF16)] * 3
        args += [wt, wrow, wrow, wrow, wrow]
    if w_next is not None:
        wr = D // (nb * ntt)
        in_specs += [pl.BlockSpec((None, wr, 6 * DS), lambda b, tt: (next_layer, b * ntt + tt, 0))]
        out_specs += [pl.BlockSpec((None, wr, 4 * DS), lambda b, tt: (0, b * ntt + tt, 0)),
                      pl.BlockSpec((None, 2 * DS, wr), lambda b, tt: (0, 0, b * ntt + tt))]
        out_shape += [jax.ShapeDtypeStruct((1, D, 4 * DS), BF16), jax.ShapeDtypeStruct((1, 2 * DS, D), BF16)]
        args += [w_next]
    return pl.pallas_call(
        functools.partial(_inproj_kernel, cb=cb, tb=tb, nb=nb, ntt=ntt, u_only=u_only,
                          cast_next=w_next is not None),
        grid=(nb, ntt),
        in_specs=in_specs,
        out_specs=out_specs,
        out_shape=out_shape,
        scratch_shapes=[pltpu.VMEM((2, tb, cb, D), F32), pltpu.SemaphoreType.DMA((2, tb)),
                        pltpu.VMEM((r, D), BF16)],
        compiler_params=_cparams(("arbitrary", "arbitrary")),
        name="inproj_u" if u_only else "inproj",
    )(*args)


GB = 8
NCL = 512
NCC = 16
NCT = NCL + NCC
NCP = 640
CBL = 128


def _s5_kernel(ut_ref, utc_ref, w1t_ref, w2_ref, acoef_ref, wo_ref, yt_ref, ytc_ref, wo_bf_ref,
               sre_scr, sim_scr, hre_scr, him_scr, yloc_scr, *, ctx_out):
    nbl = NCL // CBL
    wo_bf_ref[...] = wo_ref[...].astype(BF16)
    for gl in range(GB):
        rows = slice(gl * H, (gl + 1) * H)
        planes = []
        for t in range(T):
            lat = [ut_ref[rows, (b * T + t) * CBL:(b * T + t + 1) * CBL] for b in range(nbl)]
            planes.append(jnp.concatenate(lat + [utc_ref[t, rows, :]], axis=1))
        z = jnp.concatenate(planes, axis=0)
        r1 = jnp.dot(w1t_ref[gl], z, preferred_element_type=F32)
        yloc_scr[gl] = r1[0:TH]
        st = r1[TH:2 * TH].T
        sre_scr[pl.ds(gl, NCP, stride=GB), :] = st[:, 0:LANES]
        sim_scr[pl.ds(gl, NCP, stride=GB), :] = st[:, LANES:2 * LANES]

    ar, ai = acoef_ref[0], acoef_ref[1]
    fwd_lane = lax.broadcasted_iota(jnp.int32, (GB, LANES), 1) < P
    bwd_lane = jnp.logical_not(fwd_lane)
    hre_scr[NCT * GB:NCP * GB, :] = jnp.zeros(((NCP - NCT) * GB, LANES), F32)
    him_scr[NCT * GB:NCP * GB, :] = jnp.zeros(((NCP - NCT) * GB, LANES), F32)

    def step(k, carry, fwd_shift):
        hr, hi = carry
        cf = k + fwd_shift
        cb_ = NCT - 1 - k
        rf = pl.ds(pl.multiple_of(cf * GB, GB), GB)
        rb = pl.ds(pl.multiple_of(cb_ * GB, GB), GB)
        pltpu.store(hre_scr.at[rf, :], hr, mask=fwd_lane)
        pltpu.store(hre_scr.at[rb, :], hr, mask=bwd_lane)
        pltpu.store(him_scr.at[rf, :], hi, mask=fwd_lane)
        pltpu.store(him_scr.at[rb, :], hi, mask=bwd_lane)
        sr = jnp.where(fwd_lane, sre_scr[rf, :], sre_scr[rb, :])
        si = jnp.where(fwd_lane, sim_scr[rf, :], sim_scr[rb, :])
        return ar * hr - ai * hi + sr, ar * hi + ai * hr + si

    zero = jnp.zeros((GB, LANES), F32)
    carry = lax.fori_loop(0, NCC, functools.partial(step, fwd_shift=NCL), (zero, zero))
    lax.fori_loop(NCC, NCT, functools.partial(step, fwd_shift=-NCC), carry, unroll=4)

    for gl in range(GB):
        rows = slice(gl * H, (gl + 1) * H)
        sel = pl.ds(gl, NCP, stride=GB)
        hp = jnp.concatenate([hre_scr[sel, :], him_scr[sel, :]], axis=1)
        y = yloc_scr[gl] + lax.dot_general(w2_ref[gl], hp.astype(BF16), NT_DIMS, preferred_element_type=F32)
        yb = y.astype(BF16)
        for t in range(T):
            for b in range(nbl):
                yt_ref[rows, (b * T + t) * CBL:(b * T + t + 1) * CBL] = \
                    yb[t * H:(t + 1) * H, b * CBL:(b + 1) * CBL]
            if ctx_out:
                ytc_ref[t, rows, :] = yb[t * H:(t + 1) * H, NCL:NCP]
    if not ctx_out:
        ytc_ref[...] = jnp.zeros(ytc_ref.shape, BF16)


def _s5(ut, utc, w1t, w2, acoef, w_out, layer, ctx_out):
    n = ut.shape[1]
    ncx = NCP - NCL
    wr = 2 * DS // (G // GB)
    state = pltpu.VMEM((NCP * GB, LANES), F32)
    return pl.pallas_call(
        functools.partial(_s5_kernel, ctx_out=ctx_out),
        grid=(G // GB,),
        in_specs=[pl.BlockSpec((GB * H, n), lambda i: (i, 0)),
                  pl.BlockSpec((T, GB * H, ncx), lambda i: (0, i, 0)),
                  pl.BlockSpec((None, GB, 2 * TH, TH), lambda i: (layer, i, 0, 0)),
                  pl.BlockSpec((None, GB, TH, TH), lambda i: (layer, i, 0, 0)),
                  pl.BlockSpec((None, 2, GB, LANES), lambda i: (layer, 0, i, 0)),
                  pl.BlockSpec((None, wr, D), lambda i: (layer, i, 0))],
        out_specs=[pl.BlockSpec((GB * H, n), lambda i: (i, 0)),
                   pl.BlockSpec((T, GB * H, ncx), lambda i: (0, i, 0)),
                   pl.BlockSpec((None, wr, D), lambda i: (0, i, 0))],
        out_shape=[jax.ShapeDtypeStruct((DS, n), BF16), jax.ShapeDtypeStruct((T, DS, ncx), BF16),
                   jax.ShapeDtypeStruct((1, 2 * DS, D), BF16)],
        scratch_shapes=[state, state, state, state, pltpu.VMEM((GB, TH, NCP), F32)],
        compiler_params=_cparams(("parallel",)),
        name="s5",
    )(ut, utc, w1t, w2, acoef, w_out)


def _outproj_kernel(x_hbm, mod_ref, yt_ref, szt_ref, s_ref, sprev_ref, snext_ref, bg_ref, szc_ref,
                    wgt_ref, bglu_ref, cw_ref, cb_ref, wo5_ref, wocv_ref, lng_ref, lnb_ref,
                    o_hbm, xbuf, xsem, obuf, osem, *, cb, tb, nb, ntt, row_chunks, alpha):
    tt = pl.program_id(1)
    step = pl.program_id(0) * ntt + tt
    nsteps = nb * ntt
    slot = _fetch_planes(x_hbm, xbuf, xsem, step, nsteps, cb=cb, tb=tb, ntt=ntt)
    g = jax.nn.gelu(yt_ref[...].astype(F32))
    gate = jnp.dot(wgt_ref[...], g.astype(BF16), preferred_element_type=F32) + bglu_ref[...]
    o5t = (g * jax.nn.sigmoid(gate) * szt_ref[...].astype(F32)).astype(BF16)
    sub = lax.dot_general(o5t, wo5_ref[...], TN_DIMS, preferred_element_type=F32)
    crow = lax.broadcasted_iota(jnp.int32, (cb, 1), 0) % row_chunks
    sp = sprev_ref[...].astype(F32)
    sn = snext_ref[...].astype(F32)
    sp_wrap = jnp.where(crow != 0, pltpu.roll(sp, 1, 0), 0.0)
    sn_wrap = jnp.where(crow != row_chunks - 1, pltpu.roll(sn, cb - 1, 0), 0.0)
    sp = jnp.where(tt == 0, sp_wrap, sp)
    sn = jnp.where(tt == ntt - 1, sn_wrap, sn)
    s_all = jnp.concatenate([sp, s_ref[...].astype(F32), sn], axis=0)
    r = cb * tb
    conv = (cw_ref[0:1, :] * s_all[0:r] + cw_ref[1:2, :] * s_all[cb:cb + r]
            + cw_ref[2:3, :] * s_all[2 * cb:2 * cb + r] + cb_ref[...])
    ocv = (bg_ref[...].astype(F32) * conv * szc_ref[...].astype(F32)).astype(BF16)
    sub = sub + jnp.dot(ocv, wocv_ref[...], preferred_element_type=F32)
    gate_vec = mod_ref[2:3, :]
    for t in range(tb):
        v = alpha * xbuf[slot, t] + gate_vec * sub[t * cb:(t + 1) * cb]
        obuf[slot, t] = _layernorm_rows(v) * lng_ref[...] + lnb_ref[...]
    put = functools.partial(_plane_copies, o_hbm, obuf, osem, cb=cb, tb=tb, ntt=ntt, to_hbm=True)

    @pl.when(step > 0)
    def _():
        for cp in put(step - 1, 1 - slot):
            cp.wait()

    for cp in put(step, slot):
        cp.start()

    @pl.when(step == nsteps - 1)
    def _():
        for cp in put(step, slot):
            cp.wait()


def _outproj(x3, mod, yt, szt, s, bg, szc, wo, wts, layer, mod_idx, cb, tb, row_chunks, alpha):
    nc = x3.shape[0]
    nb, ntt = nc // cb, T // tb
    r = cb * tb
    col = lambda b, tt: (0, b * ntt + tt)
    row = lambda b, tt: (b * ntt + tt, 0)
    prev_plane = lambda b, tt: (b * T + (tt * tb + T - 1) % T, 0)
    next_plane = lambda b, tt: (b * T + (tt * tb + tb) % T, 0)
    wgt, bglu, cw, cbias, lng, lnb = wts
    planes = pltpu.VMEM((2, tb, cb, D), F32)
    return pl.pallas_call(
        functools.partial(_outproj_kernel, cb=cb, tb=tb, nb=nb, ntt=ntt, row_chunks=row_chunks, alpha=alpha),
        grid=(nb, ntt),
        in_specs=[pl.BlockSpec(memory_space=pl.ANY),
                  _layer_spec((3, D), mod_idx, 0, 0),
                  pl.BlockSpec((DS, r), col), pl.BlockSpec((DS, r), col),
                  pl.BlockSpec((r, DS), row),
                  pl.BlockSpec((cb, DS), prev_plane), pl.BlockSpec((cb, DS), next_plane),
                  pl.BlockSpec((r, DS), row), pl.BlockSpec((r, DS), row),
                  _layer_spec((DS, DS), layer, 0, 0), _layer_spec((DS, 1), layer, 0, 0),
                  _layer_spec((3, DS), layer, 0, 0), _layer_spec((1, DS), layer, 0, 0),
                  _layer_spec((DS, D), 0, 0, 0), _layer_spec((DS, D), 0, 1, 0),
                  _layer_spec((1, D), layer, 0, 0), _layer_spec((1, D), layer, 0, 0)],
        out_specs=pl.BlockSpec(memory_space=pl.ANY),
        out_shape=jax.ShapeDtypeStruct(x3.shape, F32),
        scratch_shapes=[planes, pltpu.SemaphoreType.DMA((2, tb)), planes, pltpu.SemaphoreType.DMA((2, tb))],
        compiler_params=_cparams(("arbitrary", "arbitrary")),
        name="outproj",
    )(x3, mod, yt, szt, s, s, s, bg, szc, wgt, bglu, cw, cbias, wo, wo, lng, lnb)


def kernel(x, c, ctx, c_ctx, w_ada, b_ada, w_in, s5_a_re, s5_a_im, s5_log_dt, s5_b_re, s5_b_im,
           s5_c_re, s5_c_im, s5_d, w_glu, b_glu, conv_w, conv_b, w_out, ln_g, ln_b):
    bsz, n, d = x.shape
    n_ctx = ctx.shape[1]
    depth = w_in.shape[0]
    assert bsz == 1 and d == D and n == NCL * T and n_ctx == NCC * T
    alpha = (2.0 * depth) ** 0.25

    condT = jnp.stack([c[0], c_ctx], axis=1)
    mod = _adaln(condT, w_ada, b_ada).reshape(depth * 2, 3, D)
    w1t, w2, acoef, w_row, w_t, w_glu_t = _s5prep(s5_a_re, s5_a_im, s5_log_dt, s5_b_re, s5_b_im,
                                                  s5_c_re, s5_c_im, s5_d, w_in, w_glu)
    out_w = (w_glu_t, b_glu.reshape(depth, DS, 1), conv_w, conv_b.reshape(depth, 1, DS),
             ln_g.reshape(depth, 1, D), ln_b.reshape(depth, 1, D))

    x3 = x.reshape(n // T, T, D)
    c3 = ctx.reshape(n_ctx // T, T, D)
    tb = 4
    for i in range(depth):
        last = i == depth - 1
        ut, szt, s, bg, szc, *w_next = _inproj(x3, mod, w_t, w_row, 2 * i, CBL, tb,
                                               w_next=None if last else w_in, next_layer=i + 1)
        utc2, *ctx_rest = _inproj(c3, mod, w_t, w_row, 2 * i + 1, NCC, T, u_only=last)
        utc = jnp.pad(utc2.reshape(DS, T, NCC).transpose(1, 0, 2),
                      ((0, 0), (0, 0), (0, NCP - NCT)))
        yt, ytc, wo = _s5(ut, utc, w1t, w2, acoef, w_out, i, ctx_out=not last)
        x3 = _outproj(x3, mod, yt, szt, s, bg, szc, wo, out_w, i, 2 * i, CBL, tb, GRID_W // T, alpha)
        if not last:
            ytc2 = ytc[:, :, 0:NCC].transpose(1, 0, 2).reshape(DS, n_ctx)
            c3 = _outproj(c3, mod, ytc2, *ctx_rest, wo, out_w, i, 2 * i + 1, NCC, T, NCC, alpha)
            w_row, w_t = w_next
    return x3.reshape(bsz, n, D)
```

```python
import functools

import jax
import jax.numpy as jnp
from jax import lax
from jax.experimental import pallas as pl
from jax.experimental.pallas import tpu as pltpu

D = 2048
DS = 1024
G = 64
H = 16
P = 64
T = 16
TH = T * H
GRID_W = 64
LN_EPS = 1e-6
LANES = 128
VMEM_LIMIT = 60 * 1024 * 1024

F32 = jnp.float32
BF16 = jnp.bfloat16
NT_DIMS = (((1,), (1,)), ((), ()))
TN_DIMS = (((0,), (0,)), ((), ()))


def _cparams(sem):
    return pltpu.CompilerParams(dimension_semantics=sem, vmem_limit_bytes=VMEM_LIMIT)


def _layer_spec(shape, *idx):
    return pl.BlockSpec((None,) + shape, lambda *_: idx, pipeline_mode=pl.Buffered(1))


def _silu(v):
    return v * jax.nn.sigmoid(v)


def _adaln_kernel(condT_ref, w_ref, b_ref, o_ref, s_scr):
    @pl.when(jnp.logical_and(pl.program_id(0) == 0, pl.program_id(1) == 0))
    def _():
        s_scr[...] = _silu(condT_ref[...])

    w = w_ref[0]
    for i in range(2):
        o_ref[0, i:i + 1, :] = jnp.sum(w * s_scr[:, i:i + 1], axis=0, keepdims=True) + b_ref[0]


def _adaln(condT, w_ada, b_ada):
    depth, _, n3 = w_ada.shape
    tn = 2048
    return pl.pallas_call(
        _adaln_kernel,
        grid=(depth, n3 // tn),
        in_specs=[pl.BlockSpec((D, 2), lambda l, j: (0, 0)),
                  pl.BlockSpec((1, D, tn), lambda l, j: (l, 0, j)),
                  pl.BlockSpec((1, 1, tn), lambda l, j: (l, 0, j))],
        out_specs=pl.BlockSpec((1, 2, tn), lambda l, j: (l, 0, j)),
        out_shape=jax.ShapeDtypeStruct((depth, 2, n3), F32),
        scratch_shapes=[pltpu.VMEM((D, 2), F32)],
        compiler_params=_cparams(("arbitrary", "arbitrary")),
        name="adaln",
    )(condT, w_ada, b_ada.reshape(depth, 1, n3))


GP = 8
PRM_ROWS = 40


def _s5prep_kernel(prm_ref, bre_ref, bim_ref, d_ref, win_ref, wglu_ref,
                   w1t_ref, w2_ref, acoef_ref, wrow_ref, wt_ref, wgt_ref, kt_scr):
    wn = win_ref[...]
    wt_ref[...] = wn[:, 0:2 * DS].T.astype(BF16)
    wrow_ref[...] = wn[:, 2 * DS:].astype(BF16)
    wgt_ref[...] = wglu_ref[...].T.astype(BF16)
    lane = lax.broadcasted_iota(jnp.int32, (1, LANES), 1)
    fwd_lane = lane < P
    krow = lax.broadcasted_iota(jnp.int32, (T, 1), 0)
    fwd_row = lax.broadcasted_iota(jnp.int32, (2 * P, 1), 0) < P
    col256 = lax.broadcasted_iota(jnp.int32, (T, TH), 1)
    r16 = lax.broadcasted_iota(jnp.int32, (T, TH), 0)
    diag_mask = (col256 % H == r16).astype(F32)
    e_rep = diag_mask.astype(BF16)
    e_blk = (col256 // H == r16).astype(BF16)

    def rows16(tab):
        return jnp.broadcast_to(tab[:, None, :], (T, H, tab.shape[1])).reshape(TH, tab.shape[1])

    def tile16(mat):
        return jnp.broadcast_to(mat[None], (T, H, mat.shape[1])).reshape(TH, mat.shape[1])

    def bdot(a, b):
        return jnp.dot(a.astype(BF16), b.astype(BF16), preferred_element_type=F32)

    def expand(tab, e):
        return lax.dot_general(tab.astype(BF16), e, TN_DIMS, preferred_element_type=F32)

    for gi in range(GP):
        ar, ai = prm_ref[0, gi, 0:1, :], prm_ref[0, gi, 1:2, :]
        dt = jnp.exp(prm_ref[0, gi, 2:3, :])
        la, th = ar * dt, ai * dt
        cr, ci = tile16(prm_ref[0, gi, 8:24, :]), tile16(prm_ref[0, gi, 24:40, :])
        b_x = [bdot(jnp.concatenate([b_ref[0, 0, gi], b_ref[0, 1, gi]], axis=0), e_rep)
               for b_ref in (bre_ref, bim_ref)]

        mag1 = jnp.exp(la)
        a1r, a1i = mag1 * jnp.cos(th), mag1 * jnp.sin(th)
        squares = [(a1r, a1i)]
        for _ in range(4):
            sr, si = squares[-1]
            squares.append((sr * sr - si * si, 2.0 * sr * si))

        zr, zi = a1r - 1.0, a1i
        den = ar * ar + ai * ai
        fr, fi = (zr * ar + zi * ai) / den, (zi * ar - zr * ai) / den

        def powrow(kf, kb, with_f):
            k = jnp.where(fwd_lane, kf, kb)
            if with_f:
                pr, pi = jnp.broadcast_to(fr, k.shape), jnp.broadcast_to(fi, k.shape)
            else:
                pr, pi = jnp.ones(k.shape, F32), jnp.zeros(k.shape, F32)
            for b, (sr, si) in enumerate(squares):
                bit = ((k >> b) & 1) == 1
                pr, pi = jnp.where(bit, pr * sr - pi * si, pr), jnp.where(bit, pr * si + pi * sr, pi)
            return pr, pi

        def ctab(kf, kb, with_f):
            pr, pi = powrow(kf, kb, with_f)
            pr, pi = rows16(pr), rows16(pi)
            return cr * pr - ci * pi, cr * pi + ci * pr

        bbr_x, bbi_x = b_x
        ckr, cki = ctab(krow, (T - 1) - krow, True)
        zero = jnp.zeros_like(bbr_x)
        kf = bdot(ckr, jnp.where(fwd_row, bbr_x, zero)) - bdot(cki, jnp.where(fwd_row, bbi_x, zero))
        kb = bdot(ckr, jnp.where(fwd_row, zero, bbr_x)) - bdot(cki, jnp.where(fwd_row, zero, bbi_x))
        kt_scr[0:TH - H, :] = kb[0:TH - H]
        kt_scr[TH - H:TH, :] = kb[TH - H:TH] + kf[0:H] + diag_mask * d_ref[0, gi:gi + 1, :]
        kt_scr[TH:2 * TH - H, :] = kf[H:TH]
        for r0 in range(0, TH, LANES):
            for c0 in range(0, TH, LANES):
                blk = (lax.broadcasted_iota(jnp.int32, (LANES, LANES), 1) + c0) // H
                m = jnp.zeros((LANES, LANES), F32)
                for s in range(c0 // H, (c0 + LANES) // H):
                    w0 = (T - 1 - s) * H + r0
                    m = jnp.where(blk == s, kt_scr[w0:w0 + LANES, c0:c0 + LANES], m)
                w1t_ref[0, gi, r0:r0 + LANES, c0:c0 + LANES] = m.astype(BF16)

        pir, pii = powrow((T - 1) - krow, krow, True)
        pxr, pxi = expand(pir, e_blk), expand(pii, e_blk)
        w1t_ref[0, gi, TH:TH + 2 * P, :] = (pxr * bbr_x - pxi * bbi_x).astype(BF16)
        w1t_ref[0, gi, TH + 2 * P:2 * TH, :] = (pxr * bbi_x + pxi * bbr_x).astype(BF16)

        cor, coi = ctab(krow + 1, T - krow, False)
        w2_ref[0, gi, :, 0:LANES] = cor.astype(BF16)
        w2_ref[0, gi, :, LANES:2 * LANES] = (-coi).astype(BF16)

        acoef_ref[0, 0, gi:gi + 1, :] = squares[4][0]
        acoef_ref[0, 1, gi:gi + 1, :] = squares[4][1]


def _s5prep(a_re, a_im, log_dt, b_re, b_im, c_re, c_im, d_skip, w_in, w_glu):
    depth = a_re.shape[0]
    nsteps = depth * (G // GP)
    wr = D // nsteps
    gr = DS // (G // GP)
    step = lambda l, g: l * (G // GP) + g
    d_cols = jnp.tile(d_skip.reshape(depth, G, H), (1, 1, T))

    def dircat(v):
        return jnp.concatenate([v[:, 0], v[:, 1]], axis=-1)

    ld = jnp.broadcast_to(log_dt[..., None, None], log_dt.shape + (1, P))
    pad = jnp.zeros((depth, G, 5, 2 * P), F32)
    prm = jnp.concatenate([dircat(a_re[:, :, :, None, :]), dircat(a_im[:, :, :, None, :]), dircat(ld), pad,
                           dircat(c_re), dircat(c_im)], axis=2)
    b_spec = pl.BlockSpec((1, 2, GP, P, H), lambda l, g: (l, 0, g, 0, 0))
    return pl.pallas_call(
        _s5prep_kernel,
        grid=(depth, G // GP),
        in_specs=[pl.BlockSpec((1, GP, PRM_ROWS, 2 * P), lambda l, g: (l, g, 0, 0)), b_spec, b_spec,
                  pl.BlockSpec((1, GP, TH), lambda l, g: (l, g, 0)),
                  pl.BlockSpec((None, wr, 6 * DS), lambda l, g: (0, step(l, g), 0)),
                  pl.BlockSpec((None, gr, DS), lambda l, g: (l, g, 0))],
        out_specs=[pl.BlockSpec((1, GP, 2 * TH, TH), lambda l, g: (l, g, 0, 0)),
                   pl.BlockSpec((1, GP, TH, TH), lambda l, g: (l, g, 0, 0)),
                   pl.BlockSpec((1, 2, GP, 2 * P), lambda l, g: (l, 0, g, 0)),
                   pl.BlockSpec((None, wr, 4 * DS), lambda l, g: (0, step(l, g), 0)),
                   pl.BlockSpec((None, 2 * DS, wr), lambda l, g: (0, 0, step(l, g))),
                   pl.BlockSpec((None, DS, gr), lambda l, g: (l, 0, g))],
        out_shape=[jax.ShapeDtypeStruct((depth, G, 2 * TH, TH), BF16),
                   jax.ShapeDtypeStruct((depth, G, TH, TH), BF16),
                   jax.ShapeDtypeStruct((depth, 2, G, 2 * P), F32),
                   jax.ShapeDtypeStruct((1, D, 4 * DS), BF16),
                   jax.ShapeDtypeStruct((1, 2 * DS, D), BF16),
                   jax.ShapeDtypeStruct((depth, DS, DS), BF16)],
        scratch_shapes=[pltpu.VMEM((2 * TH, TH), F32)],
        compiler_params=_cparams(("parallel", "parallel")),
        name="s5prep",
    )(prm, b_re, b_im, d_cols, w_in, w_glu)


def _plane_copies(hbm3, buf, sem, step, slot, *, cb, tb, ntt, to_hbm):
    b, tt = step // ntt, step % ntt
    copies = []
    for t in range(tb):
        hbm = hbm3.at[pl.ds(b * cb, cb), tt * tb + t, :]
        vm = buf.at[slot, t]
        copies.append(pltpu.make_async_copy(vm, hbm, sem.at[slot, t]) if to_hbm
                      else pltpu.make_async_copy(hbm, vm, sem.at[slot, t]))
    return copies


def _fetch_planes(x_hbm, xbuf, xsem, step, nsteps, **kw):
    slot = step % 2
    mk = functools.partial(_plane_copies, x_hbm, xbuf, xsem, to_hbm=False, **kw)

    @pl.when(step == 0)
    def _():
        for cp in mk(step, slot):
            cp.start()

    @pl.when(step + 1 < nsteps)
    def _():
        for cp in mk(step + 1, 1 - slot):
            cp.start()

    for cp in mk(step, slot):
        cp.wait()
    return slot


def _layernorm_rows(v):
    mu = jnp.mean(v, axis=-1, keepdims=True)
    vc = v - mu
    var = jnp.mean(vc * vc, axis=-1, keepdims=True)
    return vc * lax.rsqrt(var + LN_EPS)


def _inproj_kernel(*refs, cb, tb, nb, ntt, u_only, cast_next):
    if u_only:
        x_hbm, mod_ref, wut_ref, ut_ref, xbuf, xsem, h_scr = refs
    elif cast_next:
        (x_hbm, mod_ref, wut_ref, wzt_ref, wv_ref, wb_ref, wc_ref, wzc_ref, wnext_ref,
         ut_ref, szt_ref, s_ref, bg_ref, szc_ref, wrow_next_ref, wt_next_ref, xbuf, xsem, h_scr) = refs
        wn = wnext_ref[...]
        wt_next_ref[...] = wn[:, 0:2 * DS].T.astype(BF16)
        wrow_next_ref[...] = wn[:, 2 * DS:].astype(BF16)
    else:
        (x_hbm, mod_ref, wut_ref, wzt_ref, wv_ref, wb_ref, wc_ref, wzc_ref,
         ut_ref, szt_ref, s_ref, bg_ref, szc_ref, xbuf, xsem, h_scr) = refs
    step = pl.program_id(0) * ntt + pl.program_id(1)
    slot = _fetch_planes(x_hbm, xbuf, xsem, step, nb * ntt, cb=cb, tb=tb, ntt=ntt)
    shift, scale = mod_ref[0:1, :], mod_ref[1:2, :]
    for t in range(tb):
        hn = _layernorm_rows(xbuf[slot, t]) * (1.0 + scale) + shift
        h_scr[t * cb:(t + 1) * cb, :] = hn.astype(BF16)
    h = h_scr[...]
    ut_ref[...] = lax.dot_general(wut_ref[...], h, NT_DIMS, preferred_element_type=F32).astype(BF16)
    if u_only:
        return
    szt_ref[...] = _silu(lax.dot_general(wzt_ref[...], h, NT_DIMS, preferred_element_type=F32)).astype(BF16)
    v = jnp.dot(h, wv_ref[...], preferred_element_type=F32)
    cg = jnp.dot(h, wc_ref[...], preferred_element_type=F32)
    s_ref[...] = (cg * v).astype(BF16)
    bg_ref[...] = jnp.dot(h, wb_ref[...], preferred_element_type=F32).astype(BF16)
    szc_ref[...] = _silu(jnp.dot(h, wzc_ref[...], preferred_element_type=F32)).astype(BF16)


def _inproj(x3, mod, wt, wrow, mod_idx, cb, tb, u_only=False, w_next=None, next_layer=0):
    nc = x3.shape[0]
    n = nc * T
    nb, ntt = nc // cb, T // tb
    r = cb * tb
    col = lambda b, tt: (0, b * ntt + tt)
    row = lambda b, tt: (b * ntt + tt, 0)
    in_specs = [pl.BlockSpec(memory_space=pl.ANY), _layer_spec((3, D), mod_idx, 0, 0),
                _layer_spec((DS, D), 0, 0, 0)]
    out_specs = [pl.BlockSpec((DS, r), col)]
    out_shape = [jax.ShapeDtypeStruct((DS, n), BF16)]
    args = [x3, mod, wt]
    if not u_only:
        in_specs += [_layer_spec((DS, D), 0, 1, 0)] + [_layer_spec((D, DS), 0, 0, k) for k in range(4)]
        out_specs += [pl.BlockSpec((DS, r), col)] + [pl.BlockSpec((r, DS), row)] * 3
        out_shape += [jax.ShapeDtypeStruct((DS, n), BF16)] + [jax.ShapeDtypeStruct((n, DS), BF16)] * 3
        args += [wt, wrow, wrow, wrow, wrow]
    if w_next is not None:
        wr = D // (nb * ntt)
        in_specs += [pl.BlockSpec((None, wr, 6 * DS), lambda b, tt: (next_layer, b * ntt + tt, 0))]
        out_specs += [pl.BlockSpec((None, wr, 4 * DS), lambda b, tt: (0, b * ntt + tt, 0)),
                      pl.BlockSpec((None, 2 * DS, wr), lambda b, tt: (0, 0, b * ntt + tt))]
        out_shape += [jax.ShapeDtypeStruct((1, D, 4 * DS), BF16), jax.ShapeDtypeStruct((1, 2 * DS, D), BF16)]
        args += [w_next]
    return pl.pallas_call(
        functools.partial(_inproj_kernel, cb=cb, tb=tb, nb=nb, ntt=ntt, u_only=u_only,
                          cast_next=w_next is not None),
        grid=(nb, ntt),
        in_specs=in_specs,
        out_specs=out_specs,
        out_shape=out_shape,
        scratch_shapes=[pltpu.VMEM((2, tb, cb, D), F32), pltpu.SemaphoreType.DMA((2, tb)),
                        pltpu.VMEM((r, D), BF16)],
        compiler_params=_cparams(("arbitrary", "arbitrary")),
        name="inproj_u" if u_only else "inproj",
    )(*args)


GB = 8
NCL = 512
NCC = 16
NCT = NCL + NCC
NCP = 640
CBL = 128


def _s5_kernel(ut_ref, utc_ref, w1t_ref, w2_ref, acoef_ref, wo_ref, yt_ref, ytc_ref, wo_bf_ref,
               sre_scr, sim_scr, hre_scr, him_scr, yloc_scr, *, ctx_out):
    nbl = NCL // CBL
    wo_bf_ref[...] = wo_ref[...].astype(BF16)
    for gl in range(GB):
        rows = slice(gl * H, (gl + 1) * H)
        planes = []
        for t in range(T):
            lat = [ut_ref[rows, (b * T + t) * CBL:(b * T + t + 1) * CBL] for b in range(nbl)]
            planes.append(jnp.concatenate(lat + [utc_ref[t, rows, :]], axis=1))
        z = jnp.concatenate(planes, axis=0)
        r1 = jnp.dot(w1t_ref[gl], z, preferred_element_type=F32)
        yloc_scr[gl] = r1[0:TH]
        st = r1[TH:2 * TH].T
        sre_scr[pl.ds(gl, NCP, stride=GB), :] = st[:, 0:LANES]
        sim_scr[pl.ds(gl, NCP, stride=GB), :] = st[:, LANES:2 * LANES]

    ar, ai = acoef_ref[0], acoef_ref[1]
    fwd_lane = lax.broadcasted_iota(jnp.int32, (GB, LANES), 1) < P
    bwd_lane = jnp.logical_not(fwd_lane)
    hre_scr[NCT * GB:NCP * GB, :] = jnp.zeros(((NCP - NCT) * GB, LANES), F32)
    him_scr[NCT * GB:NCP * GB, :] = jnp.zeros(((NCP - NCT) * GB, LANES), F32)

    def step(k, carry, fwd_shift):
        hr, hi = carry
        cf = k + fwd_shift
        cb_ = NCT - 1 - k
        rf = pl.ds(pl.multiple_of(cf * GB, GB), GB)
        rb = pl.ds(pl.multiple_of(cb_ * GB, GB), GB)
        pltpu.store(hre_scr.at[rf, :], hr, mask=fwd_lane)
        pltpu.store(hre_scr.at[rb, :], hr, mask=bwd_lane)
        pltpu.store(him_scr.at[rf, :], hi, mask=fwd_lane)
        pltpu.store(him_scr.at[rb, :], hi, mask=bwd_lane)
        sr = jnp.where(fwd_lane, sre_scr[rf, :], sre_scr[rb, :])
        si = jnp.where(fwd_lane, sim_scr[rf, :], sim_scr[rb, :])
        return ar * hr - ai * hi + sr, ar * hi + ai * hr + si

    zero = jnp.zeros((GB, LANES), F32)
    carry = lax.fori_loop(0, NCC, functools.partial(step, fwd_shift=NCL), (zero, zero))
    lax.fori_loop(NCC, NCT, functools.partial(step, fwd_shift=-NCC), carry, unroll=4)

    for gl in range(GB):
        rows = slice(gl * H, (gl + 1) * H)
        sel = pl.ds(gl, NCP, stride=GB)
        hp = jnp.concatenate([hre_scr[sel, :], him_scr[sel, :]], axis=1)
        y = yloc_scr[gl] + lax.dot_general(w2_ref[gl], hp.astype(BF16), NT_DIMS, preferred_element_type=F32)
        yb = y.astype(BF16)
        for t in range(T):
            for b in range(nbl):
                yt_ref[rows, (b * T + t) * CBL:(b * T + t + 1) * CBL] = \
                    yb[t * H:(t + 1) * H, b * CBL:(b + 1) * CBL]
            if ctx_out:
                ytc_ref[t, rows, :] = yb[t * H:(t + 1) * H, NCL:NCP]
    if not ctx_out:
        ytc_ref[...] = jnp.zeros(ytc_ref.shape, BF16)


def _s5(ut, utc, w1t, w2, acoef, w_out, layer, ctx_out):
    n = ut.shape[1]
    ncx = NCP - NCL
    wr = 2 * DS // (G // GB)
    state = pltpu.VMEM((NCP * GB, LANES), F32)
    return pl.pallas_call(
        functools.partial(_s5_kernel, ctx_out=ctx_out),
        grid=(G // GB,),
        in_specs=[pl.BlockSpec((GB * H, n), lambda i: (i, 0)),
                  pl.BlockSpec((T, GB * H, ncx), lambda i: (0, i, 0)),
                  pl.BlockSpec((None, GB, 2 * TH, TH), lambda i: (layer, i, 0, 0)),
                  pl.BlockSpec((None, GB, TH, TH), lambda i: (layer, i, 0, 0)),
                  pl.BlockSpec((None, 2, GB, LANES), lambda i: (layer, 0, i, 0)),
                  pl.BlockSpec((None, wr, D), lambda i: (layer, i, 0))],
        out_specs=[pl.BlockSpec((GB * H, n), lambda i: (i, 0)),
                   pl.BlockSpec((T, GB * H, ncx), lambda i: (0, i, 0)),
                   pl.BlockSpec((None, wr, D), lambda i: (0, i, 0))],
        out_shape=[jax.ShapeDtypeStruct((DS, n), BF16), jax.ShapeDtypeStruct((T, DS, ncx), BF16),
                   jax.ShapeDtypeStruct((1, 2 * DS, D), BF16)],
        scratch_shapes=[state, state, state, state, pltpu.VMEM((GB, TH, NCP), F32)],
        compiler_params=_cparams(("parallel",)),
        name="s5",
    )(ut, utc, w1t, w2, acoef, w_out)


def _outproj_kernel(x_hbm, mod_ref, yt_ref, szt_ref, s_ref, sprev_ref, snext_ref, bg_ref, szc_ref,
                    wgt_ref, bglu_ref, cw_ref, cb_ref, wo_ref, lng_ref, lnb_ref,
                    o_hbm, xbuf, xsem, obuf, osem, *, cb, tb, nb, ntt, row_chunks, alpha):
    tt = pl.program_id(1)
    step = pl.program_id(0) * ntt + tt
    nsteps = nb * ntt
    slot = _fetch_planes(x_hbm, xbuf, xsem, step, nsteps, cb=cb, tb=tb, ntt=ntt)
    g = jax.nn.gelu(yt_ref[...].astype(F32))
    gate = jnp.dot(wgt_ref[...], g.astype(BF16), preferred_element_type=F32) + bglu_ref[...]
    o5 = (g * jax.nn.sigmoid(gate) * szt_ref[...].astype(F32)).T.astype(BF16)
    crow = lax.broadcasted_iota(jnp.int32, (cb, 1), 0) % row_chunks
    sp = sprev_ref[...].astype(F32)
    sn = snext_ref[...].astype(F32)
    sp_wrap = jnp.where(crow != 0, pltpu.roll(sp, 1, 0), 0.0)
    sn_wrap = jnp.where(crow != row_chunks - 1, pltpu.roll(sn, cb - 1, 0), 0.0)
    sp = jnp.where(tt == 0, sp_wrap, sp).astype(BF16)
    sn = jnp.where(tt == ntt - 1, sn_wrap, sn).astype(BF16)
    s_all = jnp.concatenate([sp, s_ref[...], sn], axis=0)
    r = cb * tb
    cw = cw_ref[...].astype(BF16)
    conv = (cw[0:1, :] * s_all[0:r] + cw[1:2, :] * s_all[cb:cb + r]
            + cw[2:3, :] * s_all[2 * cb:2 * cb + r] + cb_ref[...].astype(BF16))
    ocv = bg_ref[...] * conv * szc_ref[...]
    sub = jnp.dot(jnp.concatenate([o5, ocv], axis=1), wo_ref[...], preferred_element_type=F32)
    gate_vec = mod_ref[2:3, :]
    for t in range(tb):
        v = alpha * xbuf[slot, t] + gate_vec * sub[t * cb:(t + 1) * cb]
        obuf[slot, t] = _layernorm_rows(v) * lng_ref[...] + lnb_ref[...]
    put = functools.partial(_plane_copies, o_hbm, obuf, osem, cb=cb, tb=tb, ntt=ntt, to_hbm=True)

    @pl.when(step > 0)
    def _():
        for cp in put(step - 1, 1 - slot):
            cp.wait()

    for cp in put(step, slot):
        cp.start()

    @pl.when(step == nsteps - 1)
    def _():
        for cp in put(step, slot):
            cp.wait()


def _outproj(x3, mod, yt, szt, s, bg, szc, wo, wts, layer, mod_idx, cb, tb, row_chunks, alpha):
    nc = x3.shape[0]
    nb, ntt = nc // cb, T // tb
    r = cb * tb
    col = lambda b, tt: (0, b * ntt + tt)
    row = lambda b, tt: (b * ntt + tt, 0)
    prev_plane = lambda b, tt: (b * T + (tt * tb + T - 1) % T, 0)
    next_plane = lambda b, tt: (b * T + (tt * tb + tb) % T, 0)
    wgt, bglu, cw, cbias, lng, lnb = wts
    planes = pltpu.VMEM((2, tb, cb, D), F32)
    return pl.pallas_call(
        functools.partial(_outproj_kernel, cb=cb, tb=tb, nb=nb, ntt=ntt, row_chunks=row_chunks, alpha=alpha),
        grid=(nb, ntt),
        in_specs=[pl.BlockSpec(memory_space=pl.ANY),
                  _layer_spec((3, D), mod_idx, 0, 0),
                  pl.BlockSpec((DS, r), col), pl.BlockSpec((DS, r), col),
                  pl.BlockSpec((r, DS), row),
                  pl.BlockSpec((cb, DS), prev_plane), pl.BlockSpec((cb, DS), next_plane),
                  pl.BlockSpec((r, DS), row), pl.BlockSpec((r, DS), row),
                  _layer_spec((DS, DS), layer, 0, 0), _layer_spec((DS, 1), layer, 0, 0),
                  _layer_spec((3, DS), layer, 0, 0), _layer_spec((1, DS), layer, 0, 0),
                  _layer_spec((2 * DS, D), 0, 0, 0),
                  _layer_spec((1, D), layer, 0, 0), _layer_spec((1, D), layer, 0, 0)],
        out_specs=pl.BlockSpec(memory_space=pl.ANY),
        out_shape=jax.ShapeDtypeStruct(x3.shape, F32),
        scratch_shapes=[planes, pltpu.SemaphoreType.DMA((2, tb)), planes, pltpu.SemaphoreType.DMA((2, tb))],
        compiler_params=_cparams(("arbitrary", "arbitrary")),
        name="outproj",
    )(x3, mod, yt, szt, s, s, s, bg, szc, wgt, bglu, cw, cbias, wo, lng, lnb)


def kernel(x, c, ctx, c_ctx, w_ada, b_ada, w_in, s5_a_re, s5_a_im, s5_log_dt, s5_b_re, s5_b_im,
           s5_c_re, s5_c_im, s5_d, w_glu, b_glu, conv_w, conv_b, w_out, ln_g, ln_b):
    bsz, n, d = x.shape
    n_ctx = ctx.shape[1]
    depth = w_in.shape[0]
    assert bsz == 1 and d == D and n == NCL * T and n_ctx == NCC * T
    alpha = (2.0 * depth) ** 0.25

    condT = jnp.stack([c[0], c_ctx], axis=1)
    mod = _adaln(condT, w_ada, b_ada).reshape(depth * 2, 3, D)
    w1t, w2, acoef, w_row, w_t, w_glu_t = _s5prep(s5_a_re, s5_a_im, s5_log_dt, s5_b_re, s5_b_im,
                                                  s5_c_re, s5_c_im, s5_d, w_in, w_glu)
    out_w = (w_glu_t, b_glu.reshape(depth, DS, 1), conv_w, conv_b.reshape(depth, 1, DS),
             ln_g.reshape(depth, 1, D), ln_b.reshape(depth, 1, D))

    x3 = x.reshape(n // T, T, D)
    c3 = ctx.reshape(n_ctx // T, T, D)
    tb = 4
    for i in range(depth):
        last = i == depth - 1
        ut, szt, s, bg, szc, *w_next = _inproj(x3, mod, w_t, w_row, 2 * i, CBL, tb,
                                               w_next=None if last else w_in, next_layer=i + 1)
        utc2, *ctx_rest = _inproj(c3, mod, w_t, w_row, 2 * i + 1, NCC, T, u_only=last)
        utc = jnp.pad(utc2.reshape(DS, T, NCC).transpose(1, 0, 2),
                      ((0, 0), (0, 0), (0, NCP - NCT)))
        yt, ytc, wo = _s5(ut, utc, w1t, w2, acoef, w_out, i, ctx_out=not last)
        x3 = _outproj(x3, mod, yt, szt, s, bg, szc, wo, out_w, i, 2 * i, CBL, tb, GRID_W // T, alpha)
        if not last:
            ytc2 = ytc[:, :, 0:NCC].transpose(1, 0, 2).reshape(DS, n_ctx)
            c3 = _outproj(c3, mod, ytc2, *ctx_rest, wo, out_w, i, 2 * i + 1, NCC, T, NCC, alpha)
            w_row, w_t = w_next
    return x3.reshape(bsz, n, D)
```

```python
import functools

import jax
import jax.numpy as jnp
from jax import lax
from jax.experimental import pallas as pl
from jax.experimental.pallas import tpu as pltpu

D = 2048
DS = 1024
G = 64
H = 16
P = 64
T = 16
TH = T * H
GRID_W = 64
LN_EPS = 1e-6
LANES = 128
VMEM_LIMIT = 60 * 1024 * 1024

F32 = jnp.float32
BF16 = jnp.bfloat16
NT_DIMS = (((1,), (1,)), ((), ()))
TN_DIMS = (((0,), (0,)), ((), ()))


def _cparams(sem):
    return pltpu.CompilerParams(dimension_semantics=sem, vmem_limit_bytes=VMEM_LIMIT)


def _layer_spec(shape, *idx):
    return pl.BlockSpec((None,) + shape, lambda *_: idx, pipeline_mode=pl.Buffered(1))


def _silu(v):
    return v * jax.nn.sigmoid(v)


def _adaln_block(condT_ref, w_ref, b_ref, o_ref, s_scr, first):
    @pl.when(first)
    def _():
        s_scr[...] = _silu(condT_ref[...])

    w = w_ref[...]
    for i in range(2):
        o_ref[i:i + 1, :] = jnp.sum(w * s_scr[:, i:i + 1], axis=0, keepdims=True) + b_ref[...]


GP = 8
PRM_ROWS = 40


def _s5prep_kernel(prm_ref, bre_ref, bim_ref, d_ref, win_ref, wglu_ref, condT_ref, wada_ref, bada_ref,
                   w1t_ref, w2_ref, acoef_ref, wrow_ref, wt_ref, wgt_ref, mod_ref, kt_scr, s_scr):
    first = jnp.logical_and(pl.program_id(0) == 0, pl.program_id(1) == 0)
    _adaln_block(condT_ref, wada_ref, bada_ref, mod_ref, s_scr, first)
    wn = win_ref[...]
    wt_ref[...] = wn[:, 0:2 * DS].T.astype(BF16)
    wrow_ref[...] = wn[:, 2 * DS:].astype(BF16)
    wgt_ref[...] = wglu_ref[...].T.astype(BF16)
    lane = lax.broadcasted_iota(jnp.int32, (1, LANES), 1)
    fwd_lane = lane < P
    krow = lax.broadcasted_iota(jnp.int32, (T, 1), 0)
    fwd_row = lax.broadcasted_iota(jnp.int32, (2 * P, 1), 0) < P
    col256 = lax.broadcasted_iota(jnp.int32, (T, TH), 1)
    r16 = lax.broadcasted_iota(jnp.int32, (T, TH), 0)
    diag_mask = (col256 % H == r16).astype(F32)
    e_rep = diag_mask.astype(BF16)
    e_blk = (col256 // H == r16).astype(BF16)

    def rows16(tab):
        return jnp.broadcast_to(tab[:, None, :], (T, H, tab.shape[1])).reshape(TH, tab.shape[1])

    def tile16(mat):
        return jnp.broadcast_to(mat[None], (T, H, mat.shape[1])).reshape(TH, mat.shape[1])

    def bdot(a, b):
        return jnp.dot(a.astype(BF16), b.astype(BF16), preferred_element_type=F32)

    def expand(tab, e):
        return lax.dot_general(tab.astype(BF16), e, TN_DIMS, preferred_element_type=F32)

    for gi in range(GP):
        ar, ai = prm_ref[0, gi, 0:1, :], prm_ref[0, gi, 1:2, :]
        dt = jnp.exp(prm_ref[0, gi, 2:3, :])
        la, th = ar * dt, ai * dt
        cr, ci = tile16(prm_ref[0, gi, 8:24, :]), tile16(prm_ref[0, gi, 24:40, :])
        b_x = [bdot(jnp.concatenate([b_ref[0, 0, gi], b_ref[0, 1, gi]], axis=0), e_rep)
               for b_ref in (bre_ref, bim_ref)]

        mag1 = jnp.exp(la)
        a1r, a1i = mag1 * jnp.cos(th), mag1 * jnp.sin(th)
        squares = [(a1r, a1i)]
        for _ in range(4):
            sr, si = squares[-1]
            squares.append((sr * sr - si * si, 2.0 * sr * si))

        zr, zi = a1r - 1.0, a1i
        den = ar * ar + ai * ai
        fr, fi = (zr * ar + zi * ai) / den, (zi * ar - zr * ai) / den

        def powrow(kf, kb, with_f):
            k = jnp.where(fwd_lane, kf, kb)
            if with_f:
                pr, pi = jnp.broadcast_to(fr, k.shape), jnp.broadcast_to(fi, k.shape)
            else:
                pr, pi = jnp.ones(k.shape, F32), jnp.zeros(k.shape, F32)
            for b, (sr, si) in enumerate(squares):
                bit = ((k >> b) & 1) == 1
                pr, pi = jnp.where(bit, pr * sr - pi * si, pr), jnp.where(bit, pr * si + pi * sr, pi)
            return pr, pi

        def ctab(kf, kb, with_f):
            pr, pi = powrow(kf, kb, with_f)
            pr, pi = rows16(pr), rows16(pi)
            return cr * pr - ci * pi, cr * pi + ci * pr

        bbr_x, bbi_x = b_x
        ckr, cki = ctab(krow, (T - 1) - krow, True)
        zero = jnp.zeros_like(bbr_x)
        kf = bdot(ckr, jnp.where(fwd_row, bbr_x, zero)) - bdot(cki, jnp.where(fwd_row, bbi_x, zero))
        kb = bdot(ckr, jnp.where(fwd_row, zero, bbr_x)) - bdot(cki, jnp.where(fwd_row, zero, bbi_x))
        kt_scr[0:TH - H, :] = kb[0:TH - H]
        kt_scr[TH - H:TH, :] = kb[TH - H:TH] + kf[0:H] + diag_mask * d_ref[0, gi:gi + 1, :]
        kt_scr[TH:2 * TH - H, :] = kf[H:TH]
        for r0 in range(0, TH, LANES):
            for c0 in range(0, TH, LANES):
                blk = (lax.broadcasted_iota(jnp.int32, (LANES, LANES), 1) + c0) // H
                m = jnp.zeros((LANES, LANES), F32)
                for s in range(c0 // H, (c0 + LANES) // H):
                    w0 = (T - 1 - s) * H + r0
                    m = jnp.where(blk == s, kt_scr[w0:w0 + LANES, c0:c0 + LANES], m)
                w1t_ref[0, gi, r0:r0 + LANES, c0:c0 + LANES] = m.astype(BF16)

        pir, pii = powrow((T - 1) - krow, krow, True)
        pxr, pxi = expand(pir, e_blk), expand(pii, e_blk)
        w1t_ref[0, gi, TH:TH + 2 * P, :] = (pxr * bbr_x - pxi * bbi_x).astype(BF16)
        w1t_ref[0, gi, TH + 2 * P:2 * TH, :] = (pxr * bbi_x + pxi * bbr_x).astype(BF16)

        cor, coi = ctab(krow + 1, T - krow, False)
        w2_ref[0, gi, :, 0:LANES] = cor.astype(BF16)
        w2_ref[0, gi, :, LANES:2 * LANES] = (-coi).astype(BF16)

        acoef_ref[0, 0, gi:gi + 1, :] = squares[4][0]
        acoef_ref[0, 1, gi:gi + 1, :] = squares[4][1]


def _s5prep(a_re, a_im, log_dt, b_re, b_im, c_re, c_im, d_skip, w_in, w_glu, condT, w_ada, b_ada):
    depth = a_re.shape[0]
    nsteps = depth * (G // GP)
    wr = D // nsteps
    gr = DS // (G // GP)
    n3 = w_ada.shape[2]
    mc = n3 // (G // GP)
    step = lambda l, g: l * (G // GP) + g
    d_cols = jnp.tile(d_skip.reshape(depth, G, H), (1, 1, T))

    def dircat(v):
        return jnp.concatenate([v[:, 0], v[:, 1]], axis=-1)

    ld = jnp.broadcast_to(log_dt[..., None, None], log_dt.shape + (1, P))
    pad = jnp.zeros((depth, G, 5, 2 * P), F32)
    prm = jnp.concatenate([dircat(a_re[:, :, :, None, :]), dircat(a_im[:, :, :, None, :]), dircat(ld), pad,
                           dircat(c_re), dircat(c_im)], axis=2)
    b_spec = pl.BlockSpec((1, 2, GP, P, H), lambda l, g: (l, 0, g, 0, 0))
    return pl.pallas_call(
        _s5prep_kernel,
        grid=(depth, G // GP),
        in_specs=[pl.BlockSpec((1, GP, PRM_ROWS, 2 * P), lambda l, g: (l, g, 0, 0)), b_spec, b_spec,
                  pl.BlockSpec((1, GP, TH), lambda l, g: (l, g, 0)),
                  pl.BlockSpec((None, wr, 6 * DS), lambda l, g: (0, step(l, g), 0)),
                  pl.BlockSpec((None, gr, DS), lambda l, g: (l, g, 0)),
                  pl.BlockSpec((D, 2), lambda l, g: (0, 0)),
                  pl.BlockSpec((None, D, mc), lambda l, g: (l, 0, g)),
                  pl.BlockSpec((None, 1, mc), lambda l, g: (l, 0, g))],
        out_specs=[pl.BlockSpec((1, GP, 2 * TH, TH), lambda l, g: (l, g, 0, 0)),
                   pl.BlockSpec((1, GP, TH, TH), lambda l, g: (l, g, 0, 0)),
                   pl.BlockSpec((1, 2, GP, 2 * P), lambda l, g: (l, 0, g, 0)),
                   pl.BlockSpec((None, wr, 4 * DS), lambda l, g: (0, step(l, g), 0)),
                   pl.BlockSpec((None, 2 * DS, wr), lambda l, g: (0, 0, step(l, g))),
                   pl.BlockSpec((None, DS, gr), lambda l, g: (l, 0, g)),
                   pl.BlockSpec((None, 2, mc), lambda l, g: (l, 0, g))],
        out_shape=[jax.ShapeDtypeStruct((depth, G, 2 * TH, TH), BF16),
                   jax.ShapeDtypeStruct((depth, G, TH, TH), BF16),
                   jax.ShapeDtypeStruct((depth, 2, G, 2 * P), F32),
                   jax.ShapeDtypeStruct((1, D, 4 * DS), BF16),
                   jax.ShapeDtypeStruct((1, 2 * DS, D), BF16),
                   jax.ShapeDtypeStruct((depth, DS, DS), BF16),
                   jax.ShapeDtypeStruct((depth, 2, n3), F32)],
        scratch_shapes=[pltpu.VMEM((2 * TH, TH), F32), pltpu.VMEM((D, 2), F32)],
        compiler_params=_cparams(("arbitrary", "arbitrary")),
        name="prep",
    )(prm, b_re, b_im, d_cols, w_in, w_glu, condT, w_ada, b_ada.reshape(depth, 1, n3))


def _plane_copies(hbm3, buf, sem, step, slot, *, cb, tb, ntt, to_hbm):
    b, tt = step // ntt, step % ntt
    copies = []
    for t in range(tb):
        hbm = hbm3.at[pl.ds(b * cb, cb), tt * tb + t, :]
        vm = buf.at[slot, t]
        copies.append(pltpu.make_async_copy(vm, hbm, sem.at[slot, t]) if to_hbm
                      else pltpu.make_async_copy(hbm, vm, sem.at[slot, t]))
    return copies


def _fetch_planes(x_hbm, xbuf, xsem, step, nsteps, **kw):
    slot = step % 2
    mk = functools.partial(_plane_copies, x_hbm, xbuf, xsem, to_hbm=False, **kw)

    @pl.when(step == 0)
    def _():
        for cp in mk(step, slot):
            cp.start()

    @pl.when(step + 1 < nsteps)
    def _():
        for cp in mk(step + 1, 1 - slot):
            cp.start()

    for cp in mk(step, slot):
        cp.wait()
    return slot


def _layernorm_rows(v):
    mu = jnp.mean(v, axis=-1, keepdims=True)
    vc = v - mu
    var = jnp.mean(vc * vc, axis=-1, keepdims=True)
    return vc * lax.rsqrt(var + LN_EPS)


def _inproj_kernel(*refs, cb, tb, nb, ntt, u_only, cast_next):
    if u_only:
        x_hbm, mod_ref, wut_ref, ut_ref, xbuf, xsem, h_scr = refs
    elif cast_next:
        (x_hbm, mod_ref, wut_ref, wzt_ref, wv_ref, wb_ref, wc_ref, wzc_ref, wnext_ref,
         ut_ref, szt_ref, s_ref, bg_ref, szc_ref, wrow_next_ref, wt_next_ref, xbuf, xsem, h_scr) = refs
        wn = wnext_ref[...]
        wt_next_ref[...] = wn[:, 0:2 * DS].T.astype(BF16)
        wrow_next_ref[...] = wn[:, 2 * DS:].astype(BF16)
    else:
        (x_hbm, mod_ref, wut_ref, wzt_ref, wv_ref, wb_ref, wc_ref, wzc_ref,
         ut_ref, szt_ref, s_ref, bg_ref, szc_ref, xbuf, xsem, h_scr) = refs
    step = pl.program_id(0) * ntt + pl.program_id(1)
    slot = _fetch_planes(x_hbm, xbuf, xsem, step, nb * ntt, cb=cb, tb=tb, ntt=ntt)
    shift, scale = mod_ref[0:1, :], mod_ref[1:2, :]
    for t in range(tb):
        hn = _layernorm_rows(xbuf[slot, t]) * (1.0 + scale) + shift
        h_scr[t * cb:(t + 1) * cb, :] = hn.astype(BF16)
    h = h_scr[...]
    ut_ref[...] = lax.dot_general(wut_ref[...], h, NT_DIMS, preferred_element_type=F32).astype(BF16)
    if u_only:
        return
    szt_ref[...] = _silu(lax.dot_general(wzt_ref[...], h, NT_DIMS, preferred_element_type=F32)).astype(BF16)
    v = jnp.dot(h, wv_ref[...], preferred_element_type=F32)
    cg = jnp.dot(h, wc_ref[...], preferred_element_type=F32)
    s_ref[...] = (cg * v).astype(BF16)
    bg_ref[...] = jnp.dot(h, wb_ref[...], preferred_element_type=F32).astype(BF16)
    szc_ref[...] = _silu(jnp.dot(h, wzc_ref[...], preferred_element_type=F32)).astype(BF16)


def _inproj(x3, mod, wt, wrow, mod_idx, cb, tb, u_only=False, w_next=None, next_layer=0):
    nc = x3.shape[0]
    n = nc * T
    nb, ntt = nc // cb, T // tb
    r = cb * tb
    col = lambda b, tt: (0, b * ntt + tt)
    row = lambda b, tt: (b * ntt + tt, 0)
    in_specs = [pl.BlockSpec(memory_space=pl.ANY), _layer_spec((3, D), mod_idx, 0, 0),
                _layer_spec((DS, D), 0, 0, 0)]
    out_specs = [pl.BlockSpec((DS, r), col)]
    out_shape = [jax.ShapeDtypeStruct((DS, n), BF16)]
    args = [x3, mod, wt]
    if not u_only:
        in_specs += [_layer_spec((DS, D), 0, 1, 0)] + [_layer_spec((D, DS), 0, 0, k) for k in range(4)]
        out_specs += [pl.BlockSpec((DS, r), col)] + [pl.BlockSpec((r, DS), row)] * 3
        out_shape += [jax.ShapeDtypeStruct((DS, n), BF16)] + [jax.ShapeDtypeStruct((n, DS), BF16)] * 3
        args += [wt, wrow, wrow, wrow, wrow]
    if w_next is not None:
        wr = D // (nb * ntt)
        in_specs += [pl.BlockSpec((None, wr, 6 * DS), lambda b, tt: (next_layer, b * ntt + tt, 0))]
        out_specs += [pl.BlockSpec((None, wr, 4 * DS), lambda b, tt: (0, b * ntt + tt, 0)),
                      pl.BlockSpec((None, 2 * DS, wr), lambda b, tt: (0, 0, b * ntt + tt))]
        out_shape += [jax.ShapeDtypeStruct((1, D, 4 * DS), BF16), jax.ShapeDtypeStruct((1, 2 * DS, D), BF16)]
        args += [w_next]
    return pl.pallas_call(
        functools.partial(_inproj_kernel, cb=cb, tb=tb, nb=nb, ntt=ntt, u_only=u_only,
                          cast_next=w_next is not None),
        grid=(nb, ntt),
        in_specs=in_specs,
        out_specs=out_specs,
        out_shape=out_shape,
        scratch_shapes=[pltpu.VMEM((2, tb, cb, D), F32), pltpu.SemaphoreType.DMA((2, tb)),
                        pltpu.VMEM((r, D), BF16)],
        compiler_params=_cparams(("arbitrary", "arbitrary")),
        name="inproj_u" if u_only else "inproj",
    )(*args)


GB = 8
NCL = 512
NCC = 16
NCT = NCL + NCC
NCP = 640
CBL = 128


def _s5_kernel(ut_ref, utc_ref, w1t_ref, w2_ref, acoef_ref, wo_ref, yt_ref, ytc_ref, wo_bf_ref,
               sre_scr, sim_scr, hre_scr, him_scr, yloc_scr, *, ctx_out):
    nbl = NCL // CBL
    wo_bf_ref[...] = wo_ref[...].astype(BF16)
    for gl in range(GB):
        rows = slice(gl * H, (gl + 1) * H)
        planes = []
        for t in range(T):
            lat = [ut_ref[rows, (b * T + t) * CBL:(b * T + t + 1) * CBL] for b in range(nbl)]
            planes.append(jnp.concatenate(lat + [utc_ref[t, rows, :]], axis=1))
        z = jnp.concatenate(planes, axis=0)
        r1 = jnp.dot(w1t_ref[gl], z, preferred_element_type=F32)
        yloc_scr[gl] = r1[0:TH]
        st = r1[TH:2 * TH].T
        sre_scr[pl.ds(gl, NCP, stride=GB), :] = st[:, 0:LANES]
        sim_scr[pl.ds(gl, NCP, stride=GB), :] = st[:, LANES:2 * LANES]

    ar, ai = acoef_ref[0], acoef_ref[1]
    fwd_lane = lax.broadcasted_iota(jnp.int32, (GB, LANES), 1) < P
    bwd_lane = jnp.logical_not(fwd_lane)
    hre_scr[NCT * GB:NCP * GB, :] = jnp.zeros(((NCP - NCT) * GB, LANES), F32)
    him_scr[NCT * GB:NCP * GB, :] = jnp.zeros(((NCP - NCT) * GB, LANES), F32)

    def step(k, carry, fwd_shift):
        hr, hi = carry
        cf = k + fwd_shift
        cb_ = NCT - 1 - k
        rf = pl.ds(pl.multiple_of(cf * GB, GB), GB)
        rb = pl.ds(pl.multiple_of(cb_ * GB, GB), GB)
        pltpu.store(hre_scr.at[rf, :], hr, mask=fwd_lane)
        pltpu.store(hre_scr.at[rb, :], hr, mask=bwd_lane)
        pltpu.store(him_scr.at[rf, :], hi, mask=fwd_lane)
        pltpu.store(him_scr.at[rb, :], hi, mask=bwd_lane)
        sr = jnp.where(fwd_lane, sre_scr[rf, :], sre_scr[rb, :])
        si = jnp.where(fwd_lane, sim_scr[rf, :], sim_scr[rb, :])
        return ar * hr - ai * hi + sr, ar * hi + ai * hr + si

    zero = jnp.zeros((GB, LANES), F32)
    carry = lax.fori_loop(0, NCC, functools.partial(step, fwd_shift=NCL), (zero, zero))
    lax.fori_loop(NCC, NCT, functools.partial(step, fwd_shift=-NCC), carry, unroll=4)

    for gl in range(GB):
        rows = slice(gl * H, (gl + 1) * H)
        sel = pl.ds(gl, NCP, stride=GB)
        hp = jnp.concatenate([hre_scr[sel, :], him_scr[sel, :]], axis=1)
        y = yloc_scr[gl] + lax.dot_general(w2_ref[gl], hp.astype(BF16), NT_DIMS, preferred_element_type=F32)
        yb = y.astype(BF16)
        for t in range(T):
            for b in range(nbl):
                yt_ref[rows, (b * T + t) * CBL:(b * T + t + 1) * CBL] = \
                    yb[t * H:(t + 1) * H, b * CBL:(b + 1) * CBL]
            if ctx_out:
                ytc_ref[t, rows, :] = yb[t * H:(t + 1) * H, NCL:NCP]
    if not ctx_out:
        ytc_ref[...] = jnp.zeros(ytc_ref.shape, BF16)


def _s5(ut, utc, w1t, w2, acoef, w_out, layer, ctx_out):
    n = ut.shape[1]
    ncx = NCP - NCL
    wr = 2 * DS // (G // GB)
    state = pltpu.VMEM((NCP * GB, LANES), F32)
    return pl.pallas_call(
        functools.partial(_s5_kernel, ctx_out=ctx_out),
        grid=(G // GB,),
        in_specs=[pl.BlockSpec((GB * H, n), lambda i: (i, 0)),
                  pl.BlockSpec((T, GB * H, ncx), lambda i: (0, i, 0)),
                  pl.BlockSpec((None, GB, 2 * TH, TH), lambda i: (layer, i, 0, 0)),
                  pl.BlockSpec((None, GB, TH, TH), lambda i: (layer, i, 0, 0)),
                  pl.BlockSpec((None, 2, GB, LANES), lambda i: (layer, 0, i, 0)),
                  pl.BlockSpec((None, wr, D), lambda i: (layer, i, 0))],
        out_specs=[pl.BlockSpec((GB * H, n), lambda i: (i, 0)),
                   pl.BlockSpec((T, GB * H, ncx), lambda i: (0, i, 0)),
                   pl.BlockSpec((None, wr, D), lambda i: (0, i, 0))],
        out_shape=[jax.ShapeDtypeStruct((DS, n), BF16), jax.ShapeDtypeStruct((T, DS, ncx), BF16),
                   jax.ShapeDtypeStruct((1, 2 * DS, D), BF16)],
        scratch_shapes=[state, state, state, state, pltpu.VMEM((GB, TH, NCP), F32)],
        compiler_params=_cparams(("parallel",)),
        name="s5",
    )(ut, utc, w1t, w2, acoef, w_out)


def _outproj_kernel(x_hbm, mod_ref, yt_ref, szt_ref, s_ref, sprev_ref, snext_ref, bg_ref, szc_ref,
                    wgt_ref, bglu_ref, cw_ref, cb_ref, wo_ref, lng_ref, lnb_ref,
                    o_hbm, xbuf, xsem, obuf, osem, *, cb, tb, nb, ntt, row_chunks, alpha):
    tt = pl.program_id(1)
    step = pl.program_id(0) * ntt + tt
    nsteps = nb * ntt
    slot = _fetch_planes(x_hbm, xbuf, xsem, step, nsteps, cb=cb, tb=tb, ntt=ntt)
    g = jax.nn.gelu(yt_ref[...].astype(F32))
    gate = jnp.dot(wgt_ref[...], g.astype(BF16), preferred_element_type=F32) + bglu_ref[...]
    o5 = (g * jax.nn.sigmoid(gate) * szt_ref[...].astype(F32)).T.astype(BF16)
    crow = lax.broadcasted_iota(jnp.int32, (cb, 1), 0) % row_chunks
    sp = sprev_ref[...].astype(F32)
    sn = snext_ref[...].astype(F32)
    sp_wrap = jnp.where(crow != 0, pltpu.roll(sp, 1, 0), 0.0)
    sn_wrap = jnp.where(crow != row_chunks - 1, pltpu.roll(sn, cb - 1, 0), 0.0)
    sp = jnp.where(tt == 0, sp_wrap, sp).astype(BF16)
    sn = jnp.where(tt == ntt - 1, sn_wrap, sn).astype(BF16)
    s_all = jnp.concatenate([sp, s_ref[...], sn], axis=0)
    r = cb * tb
    cw = cw_ref[...].astype(BF16)
    conv = (cw[0:1, :] * s_all[0:r] + cw[1:2, :] * s_all[cb:cb + r]
            + cw[2:3, :] * s_all[2 * cb:2 * cb + r] + cb_ref[...].astype(BF16))
    ocv = bg_ref[...] * conv * szc_ref[...]
    sub = jnp.dot(jnp.concatenate([o5, ocv], axis=1), wo_ref[...], preferred_element_type=F32)
    gate_vec = mod_ref[2:3, :]
    for t in range(tb):
        v = alpha * xbuf[slot, t] + gate_vec * sub[t * cb:(t + 1) * cb]
        obuf[slot, t] = _layernorm_rows(v) * lng_ref[...] + lnb_ref[...]
    put = functools.partial(_plane_copies, o_hbm, obuf, osem, cb=cb, tb=tb, ntt=ntt, to_hbm=True)

    @pl.when(step > 0)
    def _():
        for cp in put(step - 1, 1 - slot):
            cp.wait()

    for cp in put(step, slot):
        cp.start()

    @pl.when(step == nsteps - 1)
    def _():
        for cp in put(step, slot):
            cp.wait()


def _outproj(x3, mod, yt, szt, s, bg, szc, wo, wts, layer, mod_idx, cb, tb, row_chunks, alpha):
    nc = x3.shape[0]
    nb, ntt = nc // cb, T // tb
    r = cb * tb
    col = lambda b, tt: (0, b * ntt + tt)
    row = lambda b, tt: (b * ntt + tt, 0)
    prev_plane = lambda b, tt: (b * T + (tt * tb + T - 1) % T, 0)
    next_plane = lambda b, tt: (b * T + (tt * tb + tb) % T, 0)
    wgt, bglu, cw, cbias, lng, lnb = wts
    planes = pltpu.VMEM((2, tb, cb, D), F32)
    return pl.pallas_call(
        functools.partial(_outproj_kernel, cb=cb, tb=tb, nb=nb, ntt=ntt, row_chunks=row_chunks, alpha=alpha),
        grid=(nb, ntt),
        in_specs=[pl.BlockSpec(memory_space=pl.ANY),
                  _layer_spec((3, D), mod_idx, 0, 0),
                  pl.BlockSpec((DS, r), col), pl.BlockSpec((DS, r), col),
                  pl.BlockSpec((r, DS), row),
                  pl.BlockSpec((cb, DS), prev_plane), pl.BlockSpec((cb, DS), next_plane),
                  pl.BlockSpec((r, DS), row), pl.BlockSpec((r, DS), row),
                  _layer_spec((DS, DS), layer, 0, 0), _layer_spec((DS, 1), layer, 0, 0),
                  _layer_spec((3, DS), layer, 0, 0), _layer_spec((1, DS), layer, 0, 0),
                  _layer_spec((2 * DS, D), 0, 0, 0),
                  _layer_spec((1, D), layer, 0, 0), _layer_spec((1, D), layer, 0, 0)],
        out_specs=pl.BlockSpec(memory_space=pl.ANY),
        out_shape=jax.ShapeDtypeStruct(x3.shape, F32),
        scratch_shapes=[planes, pltpu.SemaphoreType.DMA((2, tb)), planes, pltpu.SemaphoreType.DMA((2, tb))],
        compiler_params=_cparams(("arbitrary", "arbitrary")),
        name="outproj",
    )(x3, mod, yt, szt, s, s, s, bg, szc, wgt, bglu, cw, cbias, wo, lng, lnb)


def kernel(x, c, ctx, c_ctx, w_ada, b_ada, w_in, s5_a_re, s5_a_im, s5_log_dt, s5_b_re, s5_b_im,
           s5_c_re, s5_c_im, s5_d, w_glu, b_glu, conv_w, conv_b, w_out, ln_g, ln_b):
    bsz, n, d = x.shape
    n_ctx = ctx.shape[1]
    depth = w_in.shape[0]
    assert bsz == 1 and d == D and n == NCL * T and n_ctx == NCC * T
    alpha = (2.0 * depth) ** 0.25

    condT = jnp.stack([c[0], c_ctx], axis=1)
    w1t, w2, acoef, w_row, w_t, w_glu_t, mod = _s5prep(s5_a_re, s5_a_im, s5_log_dt, s5_b_re, s5_b_im,
                                                       s5_c_re, s5_c_im, s5_d, w_in, w_glu, condT, w_ada, b_ada)
    mod = mod.reshape(depth * 2, 3, D)
    out_w = (w_glu_t, b_glu.reshape(depth, DS, 1), conv_w, conv_b.reshape(depth, 1, DS),
             ln_g.reshape(depth, 1, D), ln_b.reshape(depth, 1, D))

    x3 = x.reshape(n // T, T, D)
    c3 = ctx.reshape(n_ctx // T, T, D)
    tb = 4
    for i in range(depth):
        last = i == depth - 1
        ut, szt, s, bg, szc, *w_next = _inproj(x3, mod, w_t, w_row, 2 * i, CBL, tb,
                                               w_next=None if last else w_in, next_layer=i + 1)
        utc2, *ctx_rest = _inproj(c3, mod, w_t, w_row, 2 * i + 1, NCC, T, u_only=last)
        utc = jnp.pad(utc2.reshape(DS, T, NCC).transpose(1, 0, 2),
                      ((0, 0), (0, 0), (0, NCP - NCT)))
        yt, ytc, wo = _s5(ut, utc, w1t, w2, acoef, w_out, i, ctx_out=not last)
        x3 = _outproj(x3, mod, yt, szt, s, bg, szc, wo, out_w, i, 2 * i, CBL, tb, GRID_W // T, alpha)
        if not last:
            ytc2 = ytc[:, :, 0:NCC].transpose(1, 0, 2).reshape(DS, n_ctx)
            c3 = _outproj(c3, mod, ytc2, *ctx_rest, wo, out_w, i, 2 * i + 1, NCC, T, NCC, alpha)
            w_row, w_t = w_next
    return x3.reshape(bsz, n, D)
```

```python
import functools

import jax
import jax.numpy as jnp
from jax import lax
from jax.experimental import pallas as pl
from jax.experimental.pallas import tpu as pltpu

D = 2048
DS = 1024
G = 64
H = 16
P = 64
T = 16
TH = T * H
GRID_W = 64
LN_EPS = 1e-6
LANES = 128
COND_ROWS = 16
VMEM_LIMIT = 60 * 1024 * 1024

F32 = jnp.float32
BF16 = jnp.bfloat16
NT_DIMS = (((1,), (1,)), ((), ()))
TN_DIMS = (((0,), (0,)), ((), ()))


def _cparams(sem):
    return pltpu.CompilerParams(dimension_semantics=sem, vmem_limit_bytes=VMEM_LIMIT)


def _layer_spec(shape, *idx):
    return pl.BlockSpec((None,) + shape, lambda *_: idx, pipeline_mode=pl.Buffered(1))


def _silu(v):
    return v * jax.nn.sigmoid(v)


def _adaln_block(cond_ref, w_ref, b_ref, o_ref, s_scr, first):
    @pl.when(first)
    def _():
        s_scr[...] = _silu(cond_ref[...]).astype(BF16)

    mod = jnp.dot(s_scr[...], w_ref[...].astype(BF16), preferred_element_type=F32)
    o_ref[...] = mod[0:2, :] + b_ref[...]


GP = 8
PRM_ROWS = 40


def _s5prep_kernel(prm_ref, bre_ref, bim_ref, d_ref, win_ref, wglu_ref, cond_ref, wada_ref, bada_ref,
                   w1t_ref, w2_ref, acoef_ref, wrow_ref, wt_ref, wgt_ref, mod_ref, kt_scr, s_scr):
    first = jnp.logical_and(pl.program_id(0) == 0, pl.program_id(1) == 0)
    _adaln_block(cond_ref, wada_ref, bada_ref, mod_ref, s_scr, first)
    wn = win_ref[...]
    wt_ref[...] = wn[:, 0:2 * DS].T.astype(BF16)
    wrow_ref[...] = wn[:, 2 * DS:].astype(BF16)
    wgt_ref[...] = wglu_ref[...].T.astype(BF16)
    lane = lax.broadcasted_iota(jnp.int32, (1, LANES), 1)
    fwd_lane = lane < P
    krow = lax.broadcasted_iota(jnp.int32, (T, 1), 0)
    fwd_row = lax.broadcasted_iota(jnp.int32, (2 * P, 1), 0) < P
    col256 = lax.broadcasted_iota(jnp.int32, (T, TH), 1)
    r16 = lax.broadcasted_iota(jnp.int32, (T, TH), 0)
    diag_mask = (col256 % H == r16).astype(F32)
    e_rep = diag_mask.astype(BF16)
    e_blk = (col256 // H == r16).astype(BF16)

    def rows16(tab):
        return jnp.broadcast_to(tab[:, None, :], (T, H, tab.shape[1])).reshape(TH, tab.shape[1])

    def tile16(mat):
        return jnp.broadcast_to(mat[None], (T, H, mat.shape[1])).reshape(TH, mat.shape[1])

    def bdot(a, b):
        return jnp.dot(a.astype(BF16), b.astype(BF16), preferred_element_type=F32)

    def expand(tab, e):
        return lax.dot_general(tab.astype(BF16), e, TN_DIMS, preferred_element_type=F32)

    for gi in range(GP):
        ar, ai = prm_ref[0, gi, 0:1, :], prm_ref[0, gi, 1:2, :]
        dt = jnp.exp(prm_ref[0, gi, 2:3, :])
        la, th = ar * dt, ai * dt
        cr = tile16(prm_ref[0, gi, 8:24, :].astype(BF16))
        ci = tile16(prm_ref[0, gi, 24:40, :].astype(BF16))
        b_x = [bdot(jnp.concatenate([b_ref[0, 0, gi], b_ref[0, 1, gi]], axis=0), e_rep)
               for b_ref in (bre_ref, bim_ref)]

        mag1 = jnp.exp(la)
        a1r, a1i = mag1 * jnp.cos(th), mag1 * jnp.sin(th)
        squares = [(a1r, a1i)]
        for _ in range(4):
            sr, si = squares[-1]
            squares.append((sr * sr - si * si, 2.0 * sr * si))

        zr, zi = a1r - 1.0, a1i
        den = ar * ar + ai * ai
        fr, fi = (zr * ar + zi * ai) / den, (zi * ar - zr * ai) / den

        def powrow(kf, kb, with_f):
            k = jnp.where(fwd_lane, kf, kb)
            if with_f:
                pr, pi = jnp.broadcast_to(fr, k.shape), jnp.broadcast_to(fi, k.shape)
            else:
                pr, pi = jnp.ones(k.shape, F32), jnp.zeros(k.shape, F32)
            for b, (sr, si) in enumerate(squares):
                bit = ((k >> b) & 1) == 1
                pr, pi = jnp.where(bit, pr * sr - pi * si, pr), jnp.where(bit, pr * si + pi * sr, pi)
            return pr, pi

        def ctab(kf, kb, with_f):
            pr, pi = powrow(kf, kb, with_f)
            pr, pi = rows16(pr.astype(BF16)), rows16(pi.astype(BF16))
            return cr * pr - ci * pi, cr * pi + ci * pr

        bbr_x, bbi_x = b_x
        ckr, cki = ctab(krow, (T - 1) - krow, True)
        zero = jnp.zeros_like(bbr_x)
        kf = bdot(ckr, jnp.where(fwd_row, bbr_x, zero)) - bdot(cki, jnp.where(fwd_row, bbi_x, zero))
        kb = bdot(ckr, jnp.where(fwd_row, zero, bbr_x)) - bdot(cki, jnp.where(fwd_row, zero, bbi_x))
        kt_scr[0:TH - H, :] = kb[0:TH - H]
        kt_scr[TH - H:TH, :] = kb[TH - H:TH] + kf[0:H] + diag_mask * d_ref[0, gi:gi + 1, :]
        kt_scr[TH:2 * TH - H, :] = kf[H:TH]
        for r0 in range(0, TH, LANES):
            for c0 in range(0, TH, LANES):
                blk = (lax.broadcasted_iota(jnp.int32, (LANES, LANES), 1) + c0) // H
                m = jnp.zeros((LANES, LANES), F32)
                for s in range(c0 // H, (c0 + LANES) // H):
                    w0 = (T - 1 - s) * H + r0
                    m = jnp.where(blk == s, kt_scr[w0:w0 + LANES, c0:c0 + LANES], m)
                w1t_ref[0, gi, r0:r0 + LANES, c0:c0 + LANES] = m.astype(BF16)

        pir, pii = powrow((T - 1) - krow, krow, True)
        pxr, pxi = expand(pir, e_blk), expand(pii, e_blk)
        w1t_ref[0, gi, TH:TH + 2 * P, :] = (pxr * bbr_x - pxi * bbi_x).astype(BF16)
        w1t_ref[0, gi, TH + 2 * P:2 * TH, :] = (pxr * bbi_x + pxi * bbr_x).astype(BF16)

        cor, coi = ctab(krow + 1, T - krow, False)
        w2_ref[0, gi, :, 0:LANES] = cor.astype(BF16)
        w2_ref[0, gi, :, LANES:2 * LANES] = (-coi).astype(BF16)

        acoef_ref[0, 0, gi:gi + 1, :] = squares[4][0]
        acoef_ref[0, 1, gi:gi + 1, :] = squares[4][1]


def _s5prep(a_re, a_im, log_dt, b_re, b_im, c_re, c_im, d_skip, w_in, w_glu, cond, w_ada, b_ada):
    depth = a_re.shape[0]
    nsteps = depth * (G // GP)
    wr = D // nsteps
    gr = DS // (G // GP)
    n3 = w_ada.shape[2]
    mc = n3 // (G // GP)
    step = lambda l, g: l * (G // GP) + g
    d_cols = jnp.tile(d_skip.reshape(depth, G, H), (1, 1, T))

    def dircat(v):
        return jnp.concatenate([v[:, 0], v[:, 1]], axis=-1)

    ld = jnp.broadcast_to(log_dt[..., None, None], log_dt.shape + (1, P))
    pad = jnp.zeros((depth, G, 5, 2 * P), F32)
    prm = jnp.concatenate([dircat(a_re[:, :, :, None, :]), dircat(a_im[:, :, :, None, :]), dircat(ld), pad,
                           dircat(c_re), dircat(c_im)], axis=2)
    b_spec = pl.BlockSpec((1, 2, GP, P, H), lambda l, g: (l, 0, g, 0, 0))
    return pl.pallas_call(
        _s5prep_kernel,
        grid=(depth, G // GP),
        in_specs=[pl.BlockSpec((1, GP, PRM_ROWS, 2 * P), lambda l, g: (l, g, 0, 0)), b_spec, b_spec,
                  pl.BlockSpec((1, GP, TH), lambda l, g: (l, g, 0)),
                  pl.BlockSpec((None, wr, 6 * DS), lambda l, g: (0, step(l, g), 0)),
                  pl.BlockSpec((None, gr, DS), lambda l, g: (l, g, 0)),
                  pl.BlockSpec((COND_ROWS, D), lambda l, g: (0, 0)),
                  pl.BlockSpec((None, D, mc), lambda l, g: (l, 0, g)),
                  pl.BlockSpec((None, 1, mc), lambda l, g: (l, 0, g))],
        out_specs=[pl.BlockSpec((1, GP, 2 * TH, TH), lambda l, g: (l, g, 0, 0)),
                   pl.BlockSpec((1, GP, TH, TH), lambda l, g: (l, g, 0, 0)),
                   pl.BlockSpec((1, 2, GP, 2 * P), lambda l, g: (l, 0, g, 0)),
                   pl.BlockSpec((None, wr, 4 * DS), lambda l, g: (0, step(l, g), 0)),
                   pl.BlockSpec((None, 2 * DS, wr), lambda l, g: (0, 0, step(l, g))),
                   pl.BlockSpec((None, DS, gr), lambda l, g: (l, 0, g)),
                   pl.BlockSpec((None, 2, mc), lambda l, g: (l, 0, g))],
        out_shape=[jax.ShapeDtypeStruct((depth, G, 2 * TH, TH), BF16),
                   jax.ShapeDtypeStruct((depth, G, TH, TH), BF16),
                   jax.ShapeDtypeStruct((depth, 2, G, 2 * P), F32),
                   jax.ShapeDtypeStruct((1, D, 4 * DS), BF16),
                   jax.ShapeDtypeStruct((1, 2 * DS, D), BF16),
                   jax.ShapeDtypeStruct((depth, DS, DS), BF16),
                   jax.ShapeDtypeStruct((depth, 2, n3), F32)],
        scratch_shapes=[pltpu.VMEM((2 * TH, TH), F32), pltpu.VMEM((COND_ROWS, D), BF16)],
        compiler_params=_cparams(("arbitrary", "arbitrary")),
        name="prep",
    )(prm, b_re, b_im, d_cols, w_in, w_glu, cond, w_ada, b_ada.reshape(depth, 1, n3))


def _plane_copies(hbm3, buf, sem, step, slot, *, cb, tb, ntt, to_hbm):
    b, tt = step // ntt, step % ntt
    copies = []
    for t in range(tb):
        hbm = hbm3.at[pl.ds(b * cb, cb), tt * tb + t, :]
        vm = buf.at[slot, t]
        copies.append(pltpu.make_async_copy(vm, hbm, sem.at[slot, t]) if to_hbm
                      else pltpu.make_async_copy(hbm, vm, sem.at[slot, t]))
    return copies


def _fetch_planes(x_hbm, xbuf, xsem, step, nsteps, **kw):
    slot = step % 2
    mk = functools.partial(_plane_copies, x_hbm, xbuf, xsem, to_hbm=False, **kw)

    @pl.when(step == 0)
    def _():
        for cp in mk(step, slot):
            cp.start()

    @pl.when(step + 1 < nsteps)
    def _():
        for cp in mk(step + 1, 1 - slot):
            cp.start()

    for cp in mk(step, slot):
        cp.wait()
    return slot


def _layernorm_rows(v, eps=LN_EPS):
    mu = jnp.mean(v, axis=-1, keepdims=True)
    vc = v - mu
    var = jnp.mean(vc * vc, axis=-1, keepdims=True)
    return vc * lax.rsqrt(var + eps)


def _inproj_kernel(*refs, cb, tb, nb, ntt, u_only, cast_next):
    if u_only:
        x_hbm, mod_ref, wut_ref, ut_ref, xbuf, xsem, h_scr = refs
    elif cast_next:
        (x_hbm, mod_ref, wut_ref, wzt_ref, wv_ref, wb_ref, wc_ref, wzc_ref, wnext_ref,
         ut_ref, szt_ref, s_ref, bg_ref, szc_ref, wrow_next_ref, wt_next_ref, xbuf, xsem, h_scr) = refs
        wn = wnext_ref[...]
        wt_next_ref[...] = wn[:, 0:2 * DS].T.astype(BF16)
        wrow_next_ref[...] = wn[:, 2 * DS:].astype(BF16)
    else:
        (x_hbm, mod_ref, wut_ref, wzt_ref, wv_ref, wb_ref, wc_ref, wzc_ref,
         ut_ref, szt_ref, s_ref, bg_ref, szc_ref, xbuf, xsem, h_scr) = refs
    step = pl.program_id(0) * ntt + pl.program_id(1)
    slot = _fetch_planes(x_hbm, xbuf, xsem, step, nb * ntt, cb=cb, tb=tb, ntt=ntt)
    shift, scale = mod_ref[0:1, :], mod_ref[1:2, :]
    for t in range(tb):
        hn = _layernorm_rows(xbuf[slot, t]) * (1.0 + scale) + shift
        h_scr[t * cb:(t + 1) * cb, :] = hn.astype(BF16)
    h = h_scr[...]
    ut_ref[...] = lax.dot_general(wut_ref[...], h, NT_DIMS, preferred_element_type=F32).astype(BF16)
    if u_only:
        return
    szt_ref[...] = _silu(lax.dot_general(wzt_ref[...], h, NT_DIMS, preferred_element_type=F32)).astype(BF16)
    v = jnp.dot(h, wv_ref[...], preferred_element_type=F32)
    cg = jnp.dot(h, wc_ref[...], preferred_element_type=F32)
    s_ref[...] = (cg * v).astype(BF16)
    bg_ref[...] = jnp.dot(h, wb_ref[...], preferred_element_type=F32).astype(BF16)
    szc_ref[...] = _silu(jnp.dot(h, wzc_ref[...], preferred_element_type=F32)).astype(BF16)


def _inproj(x3, mod, wt, wrow, mod_idx, cb, tb, u_only=False, w_next=None, next_layer=0):
    nc = x3.shape[0]
    n = nc * T
    nb, ntt = nc // cb, T // tb
    r = cb * tb
    col = lambda b, tt: (0, b * ntt + tt)
    row = lambda b, tt: (b * ntt + tt, 0)
    in_specs = [pl.BlockSpec(memory_space=pl.ANY), _layer_spec((3, D), mod_idx, 0, 0),
                _layer_spec((DS, D), 0, 0, 0)]
    out_specs = [pl.BlockSpec((DS, r), col)]
    out_shape = [jax.ShapeDtypeStruct((DS, n), BF16)]
    args = [x3, mod, wt]
    if not u_only:
        in_specs += [_layer_spec((DS, D), 0, 1, 0)] + [_layer_spec((D, DS), 0, 0, k) for k in range(4)]
        out_specs += [pl.BlockSpec((DS, r), col)] + [pl.BlockSpec((r, DS), row)] * 3
        out_shape += [jax.ShapeDtypeStruct((DS, n), BF16)] + [jax.ShapeDtypeStruct((n, DS), BF16)] * 3
        args += [wt, wrow, wrow, wrow, wrow]
    if w_next is not None:
        wr = D // (nb * ntt)
        in_specs += [pl.BlockSpec((None, wr, 6 * DS), lambda b, tt: (next_layer, b * ntt + tt, 0))]
        out_specs += [pl.BlockSpec((None, wr, 4 * DS), lambda b, tt: (0, b * ntt + tt, 0)),
                      pl.BlockSpec((None, 2 * DS, wr), lambda b, tt: (0, 0, b * ntt + tt))]
        out_shape += [jax.ShapeDtypeStruct((1, D, 4 * DS), BF16), jax.ShapeDtypeStruct((1, 2 * DS, D), BF16)]
        args += [w_next]
    return pl.pallas_call(
        functools.partial(_inproj_kernel, cb=cb, tb=tb, nb=nb, ntt=ntt, u_only=u_only,
                          cast_next=w_next is not None),
        grid=(nb, ntt),
        in_specs=in_specs,
        out_specs=out_specs,
        out_shape=out_shape,
        scratch_shapes=[pltpu.VMEM((2, tb, cb, D), F32), pltpu.SemaphoreType.DMA((2, tb)),
                        pltpu.VMEM((r, D), BF16)],
        compiler_params=_cparams(("arbitrary", "arbitrary")),
        name="inproj_u" if u_only else "inproj",
    )(*args)


GB = 8
NCL = 512
NCC = 16
NCT = NCL + NCC
NCP = 640
CBL = 128


def _s5_kernel(ut_ref, utc_ref, w1t_ref, w2_ref, acoef_ref, wo_ref, yt_ref, ytc_ref, wo_bf_ref,
               sre_scr, sim_scr, hre_scr, him_scr, yloc_scr, *, ctx_out):
    nbl = NCL // CBL
    wo_bf_ref[...] = wo_ref[...].astype(BF16)
    for gl in range(GB):
        rows = slice(gl * H, (gl + 1) * H)
        planes = []
        for t in range(T):
            lat = [ut_ref[rows, (b * T + t) * CBL:(b * T + t + 1) * CBL] for b in range(nbl)]
            planes.append(jnp.concatenate(lat + [utc_ref[t, rows, :]], axis=1))
        z = jnp.concatenate(planes, axis=0)
        r1 = jnp.dot(w1t_ref[gl], z, preferred_element_type=F32)
        yloc_scr[gl] = r1[0:TH]
        st = r1[TH:2 * TH].T
        sre_scr[pl.ds(gl, NCP, stride=GB), :] = st[:, 0:LANES]
        sim_scr[pl.ds(gl, NCP, stride=GB), :] = st[:, LANES:2 * LANES]

    ar, ai = acoef_ref[0], acoef_ref[1]
    fwd_lane = lax.broadcasted_iota(jnp.int32, (GB, LANES), 1) < P
    bwd_lane = jnp.logical_not(fwd_lane)
    hre_scr[NCT * GB:NCP * GB, :] = jnp.zeros(((NCP - NCT) * GB, LANES), F32)
    him_scr[NCT * GB:NCP * GB, :] = jnp.zeros(((NCP - NCT) * GB, LANES), F32)

    def step(k, carry, fwd_shift):
        hr, hi = carry
        cf = k + fwd_shift
        cb_ = NCT - 1 - k
        rf = pl.ds(pl.multiple_of(cf * GB, GB), GB)
        rb = pl.ds(pl.multiple_of(cb_ * GB, GB), GB)
        pltpu.store(hre_scr.at[rf, :], hr, mask=fwd_lane)
        pltpu.store(hre_scr.at[rb, :], hr, mask=bwd_lane)
        pltpu.store(him_scr.at[rf, :], hi, mask=fwd_lane)
        pltpu.store(him_scr.at[rb, :], hi, mask=bwd_lane)
        sr = jnp.where(fwd_lane, sre_scr[rf, :], sre_scr[rb, :])
        si = jnp.where(fwd_lane, sim_scr[rf, :], sim_scr[rb, :])
        return ar * hr - ai * hi + sr, ar * hi + ai * hr + si

    zero = jnp.zeros((GB, LANES), F32)
    carry = lax.fori_loop(0, NCC, functools.partial(step, fwd_shift=NCL), (zero, zero))
    lax.fori_loop(NCC, NCT, functools.partial(step, fwd_shift=-NCC), carry, unroll=4)

    for gl in range(GB):
        rows = slice(gl * H, (gl + 1) * H)
        sel = pl.ds(gl, NCP, stride=GB)
        hp = jnp.concatenate([hre_scr[sel, :], him_scr[sel, :]], axis=1)
        y = yloc_scr[gl] + lax.dot_general(w2_ref[gl], hp.astype(BF16), NT_DIMS, preferred_element_type=F32)
        yb = y.astype(BF16)
        for t in range(T):
            for b in range(nbl):
                yt_ref[rows, (b * T + t) * CBL:(b * T + t + 1) * CBL] = \
                    yb[t * H:(t + 1) * H, b * CBL:(b + 1) * CBL]
            if ctx_out:
                ytc_ref[t, rows, :] = yb[t * H:(t + 1) * H, NCL:NCP]
    if not ctx_out:
        ytc_ref[...] = jnp.zeros(ytc_ref.shape, BF16)


def _s5(ut, utc, w1t, w2, acoef, w_out, layer, ctx_out):
    n = ut.shape[1]
    ncx = NCP - NCL
    wr = 2 * DS // (G // GB)
    state = pltpu.VMEM((NCP * GB, LANES), F32)
    return pl.pallas_call(
        functools.partial(_s5_kernel, ctx_out=ctx_out),
        grid=(G // GB,),
        in_specs=[pl.BlockSpec((GB * H, n), lambda i: (i, 0)),
                  pl.BlockSpec((T, GB * H, ncx), lambda i: (0, i, 0)),
                  pl.BlockSpec((None, GB, 2 * TH, TH), lambda i: (layer, i, 0, 0)),
                  pl.BlockSpec((None, GB, TH, TH), lambda i: (layer, i, 0, 0)),
                  pl.BlockSpec((None, 2, GB, LANES), lambda i: (layer, 0, i, 0)),
                  pl.BlockSpec((None, wr, D), lambda i: (layer, i, 0))],
        out_specs=[pl.BlockSpec((GB * H, n), lambda i: (i, 0)),
                   pl.BlockSpec((T, GB * H, ncx), lambda i: (0, i, 0)),
                   pl.BlockSpec((None, wr, D), lambda i: (0, i, 0))],
        out_shape=[jax.ShapeDtypeStruct((DS, n), BF16), jax.ShapeDtypeStruct((T, DS, ncx), BF16),
                   jax.ShapeDtypeStruct((1, 2 * DS, D), BF16)],
        scratch_shapes=[state, state, state, state, pltpu.VMEM((GB, TH, NCP), F32)],
        compiler_params=_cparams(("parallel",)),
        name="s5",
    )(ut, utc, w1t, w2, acoef, w_out)


def _outproj_kernel(x_hbm, mod_ref, yt_ref, szt_ref, s_ref, sprev_ref, snext_ref, bg_ref, szc_ref,
                    wgt_ref, bglu_ref, cw_ref, cb_ref, wo_ref, lng_ref, lnb_ref,
                    o_hbm, xbuf, xsem, obuf, osem, *, cb, tb, nb, ntt, row_chunks, alpha):
    tt = pl.program_id(1)
    step = pl.program_id(0) * ntt + tt
    nsteps = nb * ntt
    slot = _fetch_planes(x_hbm, xbuf, xsem, step, nsteps, cb=cb, tb=tb, ntt=ntt)
    g = jax.nn.gelu(yt_ref[...].astype(F32))
    gate = jnp.dot(wgt_ref[...], g.astype(BF16), preferred_element_type=F32) + bglu_ref[...]
    o5 = (g * jax.nn.sigmoid(gate) * szt_ref[...].astype(F32)).T.astype(BF16)
    crow = lax.broadcasted_iota(jnp.int32, (cb, 1), 0) % row_chunks
    sp = sprev_ref[...].astype(F32)
    sn = snext_ref[...].astype(F32)
    sp_wrap = jnp.where(crow != 0, pltpu.roll(sp, 1, 0), 0.0)
    sn_wrap = jnp.where(crow != row_chunks - 1, pltpu.roll(sn, cb - 1, 0), 0.0)
    sp = jnp.where(tt == 0, sp_wrap, sp).astype(BF16)
    sn = jnp.where(tt == ntt - 1, sn_wrap, sn).astype(BF16)
    s_all = jnp.concatenate([sp, s_ref[...], sn], axis=0)
    r = cb * tb
    cw = cw_ref[...].astype(BF16)
    conv = (cw[0:1, :] * s_all[0:r] + cw[1:2, :] * s_all[cb:cb + r]
            + cw[2:3, :] * s_all[2 * cb:2 * cb + r] + cb_ref[...].astype(BF16))
    ocv = bg_ref[...] * conv * szc_ref[...]
    sub = jnp.dot(jnp.concatenate([o5, ocv], axis=1), wo_ref[...], preferred_element_type=F32)
    gate_vec = mod_ref[2:3, :] * (1.0 / alpha)
    for t in range(tb):
        v = xbuf[slot, t] + gate_vec * sub[t * cb:(t + 1) * cb]
        obuf[slot, t] = _layernorm_rows(v, LN_EPS / (alpha * alpha)) * lng_ref[...] + lnb_ref[...]
    put = functools.partial(_plane_copies, o_hbm, obuf, osem, cb=cb, tb=tb, ntt=ntt, to_hbm=True)

    @pl.when(step > 0)
    def _():
        for cp in put(step - 1, 1 - slot):
            cp.wait()

    for cp in put(step, slot):
        cp.start()

    @pl.when(step == nsteps - 1)
    def _():
        for cp in put(step, slot):
            cp.wait()


def _outproj(x3, mod, yt, szt, s, bg, szc, wo, wts, layer, mod_idx, cb, tb, row_chunks, alpha):
    nc = x3.shape[0]
    nb, ntt = nc // cb, T // tb
    r = cb * tb
    col = lambda b, tt: (0, b * ntt + tt)
    row = lambda b, tt: (b * ntt + tt, 0)
    prev_plane = lambda b, tt: (b * T + (tt * tb + T - 1) % T, 0)
    next_plane = lambda b, tt: (b * T + (tt * tb + tb) % T, 0)
    wgt, bglu, cw, cbias, lng, lnb = wts
    planes = pltpu.VMEM((2, tb, cb, D), F32)
    return pl.pallas_call(
        functools.partial(_outproj_kernel, cb=cb, tb=tb, nb=nb, ntt=ntt, row_chunks=row_chunks, alpha=alpha),
        grid=(nb, ntt),
        in_specs=[pl.BlockSpec(memory_space=pl.ANY),
                  _layer_spec((3, D), mod_idx, 0, 0),
                  pl.BlockSpec((DS, r), col), pl.BlockSpec((DS, r), col),
                  pl.BlockSpec((r, DS), row),
                  pl.BlockSpec((cb, DS), prev_plane), pl.BlockSpec((cb, DS), next_plane),
                  pl.BlockSpec((r, DS), row), pl.BlockSpec((r, DS), row),
                  _layer_spec((DS, DS), layer, 0, 0), _layer_spec((DS, 1), layer, 0, 0),
                  _layer_spec((3, DS), layer, 0, 0), _layer_spec((1, DS), layer, 0, 0),
                  _layer_spec((2 * DS, D), 0, 0, 0),
                  _layer_spec((1, D), layer, 0, 0), _layer_spec((1, D), layer, 0, 0)],
        out_specs=pl.BlockSpec(memory_space=pl.ANY),
        out_shape=jax.ShapeDtypeStruct(x3.shape, F32),
        scratch_shapes=[planes, pltpu.SemaphoreType.DMA((2, tb)), planes, pltpu.SemaphoreType.DMA((2, tb))],
        compiler_params=_cparams(("arbitrary", "arbitrary")),
        name="outproj",
    )(x3, mod, yt, szt, s, s, s, bg, szc, wgt, bglu, cw, cbias, wo, lng, lnb)


def kernel(x, c, ctx, c_ctx, w_ada, b_ada, w_in, s5_a_re, s5_a_im, s5_log_dt, s5_b_re, s5_b_im,
           s5_c_re, s5_c_im, s5_d, w_glu, b_glu, conv_w, conv_b, w_out, ln_g, ln_b):
    bsz, n, d = x.shape
    n_ctx = ctx.shape[1]
    depth = w_in.shape[0]
    assert bsz == 1 and d == D and n == NCL * T and n_ctx == NCC * T
    alpha = (2.0 * depth) ** 0.25

    cond = jnp.pad(jnp.stack([c[0], c_ctx]), ((0, COND_ROWS - 2), (0, 0)))
    w1t, w2, acoef, w_row, w_t, w_glu_t, mod = _s5prep(s5_a_re, s5_a_im, s5_log_dt, s5_b_re, s5_b_im,
                                                       s5_c_re, s5_c_im, s5_d, w_in, w_glu, cond, w_ada, b_ada)
    mod = mod.reshape(depth * 2, 3, D)
    out_w = (w_glu_t, b_glu.reshape(depth, DS, 1), conv_w, conv_b.reshape(depth, 1, DS),
             ln_g.reshape(depth, 1, D), ln_b.reshape(depth, 1, D))

    x3 = x.reshape(n // T, T, D)
    c3 = ctx.reshape(n_ctx // T, T, D)
    tb = 4
    for i in range(depth):
        last = i == depth - 1
        ut, szt, s, bg, szc, *w_next = _inproj(x3, mod, w_t, w_row, 2 * i, CBL, tb,
                                               w_next=None if last else w_in, next_layer=i + 1)
        utc2, *ctx_rest = _inproj(c3, mod, w_t, w_row, 2 * i + 1, NCC, T, u_only=last)
        utc = jnp.pad(utc2.reshape(DS, T, NCC).transpose(1, 0, 2),
                      ((0, 0), (0, 0), (0, NCP - NCT)))
        yt, ytc, wo = _s5(ut, utc, w1t, w2, acoef, w_out, i, ctx_out=not last)
        x3 = _outproj(x3, mod, yt, szt, s, bg, szc, wo, out_w, i, 2 * i, CBL, tb, GRID_W // T, alpha)
        if not last:
            ytc2 = ytc[:, :, 0:NCC].transpose(1, 0, 2).reshape(DS, n_ctx)
            c3 = _outproj(c3, mod, ytc2, *ctx_rest, wo, out_w, i, 2 * i + 1, NCC, T, NCC, alpha)
            w_row, w_t = w_next
    return x3.reshape(bsz, n, D)
```

```python
import functools

import jax
import jax.numpy as jnp
from jax import lax
from jax.experimental import pallas as pl
from jax.experimental.pallas import tpu as pltpu

D = 2048
DS = 1024
G = 64
H = 16
P = 64
T = 16
TH = T * H
GRID_W = 64
LN_EPS = 1e-6
LANES = 128
COND_ROWS = 16
VMEM_LIMIT = 60 * 1024 * 1024

F32 = jnp.float32
BF16 = jnp.bfloat16
NT_DIMS = (((1,), (1,)), ((), ()))
TN_DIMS = (((0,), (0,)), ((), ()))


def _cparams(sem):
    return pltpu.CompilerParams(dimension_semantics=sem, vmem_limit_bytes=VMEM_LIMIT)


def _layer_spec(shape, *idx):
    return pl.BlockSpec((None,) + shape, lambda *_: idx, pipeline_mode=pl.Buffered(1))


def _silu(v):
    return v * jax.nn.sigmoid(v)


def _adaln_block(cond_ref, w_ref, b_ref, o_ref, s_scr, first):
    @pl.when(first)
    def _():
        s_scr[...] = _silu(cond_ref[...]).astype(BF16)

    mod = jnp.dot(s_scr[...], w_ref[...].astype(BF16), preferred_element_type=F32)
    o_ref[...] = mod[0:2, :] + b_ref[...]


GP = 8
PRM_ROWS = 40


def _s5prep_kernel(prm_ref, bre_ref, bim_ref, d_ref, win_ref, wglu_ref, cond_ref, wada_ref, bada_ref,
                   w1t_ref, w2_ref, acoef_ref, wrow_ref, wt_ref, wgt_ref, mod_ref, kt_scr, s_scr):
    @pl.when(pl.program_id(0) == 0)
    def _():
        _adaln_block(cond_ref, wada_ref, bada_ref, mod_ref, s_scr, pl.program_id(1) == 0)
    wn = win_ref[...]
    wt_ref[...] = wn[:, 0:2 * DS].T.astype(BF16)
    wrow_ref[...] = wn[:, 2 * DS:].astype(BF16)
    wgt_ref[...] = wglu_ref[...].T.astype(BF16)
    lane = lax.broadcasted_iota(jnp.int32, (1, LANES), 1)
    fwd_lane = lane < P
    krow = lax.broadcasted_iota(jnp.int32, (T, 1), 0)
    fwd_row = lax.broadcasted_iota(jnp.int32, (2 * P, 1), 0) < P
    col256 = lax.broadcasted_iota(jnp.int32, (T, TH), 1)
    r16 = lax.broadcasted_iota(jnp.int32, (T, TH), 0)
    diag_mask = (col256 % H == r16).astype(F32)
    e_rep = diag_mask.astype(BF16)
    e_blk = (col256 // H == r16).astype(BF16)

    def rows16(tab):
        return jnp.broadcast_to(tab[:, None, :], (T, H, tab.shape[1])).reshape(TH, tab.shape[1])

    def tile16(mat):
        return jnp.broadcast_to(mat[None], (T, H, mat.shape[1])).reshape(TH, mat.shape[1])

    def bdot(a, b):
        return jnp.dot(a.astype(BF16), b.astype(BF16), preferred_element_type=F32)

    def expand(tab, e):
        return lax.dot_general(tab.astype(BF16), e, TN_DIMS, preferred_element_type=F32)

    for gi in range(GP):
        ar, ai = prm_ref[0, gi, 0:1, :], prm_ref[0, gi, 1:2, :]
        dt = jnp.exp(prm_ref[0, gi, 2:3, :])
        la, th = ar * dt, ai * dt
        cr = tile16(prm_ref[0, gi, 8:24, :].astype(BF16))
        ci = tile16(prm_ref[0, gi, 24:40, :].astype(BF16))
        b_x = [bdot(jnp.concatenate([b_ref[0, 0, gi], b_ref[0, 1, gi]], axis=0), e_rep)
               for b_ref in (bre_ref, bim_ref)]

        mag1 = jnp.exp(la)
        a1r, a1i = mag1 * jnp.cos(th), mag1 * jnp.sin(th)
        squares = [(a1r, a1i)]
        for _ in range(4):
            sr, si = squares[-1]
            squares.append((sr * sr - si * si, 2.0 * sr * si))

        zr, zi = a1r - 1.0, a1i
        den = ar * ar + ai * ai
        fr, fi = (zr * ar + zi * ai) / den, (zi * ar - zr * ai) / den

        def powrow(kf, kb, with_f):
            k = jnp.where(fwd_lane, kf, kb)
            if with_f:
                pr, pi = jnp.broadcast_to(fr, k.shape), jnp.broadcast_to(fi, k.shape)
            else:
                pr, pi = jnp.ones(k.shape, F32), jnp.zeros(k.shape, F32)
            for b, (sr, si) in enumerate(squares):
                bit = ((k >> b) & 1) == 1
                pr, pi = jnp.where(bit, pr * sr - pi * si, pr), jnp.where(bit, pr * si + pi * sr, pi)
            return pr, pi

        def ctab(kf, kb, with_f):
            pr, pi = powrow(kf, kb, with_f)
            pr, pi = rows16(pr.astype(BF16)), rows16(pi.astype(BF16))
            return cr * pr - ci * pi, cr * pi + ci * pr

        bbr_x, bbi_x = b_x
        ckr, cki = ctab(krow, (T - 1) - krow, True)
        zero = jnp.zeros_like(bbr_x)
        kf = bdot(ckr, jnp.where(fwd_row, bbr_x, zero)) - bdot(cki, jnp.where(fwd_row, bbi_x, zero))
        kb = bdot(ckr, jnp.where(fwd_row, zero, bbr_x)) - bdot(cki, jnp.where(fwd_row, zero, bbi_x))
        kt_scr[0:TH - H, :] = kb[0:TH - H]
        kt_scr[TH - H:TH, :] = kb[TH - H:TH] + kf[0:H] + diag_mask * d_ref[0, gi:gi + 1, :]
        kt_scr[TH:2 * TH - H, :] = kf[H:TH]
        for r0 in range(0, TH, LANES):
            for c0 in range(0, TH, LANES):
                blk = (lax.broadcasted_iota(jnp.int32, (LANES, LANES), 1) + c0) // H
                m = jnp.zeros((LANES, LANES), F32)
                for s in range(c0 // H, (c0 + LANES) // H):
                    w0 = (T - 1 - s) * H + r0
                    m = jnp.where(blk == s, kt_scr[w0:w0 + LANES, c0:c0 + LANES], m)
                w1t_ref[0, gi, r0:r0 + LANES, c0:c0 + LANES] = m.astype(BF16)

        pir, pii = powrow((T - 1) - krow, krow, True)
        pxr, pxi = expand(pir, e_blk), expand(pii, e_blk)
        w1t_ref[0, gi, TH:TH + 2 * P, :] = (pxr * bbr_x - pxi * bbi_x).astype(BF16)
        w1t_ref[0, gi, TH + 2 * P:2 * TH, :] = (pxr * bbi_x + pxi * bbr_x).astype(BF16)

        cor, coi = ctab(krow + 1, T - krow, False)
        w2_ref[0, gi, :, 0:LANES] = cor.astype(BF16)
        w2_ref[0, gi, :, LANES:2 * LANES] = (-coi).astype(BF16)

        acoef_ref[0, 0, gi:gi + 1, :] = squares[4][0]
        acoef_ref[0, 1, gi:gi + 1, :] = squares[4][1]


def _s5prep(a_re, a_im, log_dt, b_re, b_im, c_re, c_im, d_skip, w_in, w_glu, cond, w_ada, b_ada):
    depth = a_re.shape[0]
    nsteps = depth * (G // GP)
    wr = D // nsteps
    gr = DS // (G // GP)
    n3 = w_ada.shape[2]
    mc = n3 // (G // GP)
    ada_blk = lambda l, g: jnp.where(l == 0, g, G // GP - 1)
    step = lambda l, g: l * (G // GP) + g
    d_cols = jnp.tile(d_skip.reshape(depth, G, H), (1, 1, T))

    def dircat(v):
        return jnp.concatenate([v[:, 0], v[:, 1]], axis=-1)

    ld = jnp.broadcast_to(log_dt[..., None, None], log_dt.shape + (1, P))
    pad = jnp.zeros((depth, G, 5, 2 * P), F32)
    prm = jnp.concatenate([dircat(a_re[:, :, :, None, :]), dircat(a_im[:, :, :, None, :]), dircat(ld), pad,
                           dircat(c_re), dircat(c_im)], axis=2)
    b_spec = pl.BlockSpec((1, 2, GP, P, H), lambda l, g: (l, 0, g, 0, 0))
    return pl.pallas_call(
        _s5prep_kernel,
        grid=(depth, G // GP),
        in_specs=[pl.BlockSpec((1, GP, PRM_ROWS, 2 * P), lambda l, g: (l, g, 0, 0)), b_spec, b_spec,
                  pl.BlockSpec((1, GP, TH), lambda l, g: (l, g, 0)),
                  pl.BlockSpec((None, wr, 6 * DS), lambda l, g: (0, step(l, g), 0)),
                  pl.BlockSpec((None, gr, DS), lambda l, g: (l, g, 0)),
                  pl.BlockSpec((COND_ROWS, D), lambda l, g: (0, 0)),
                  pl.BlockSpec((None, D, mc), lambda l, g: (0, 0, ada_blk(l, g))),
                  pl.BlockSpec((None, 1, mc), lambda l, g: (0, 0, ada_blk(l, g)))],
        out_specs=[pl.BlockSpec((1, GP, 2 * TH, TH), lambda l, g: (l, g, 0, 0)),
                   pl.BlockSpec((1, GP, TH, TH), lambda l, g: (l, g, 0, 0)),
                   pl.BlockSpec((1, 2, GP, 2 * P), lambda l, g: (l, 0, g, 0)),
                   pl.BlockSpec((None, wr, 4 * DS), lambda l, g: (0, step(l, g), 0)),
                   pl.BlockSpec((None, 2 * DS, wr), lambda l, g: (0, 0, step(l, g))),
                   pl.BlockSpec((None, DS, gr), lambda l, g: (l, 0, g)),
                   pl.BlockSpec((None, 2, mc), lambda l, g: (0, 0, ada_blk(l, g)))],
        out_shape=[jax.ShapeDtypeStruct((depth, G, 2 * TH, TH), BF16),
                   jax.ShapeDtypeStruct((depth, G, TH, TH), BF16),
                   jax.ShapeDtypeStruct((depth, 2, G, 2 * P), F32),
                   jax.ShapeDtypeStruct((1, D, 4 * DS), BF16),
                   jax.ShapeDtypeStruct((1, 2 * DS, D), BF16),
                   jax.ShapeDtypeStruct((depth, DS, DS), BF16),
                   jax.ShapeDtypeStruct((1, 2, n3), F32)],
        scratch_shapes=[pltpu.VMEM((2 * TH, TH), F32), pltpu.VMEM((COND_ROWS, D), BF16)],
        compiler_params=_cparams(("arbitrary", "arbitrary")),
        name="prep",
    )(prm, b_re, b_im, d_cols, w_in, w_glu, cond, w_ada, b_ada.reshape(depth, 1, n3))


def _plane_copies(hbm3, buf, sem, step, slot, *, cb, tb, ntt, to_hbm):
    b, tt = step // ntt, step % ntt
    copies = []
    for t in range(tb):
        hbm = hbm3.at[pl.ds(b * cb, cb), tt * tb + t, :]
        vm = buf.at[slot, t]
        copies.append(pltpu.make_async_copy(vm, hbm, sem.at[slot, t]) if to_hbm
                      else pltpu.make_async_copy(hbm, vm, sem.at[slot, t]))
    return copies


def _fetch_planes(x_hbm, xbuf, xsem, step, nsteps, **kw):
    slot = step % 2
    mk = functools.partial(_plane_copies, x_hbm, xbuf, xsem, to_hbm=False, **kw)

    @pl.when(step == 0)
    def _():
        for cp in mk(step, slot):
            cp.start()

    @pl.when(step + 1 < nsteps)
    def _():
        for cp in mk(step + 1, 1 - slot):
            cp.start()

    for cp in mk(step, slot):
        cp.wait()
    return slot


def _layernorm_rows(v, eps=LN_EPS):
    mu = jnp.mean(v, axis=-1, keepdims=True)
    vc = v - mu
    var = jnp.mean(vc * vc, axis=-1, keepdims=True)
    return vc * lax.rsqrt(var + eps)


def _inproj_kernel(*refs, cb, tb, nb, ntt, u_only, cast_next):
    if u_only:
        x_hbm, mod_ref, wut_ref, ut_ref, xbuf, xsem, h_scr = refs
    elif cast_next:
        (x_hbm, mod_ref, wut_ref, wzt_ref, wv_ref, wb_ref, wc_ref, wzc_ref, wnext_ref,
         ut_ref, szt_ref, s_ref, bg_ref, szc_ref, wrow_next_ref, wt_next_ref, xbuf, xsem, h_scr) = refs
        wn = wnext_ref[...]
        wt_next_ref[...] = wn[:, 0:2 * DS].T.astype(BF16)
        wrow_next_ref[...] = wn[:, 2 * DS:].astype(BF16)
    else:
        (x_hbm, mod_ref, wut_ref, wzt_ref, wv_ref, wb_ref, wc_ref, wzc_ref,
         ut_ref, szt_ref, s_ref, bg_ref, szc_ref, xbuf, xsem, h_scr) = refs
    step = pl.program_id(0) * ntt + pl.program_id(1)
    slot = _fetch_planes(x_hbm, xbuf, xsem, step, nb * ntt, cb=cb, tb=tb, ntt=ntt)
    shift, scale = mod_ref[0:1, :], mod_ref[1:2, :]
    for t in range(tb):
        hn = _layernorm_rows(xbuf[slot, t]) * (1.0 + scale) + shift
        h_scr[t * cb:(t + 1) * cb, :] = hn.astype(BF16)
    h = h_scr[...]
    ut_ref[...] = lax.dot_general(wut_ref[...], h, NT_DIMS, preferred_element_type=F32).astype(BF16)
    if u_only:
        return
    szt_ref[...] = _silu(lax.dot_general(wzt_ref[...], h, NT_DIMS, preferred_element_type=F32)).astype(BF16)
    v = jnp.dot(h, wv_ref[...], preferred_element_type=F32)
    cg = jnp.dot(h, wc_ref[...], preferred_element_type=F32)
    s_ref[...] = (cg * v).astype(BF16)
    bg_ref[...] = jnp.dot(h, wb_ref[...], preferred_element_type=F32).astype(BF16)
    szc_ref[...] = _silu(jnp.dot(h, wzc_ref[...], preferred_element_type=F32)).astype(BF16)


def _inproj(x3, mod, wt, wrow, mod_idx, cb, tb, u_only=False, w_next=None, next_layer=0):
    nc = x3.shape[0]
    n = nc * T
    nb, ntt = nc // cb, T // tb
    r = cb * tb
    col = lambda b, tt: (0, b * ntt + tt)
    row = lambda b, tt: (b * ntt + tt, 0)
    in_specs = [pl.BlockSpec(memory_space=pl.ANY), _layer_spec((3, D), mod_idx, 0, 0),
                _layer_spec((DS, D), 0, 0, 0)]
    out_specs = [pl.BlockSpec((DS, r), col)]
    out_shape = [jax.ShapeDtypeStruct((DS, n), BF16)]
    args = [x3, mod, wt]
    if not u_only:
        in_specs += [_layer_spec((DS, D), 0, 1, 0)] + [_layer_spec((D, DS), 0, 0, k) for k in range(4)]
        out_specs += [pl.BlockSpec((DS, r), col)] + [pl.BlockSpec((r, DS), row)] * 3
        out_shape += [jax.ShapeDtypeStruct((DS, n), BF16)] + [jax.ShapeDtypeStruct((n, DS), BF16)] * 3
        args += [wt, wrow, wrow, wrow, wrow]
    if w_next is not None:
        wr = D // (nb * ntt)
        in_specs += [pl.BlockSpec((None, wr, 6 * DS), lambda b, tt: (next_layer, b * ntt + tt, 0))]
        out_specs += [pl.BlockSpec((None, wr, 4 * DS), lambda b, tt: (0, b * ntt + tt, 0)),
                      pl.BlockSpec((None, 2 * DS, wr), lambda b, tt: (0, 0, b * ntt + tt))]
        out_shape += [jax.ShapeDtypeStruct((1, D, 4 * DS), BF16), jax.ShapeDtypeStruct((1, 2 * DS, D), BF16)]
        args += [w_next]
    return pl.pallas_call(
        functools.partial(_inproj_kernel, cb=cb, tb=tb, nb=nb, ntt=ntt, u_only=u_only,
                          cast_next=w_next is not None),
        grid=(nb, ntt),
        in_specs=in_specs,
        out_specs=out_specs,
        out_shape=out_shape,
        scratch_shapes=[pltpu.VMEM((2, tb, cb, D), F32), pltpu.SemaphoreType.DMA((2, tb)),
                        pltpu.VMEM((r, D), BF16)],
        compiler_params=_cparams(("arbitrary", "arbitrary")),
        name="inproj_u" if u_only else "inproj",
    )(*args)


GB = 8
NCL = 512
NCC = 16
NCT = NCL + NCC
NCP = 640
CBL = 128


def _s5_kernel(ut_ref, utc_ref, w1t_ref, w2_ref, acoef_ref, wo_ref, yt_ref, ytc_ref, wo_bf_ref,
               sre_scr, sim_scr, hre_scr, him_scr, yloc_scr, *, ctx_out):
    nbl = NCL // CBL
    wo_bf_ref[...] = wo_ref[...].astype(BF16)
    for gl in range(GB):
        rows = slice(gl * H, (gl + 1) * H)
        planes = []
        for t in range(T):
            lat = [ut_ref[rows, (b * T + t) * CBL:(b * T + t + 1) * CBL] for b in range(nbl)]
            planes.append(jnp.concatenate(lat + [utc_ref[t, rows, :]], axis=1))
        z = jnp.concatenate(planes, axis=0)
        r1 = jnp.dot(w1t_ref[gl], z, preferred_element_type=F32)
        yloc_scr[gl] = r1[0:TH]
        st = r1[TH:2 * TH].T
        sre_scr[pl.ds(gl, NCP, stride=GB), :] = st[:, 0:LANES]
        sim_scr[pl.ds(gl, NCP, stride=GB), :] = st[:, LANES:2 * LANES]

    ar, ai = acoef_ref[0], acoef_ref[1]
    fwd_lane = lax.broadcasted_iota(jnp.int32, (GB, LANES), 1) < P
    bwd_lane = jnp.logical_not(fwd_lane)
    hre_scr[NCT * GB:NCP * GB, :] = jnp.zeros(((NCP - NCT) * GB, LANES), F32)
    him_scr[NCT * GB:NCP * GB, :] = jnp.zeros(((NCP - NCT) * GB, LANES), F32)

    def step(k, carry, fwd_shift):
        hr, hi = carry
        cf = k + fwd_shift
        cb_ = NCT - 1 - k
        rf = pl.ds(pl.multiple_of(cf * GB, GB), GB)
        rb = pl.ds(pl.multiple_of(cb_ * GB, GB), GB)
        pltpu.store(hre_scr.at[rf, :], hr, mask=fwd_lane)
        pltpu.store(hre_scr.at[rb, :], hr, mask=bwd_lane)
        pltpu.store(him_scr.at[rf, :], hi, mask=fwd_lane)
        pltpu.store(him_scr.at[rb, :], hi, mask=bwd_lane)
        sr = jnp.where(fwd_lane, sre_scr[rf, :], sre_scr[rb, :])
        si = jnp.where(fwd_lane, sim_scr[rf, :], sim_scr[rb, :])
        return ar * hr - ai * hi + sr, ar * hi + ai * hr + si

    zero = jnp.zeros((GB, LANES), F32)
    carry = lax.fori_loop(0, NCC, functools.partial(step, fwd_shift=NCL), (zero, zero))
    lax.fori_loop(NCC, NCT, functools.partial(step, fwd_shift=-NCC), carry, unroll=4)

    for gl in range(GB):
        rows = slice(gl * H, (gl + 1) * H)
        sel = pl.ds(gl, NCP, stride=GB)
        hp = jnp.concatenate([hre_scr[sel, :], him_scr[sel, :]], axis=1)
        y = yloc_scr[gl] + lax.dot_general(w2_ref[gl], hp.astype(BF16), NT_DIMS, preferred_element_type=F32)
        yb = y.astype(BF16)
        for t in range(T):
            for b in range(nbl):
                yt_ref[rows, (b * T + t) * CBL:(b * T + t + 1) * CBL] = \
                    yb[t * H:(t + 1) * H, b * CBL:(b + 1) * CBL]
            if ctx_out:
                ytc_ref[t, rows, :] = yb[t * H:(t + 1) * H, NCL:NCP]
    if not ctx_out:
        ytc_ref[...] = jnp.zeros(ytc_ref.shape, BF16)


def _s5(ut, utc, w1t, w2, acoef, w_out, layer, ctx_out):
    n = ut.shape[1]
    ncx = NCP - NCL
    wr = 2 * DS // (G // GB)
    state = pltpu.VMEM((NCP * GB, LANES), F32)
    return pl.pallas_call(
        functools.partial(_s5_kernel, ctx_out=ctx_out),
        grid=(G // GB,),
        in_specs=[pl.BlockSpec((GB * H, n), lambda i: (i, 0)),
                  pl.BlockSpec((T, GB * H, ncx), lambda i: (0, i, 0)),
                  pl.BlockSpec((None, GB, 2 * TH, TH), lambda i: (layer, i, 0, 0)),
                  pl.BlockSpec((None, GB, TH, TH), lambda i: (layer, i, 0, 0)),
                  pl.BlockSpec((None, 2, GB, LANES), lambda i: (layer, 0, i, 0)),
                  pl.BlockSpec((None, wr, D), lambda i: (layer, i, 0))],
        out_specs=[pl.BlockSpec((GB * H, n), lambda i: (i, 0)),
                   pl.BlockSpec((T, GB * H, ncx), lambda i: (0, i, 0)),
                   pl.BlockSpec((None, wr, D), lambda i: (0, i, 0))],
        out_shape=[jax.ShapeDtypeStruct((DS, n), BF16), jax.ShapeDtypeStruct((T, DS, ncx), BF16),
                   jax.ShapeDtypeStruct((1, 2 * DS, D), BF16)],
        scratch_shapes=[state, state, state, state, pltpu.VMEM((GB, TH, NCP), F32)],
        compiler_params=_cparams(("parallel",)),
        name="s5",
    )(ut, utc, w1t, w2, acoef, w_out)


def _outproj_kernel(*refs, cb, tb, nb, ntt, row_chunks, alpha, ada_next):
    (x_hbm, mod_ref, yt_ref, szt_ref, s_ref, sprev_ref, snext_ref, bg_ref, szc_ref,
     wgt_ref, bglu_ref, cw_ref, cb_ref, wo_ref, lng_ref, lnb_ref) = refs[:16]
    if ada_next:
        cond_ref, wada_ref, bada_ref, o_hbm, modn_ref, xbuf, xsem, obuf, osem, s_scr = refs[16:]
    else:
        o_hbm, xbuf, xsem, obuf, osem = refs[16:]
    tt = pl.program_id(1)
    step = pl.program_id(0) * ntt + tt
    nsteps = nb * ntt
    if ada_next:
        _adaln_block(cond_ref, wada_ref, bada_ref, modn_ref, s_scr, step == 0)
    slot = _fetch_planes(x_hbm, xbuf, xsem, step, nsteps, cb=cb, tb=tb, ntt=ntt)
    g = jax.nn.gelu(yt_ref[...].astype(F32))
    gate = jnp.dot(wgt_ref[...], g.astype(BF16), preferred_element_type=F32) + bglu_ref[...]
    o5 = (g * jax.nn.sigmoid(gate) * szt_ref[...].astype(F32)).T.astype(BF16)
    crow = lax.broadcasted_iota(jnp.int32, (cb, 1), 0) % row_chunks
    sp = sprev_ref[...].astype(F32)
    sn = snext_ref[...].astype(F32)
    sp_wrap = jnp.where(crow != 0, pltpu.roll(sp, 1, 0), 0.0)
    sn_wrap = jnp.where(crow != row_chunks - 1, pltpu.roll(sn, cb - 1, 0), 0.0)
    sp = jnp.where(tt == 0, sp_wrap, sp).astype(BF16)
    sn = jnp.where(tt == ntt - 1, sn_wrap, sn).astype(BF16)
    s_all = jnp.concatenate([sp, s_ref[...], sn], axis=0)
    r = cb * tb
    cw = cw_ref[...].astype(BF16)
    conv = (cw[0:1, :] * s_all[0:r] + cw[1:2, :] * s_all[cb:cb + r]
            + cw[2:3, :] * s_all[2 * cb:2 * cb + r] + cb_ref[...].astype(BF16))
    ocv = bg_ref[...] * conv * szc_ref[...]
    sub = jnp.dot(jnp.concatenate([o5, ocv], axis=1), wo_ref[...], preferred_element_type=F32)
    gate_vec = mod_ref[2:3, :] * (1.0 / alpha)
    for t in range(tb):
        v = xbuf[slot, t] + gate_vec * sub[t * cb:(t + 1) * cb]
        obuf[slot, t] = _layernorm_rows(v, LN_EPS / (alpha * alpha)) * lng_ref[...] + lnb_ref[...]
    put = functools.partial(_plane_copies, o_hbm, obuf, osem, cb=cb, tb=tb, ntt=ntt, to_hbm=True)

    @pl.when(step > 0)
    def _():
        for cp in put(step - 1, 1 - slot):
            cp.wait()

    for cp in put(step, slot):
        cp.start()

    @pl.when(step == nsteps - 1)
    def _():
        for cp in put(step, slot):
            cp.wait()


def _outproj(x3, mod, yt, szt, s, bg, szc, wo, wts, layer, mod_idx, cb, tb, row_chunks, alpha, ada_next=None):
    nc = x3.shape[0]
    nb, ntt = nc // cb, T // tb
    r = cb * tb
    extra_in, extra_out, extra_shape, extra_scratch, extra_args = [], [], [], [], []
    if ada_next is not None:
        cond, w_ada, b_ada, nl = ada_next
        n3 = w_ada.shape[2]
        mc = n3 // (nb * ntt)
        blk = lambda b, tt: b * ntt + tt
        extra_in = [pl.BlockSpec((COND_ROWS, D), lambda b, tt: (0, 0)),
                    pl.BlockSpec((None, D, mc), lambda b, tt: (nl, 0, blk(b, tt))),
                    pl.BlockSpec((None, 1, mc), lambda b, tt: (nl, 0, blk(b, tt)))]
        extra_out = [pl.BlockSpec((None, 2, mc), lambda b, tt: (0, 0, blk(b, tt)))]
        extra_shape = [jax.ShapeDtypeStruct((1, 2, n3), F32)]
        extra_scratch = [pltpu.VMEM((COND_ROWS, D), BF16)]
        extra_args = [cond, w_ada, b_ada.reshape(b_ada.shape[0], 1, n3)]
    col = lambda b, tt: (0, b * ntt + tt)
    row = lambda b, tt: (b * ntt + tt, 0)
    prev_plane = lambda b, tt: (b * T + (tt * tb + T - 1) % T, 0)
    next_plane = lambda b, tt: (b * T + (tt * tb + tb) % T, 0)
    wgt, bglu, cw, cbias, lng, lnb = wts
    planes = pltpu.VMEM((2, tb, cb, D), F32)
    return pl.pallas_call(
        functools.partial(_outproj_kernel, cb=cb, tb=tb, nb=nb, ntt=ntt, row_chunks=row_chunks, alpha=alpha,
                          ada_next=ada_next is not None),
        grid=(nb, ntt),
        in_specs=[pl.BlockSpec(memory_space=pl.ANY),
                  _layer_spec((3, D), mod_idx, 0, 0),
                  pl.BlockSpec((DS, r), col), pl.BlockSpec((DS, r), col),
                  pl.BlockSpec((r, DS), row),
                  pl.BlockSpec((cb, DS), prev_plane), pl.BlockSpec((cb, DS), next_plane),
                  pl.BlockSpec((r, DS), row), pl.BlockSpec((r, DS), row),
                  _layer_spec((DS, DS), layer, 0, 0), _layer_spec((DS, 1), layer, 0, 0),
                  _layer_spec((3, DS), layer, 0, 0), _layer_spec((1, DS), layer, 0, 0),
                  _layer_spec((2 * DS, D), 0, 0, 0),
                  _layer_spec((1, D), layer, 0, 0), _layer_spec((1, D), layer, 0, 0)] + extra_in,
        out_specs=[pl.BlockSpec(memory_space=pl.ANY)] + extra_out,
        out_shape=[jax.ShapeDtypeStruct(x3.shape, F32)] + extra_shape,
        scratch_shapes=[planes, pltpu.SemaphoreType.DMA((2, tb)), planes, pltpu.SemaphoreType.DMA((2, tb))]
        + extra_scratch,
        compiler_params=_cparams(("arbitrary", "arbitrary")),
        name="outproj",
    )(x3, mod, yt, szt, s, s, s, bg, szc, wgt, bglu, cw, cbias, wo, lng, lnb, *extra_args)


def kernel(x, c, ctx, c_ctx, w_ada, b_ada, w_in, s5_a_re, s5_a_im, s5_log_dt, s5_b_re, s5_b_im,
           s5_c_re, s5_c_im, s5_d, w_glu, b_glu, conv_w, conv_b, w_out, ln_g, ln_b):
    bsz, n, d = x.shape
    n_ctx = ctx.shape[1]
    depth = w_in.shape[0]
    assert bsz == 1 and d == D and n == NCL * T and n_ctx == NCC * T
    alpha = (2.0 * depth) ** 0.25

    cond = jnp.pad(jnp.stack([c[0], c_ctx]), ((0, COND_ROWS - 2), (0, 0)))
    w1t, w2, acoef, w_row, w_t, w_glu_t, mod = _s5prep(s5_a_re, s5_a_im, s5_log_dt, s5_b_re, s5_b_im,
                                                       s5_c_re, s5_c_im, s5_d, w_in, w_glu, cond, w_ada, b_ada)
    mod = mod.reshape(2, 3, D)
    out_w = (w_glu_t, b_glu.reshape(depth, DS, 1), conv_w, conv_b.reshape(depth, 1, DS),
             ln_g.reshape(depth, 1, D), ln_b.reshape(depth, 1, D))

    x3 = x.reshape(n // T, T, D)
    c3 = ctx.reshape(n_ctx // T, T, D)
    tb = 4
    for i in range(depth):
        last = i == depth - 1
        ut, szt, s, bg, szc, *w_next = _inproj(x3, mod, w_t, w_row, 0, CBL, tb,
                                               w_next=None if last else w_in, next_layer=i + 1)
        utc2, *ctx_rest = _inproj(c3, mod, w_t, w_row, 1, NCC, T, u_only=last)
        utc = jnp.pad(utc2.reshape(DS, T, NCC).transpose(1, 0, 2),
                      ((0, 0), (0, 0), (0, NCP - NCT)))
        yt, ytc, wo = _s5(ut, utc, w1t, w2, acoef, w_out, i, ctx_out=not last)
        x3, *mod_next = _outproj(x3, mod, yt, szt, s, bg, szc, wo, out_w, i, 0, CBL, tb, GRID_W // T, alpha,
                                 ada_next=None if last else (cond, w_ada, b_ada, i + 1))
        if not last:
            ytc2 = ytc[:, :, 0:NCC].transpose(1, 0, 2).reshape(DS, n_ctx)
            (c3,) = _outproj(c3, mod, ytc2, *ctx_rest, wo, out_w, i, 1, NCC, T, NCC, alpha)
            w_row, w_t = w_next
            mod = mod_next[0].reshape(2, 3, D)
    return x3.reshape(bsz, n, D)
```

```python
import functools
import math

import jax
import jax.numpy as jnp
from jax import lax
from jax.experimental import pallas as pl
from jax.experimental.pallas import tpu as pltpu

D = 2048
DS = 1024
G = 64
H = 16
P = 64
T = 16
TH = T * H
GRID_W = 64
LN_EPS = 1e-6
LANES = 128
COND_ROWS = 16
VMEM_LIMIT = 60 * 1024 * 1024

F32 = jnp.float32
BF16 = jnp.bfloat16
NT_DIMS = (((1,), (1,)), ((), ()))
TN_DIMS = (((0,), (0,)), ((), ()))


def _cparams(sem):
    return pltpu.CompilerParams(dimension_semantics=sem, vmem_limit_bytes=VMEM_LIMIT)


def _layer_spec(shape, *idx):
    return pl.BlockSpec((None,) + shape, lambda *_: idx, pipeline_mode=pl.Buffered(1))


GELU_C1 = math.sqrt(2.0 / math.pi)
GELU_C2 = 0.044715 * GELU_C1
GLU_W_SCALE = 0.25
GLU_B_SCALE = 0.5
O5_SCALE = 0.25


def _silu(v):
    h = 0.5 * v
    return h * jnp.tanh(h) + h


def _adaln_block(cond_ref, w_ref, b_ref, o_ref, s_scr, first):
    @pl.when(first)
    def _():
        s_scr[...] = _silu(cond_ref[...]).astype(BF16)

    mod = jnp.dot(s_scr[...], w_ref[...].astype(BF16), preferred_element_type=F32)
    o_ref[...] = mod[0:2, :] + b_ref[...]


GP = 8
PRM_ROWS = 40


def _s5prep_kernel(prm_ref, bre_ref, bim_ref, d_ref, win_ref, wglu_ref, cond_ref, wada_ref, bada_ref,
                   w1t_ref, w2_ref, acoef_ref, wrow_ref, wt_ref, wgt_ref, mod_ref, kt_scr, s_scr):
    first = jnp.logical_and(pl.program_id(0) == 0, pl.program_id(1) == 0)
    _adaln_block(cond_ref, wada_ref, bada_ref, mod_ref, s_scr, first)
    wn = win_ref[...]
    wt_ref[...] = wn[:, 0:2 * DS].T.astype(BF16)
    wrow_ref[...] = wn[:, 2 * DS:].astype(BF16)
    wgt_ref[...] = (GLU_W_SCALE * wglu_ref[...].T).astype(BF16)
    lane = lax.broadcasted_iota(jnp.int32, (1, LANES), 1)
    fwd_lane = lane < P
    krow = lax.broadcasted_iota(jnp.int32, (T, 1), 0)
    fwd_row = lax.broadcasted_iota(jnp.int32, (2 * P, 1), 0) < P
    col256 = lax.broadcasted_iota(jnp.int32, (T, TH), 1)
    r16 = lax.broadcasted_iota(jnp.int32, (T, TH), 0)
    diag_mask = (col256 % H == r16).astype(F32)
    e_rep = diag_mask.astype(BF16)
    e_blk = (col256 // H == r16).astype(BF16)

    def rows16(tab):
        return jnp.broadcast_to(tab[:, None, :], (T, H, tab.shape[1])).reshape(TH, tab.shape[1])

    def tile16(mat):
        return jnp.broadcast_to(mat[None], (T, H, mat.shape[1])).reshape(TH, mat.shape[1])

    def bdot(a, b):
        return jnp.dot(a.astype(BF16), b.astype(BF16), preferred_element_type=F32)

    def expand(tab, e):
        return lax.dot_general(tab.astype(BF16), e, TN_DIMS, preferred_element_type=F32)

    for gi in range(GP):
        ar, ai = prm_ref[0, gi, 0:1, :], prm_ref[0, gi, 1:2, :]
        dt = jnp.exp(prm_ref[0, gi, 2:3, :])
        la, th = ar * dt, ai * dt
        cr = tile16(prm_ref[0, gi, 8:24, :].astype(BF16))
        ci = tile16(prm_ref[0, gi, 24:40, :].astype(BF16))
        b_x = [bdot(jnp.concatenate([b_ref[0, 0, gi], b_ref[0, 1, gi]], axis=0), e_rep)
               for b_ref in (bre_ref, bim_ref)]

        mag1 = jnp.exp(la)
        a1r, a1i = mag1 * jnp.cos(th), mag1 * jnp.sin(th)
        squares = [(a1r, a1i)]
        for _ in range(4):
            sr, si = squares[-1]
            squares.append((sr * sr - si * si, 2.0 * sr * si))

        zr, zi = a1r - 1.0, a1i
        den = ar * ar + ai * ai
        fr, fi = (zr * ar + zi * ai) / den, (zi * ar - zr * ai) / den

        def powrow(kf, kb, with_f):
            k = jnp.where(fwd_lane, kf, kb)
            if with_f:
                pr, pi = jnp.broadcast_to(fr, k.shape), jnp.broadcast_to(fi, k.shape)
            else:
                pr, pi = jnp.ones(k.shape, F32), jnp.zeros(k.shape, F32)
            for b, (sr, si) in enumerate(squares):
                bit = ((k >> b) & 1) == 1
                pr, pi = jnp.where(bit, pr * sr - pi * si, pr), jnp.where(bit, pr * si + pi * sr, pi)
            return pr, pi

        def ctab(kf, kb, with_f):
            pr, pi = powrow(kf, kb, with_f)
            pr, pi = rows16(pr.astype(BF16)), rows16(pi.astype(BF16))
            return cr * pr - ci * pi, cr * pi + ci * pr

        bbr_x, bbi_x = b_x
        ckr, cki = ctab(krow, (T - 1) - krow, True)
        zero = jnp.zeros_like(bbr_x)
        kf = bdot(ckr, jnp.where(fwd_row, bbr_x, zero)) - bdot(cki, jnp.where(fwd_row, bbi_x, zero))
        kb = bdot(ckr, jnp.where(fwd_row, zero, bbr_x)) - bdot(cki, jnp.where(fwd_row, zero, bbi_x))
        kt_scr[0:TH - H, :] = kb[0:TH - H]
        kt_scr[TH - H:TH, :] = kb[TH - H:TH] + kf[0:H] + diag_mask * d_ref[0, gi:gi + 1, :]
        kt_scr[TH:2 * TH - H, :] = kf[H:TH]
        for r0 in range(0, TH, LANES):
            for c0 in range(0, TH, LANES):
                blk = (lax.broadcasted_iota(jnp.int32, (LANES, LANES), 1) + c0) // H
                m = jnp.zeros((LANES, LANES), F32)
                for s in range(c0 // H, (c0 + LANES) // H):
                    w0 = (T - 1 - s) * H + r0
                    m = jnp.where(blk == s, kt_scr[w0:w0 + LANES, c0:c0 + LANES], m)
                w1t_ref[0, gi, r0:r0 + LANES, c0:c0 + LANES] = m.astype(BF16)

        pir, pii = powrow((T - 1) - krow, krow, True)
        pxr, pxi = expand(pir, e_blk), expand(pii, e_blk)
        w1t_ref[0, gi, TH:TH + 2 * P, :] = (pxr * bbr_x - pxi * bbi_x).astype(BF16)
        w1t_ref[0, gi, TH + 2 * P:2 * TH, :] = (pxr * bbi_x + pxi * bbr_x).astype(BF16)

        cor, coi = ctab(krow + 1, T - krow, False)
        w2_ref[0, gi, :, 0:LANES] = cor.astype(BF16)
        w2_ref[0, gi, :, LANES:2 * LANES] = (-coi).astype(BF16)

        acoef_ref[0, 0, gi:gi + 1, :] = squares[4][0]
        acoef_ref[0, 1, gi:gi + 1, :] = squares[4][1]


def _s5prep(a_re, a_im, log_dt, b_re, b_im, c_re, c_im, d_skip, w_in, w_glu, cond, w_ada, b_ada):
    depth = a_re.shape[0]
    nsteps = depth * (G // GP)
    wr = D // nsteps
    gr = DS // (G // GP)
    n3 = w_ada.shape[2]
    mc = n3 // (G // GP)
    step = lambda l, g: l * (G // GP) + g
    d_cols = jnp.tile(d_skip.reshape(depth, G, H), (1, 1, T))

    def dircat(v):
        return jnp.concatenate([v[:, 0], v[:, 1]], axis=-1)

    ld = jnp.broadcast_to(log_dt[..., None, None], log_dt.shape + (1, P))
    pad = jnp.zeros((depth, G, 5, 2 * P), F32)
    prm = jnp.concatenate([dircat(a_re[:, :, :, None, :]), dircat(a_im[:, :, :, None, :]), dircat(ld), pad,
                           dircat(c_re), dircat(c_im)], axis=2)
    b_spec = pl.BlockSpec((1, 2, GP, P, H), lambda l, g: (l, 0, g, 0, 0))
    return pl.pallas_call(
        _s5prep_kernel,
        grid=(depth, G // GP),
        in_specs=[pl.BlockSpec((1, GP, PRM_ROWS, 2 * P), lambda l, g: (l, g, 0, 0)), b_spec, b_spec,
                  pl.BlockSpec((1, GP, TH), lambda l, g: (l, g, 0)),
                  pl.BlockSpec((None, wr, 6 * DS), lambda l, g: (0, step(l, g), 0)),
                  pl.BlockSpec((None, gr, DS), lambda l, g: (l, g, 0)),
                  pl.BlockSpec((COND_ROWS, D), lambda l, g: (0, 0)),
                  pl.BlockSpec((None, D, mc), lambda l, g: (l, 0, g)),
                  pl.BlockSpec((None, 1, mc), lambda l, g: (l, 0, g))],
        out_specs=[pl.BlockSpec((1, GP, 2 * TH, TH), lambda l, g: (l, g, 0, 0)),
                   pl.BlockSpec((1, GP, TH, TH), lambda l, g: (l, g, 0, 0)),
                   pl.BlockSpec((1, 2, GP, 2 * P), lambda l, g: (l, 0, g, 0)),
                   pl.BlockSpec((None, wr, 4 * DS), lambda l, g: (0, step(l, g), 0)),
                   pl.BlockSpec((None, 2 * DS, wr), lambda l, g: (0, 0, step(l, g))),
                   pl.BlockSpec((None, DS, gr), lambda l, g: (l, 0, g)),
                   pl.BlockSpec((None, 2, mc), lambda l, g: (l, 0, g))],
        out_shape=[jax.ShapeDtypeStruct((depth, G, 2 * TH, TH), BF16),
                   jax.ShapeDtypeStruct((depth, G, TH, TH), BF16),
                   jax.ShapeDtypeStruct((depth, 2, G, 2 * P), F32),
                   jax.ShapeDtypeStruct((1, D, 4 * DS), BF16),
                   jax.ShapeDtypeStruct((1, 2 * DS, D), BF16),
                   jax.ShapeDtypeStruct((depth, DS, DS), BF16),
                   jax.ShapeDtypeStruct((depth, 2, n3), F32)],
        scratch_shapes=[pltpu.VMEM((2 * TH, TH), F32), pltpu.VMEM((COND_ROWS, D), BF16)],
        compiler_params=_cparams(("arbitrary", "arbitrary")),
        name="prep",
    )(prm, b_re, b_im, d_cols, w_in, w_glu, cond, w_ada, b_ada.reshape(depth, 1, n3))


def _plane_copies(hbm3, buf, sem, step, slot, *, cb, tb, ntt, to_hbm):
    b, tt = step // ntt, step % ntt
    copies = []
    for t in range(tb):
        hbm = hbm3.at[pl.ds(b * cb, cb), tt * tb + t, :]
        vm = buf.at[slot, t]
        copies.append(pltpu.make_async_copy(vm, hbm, sem.at[slot, t]) if to_hbm
                      else pltpu.make_async_copy(hbm, vm, sem.at[slot, t]))
    return copies


def _fetch_planes(x_hbm, xbuf, xsem, step, nsteps, **kw):
    slot = step % 2
    mk = functools.partial(_plane_copies, x_hbm, xbuf, xsem, to_hbm=False, **kw)

    @pl.when(step == 0)
    def _():
        for cp in mk(step, slot):
            cp.start()

    @pl.when(step + 1 < nsteps)
    def _():
        for cp in mk(step + 1, 1 - slot):
            cp.start()

    for cp in mk(step, slot):
        cp.wait()
    return slot


def _layernorm_rows(v, eps=LN_EPS):
    mu = jnp.mean(v, axis=-1, keepdims=True)
    vc = v - mu
    var = jnp.mean(vc * vc, axis=-1, keepdims=True)
    return vc * lax.rsqrt(var + eps)


def _inproj_kernel(*refs, cb, tb, nb, ntt, u_only, cast_next):
    if u_only:
        x_hbm, mod_ref, wut_ref, ut_ref, xbuf, xsem, h_scr = refs
    elif cast_next:
        (x_hbm, mod_ref, wut_ref, wzt_ref, wv_ref, wb_ref, wc_ref, wzc_ref, wnext_ref,
         ut_ref, szt_ref, s_ref, bg_ref, szc_ref, wrow_next_ref, wt_next_ref, xbuf, xsem, h_scr) = refs
        wn = wnext_ref[...]
        wt_next_ref[...] = wn[:, 0:2 * DS].T.astype(BF16)
        wrow_next_ref[...] = wn[:, 2 * DS:].astype(BF16)
    else:
        (x_hbm, mod_ref, wut_ref, wzt_ref, wv_ref, wb_ref, wc_ref, wzc_ref,
         ut_ref, szt_ref, s_ref, bg_ref, szc_ref, xbuf, xsem, h_scr) = refs
    step = pl.program_id(0) * ntt + pl.program_id(1)
    slot = _fetch_planes(x_hbm, xbuf, xsem, step, nb * ntt, cb=cb, tb=tb, ntt=ntt)
    shift, scale = mod_ref[0:1, :], mod_ref[1:2, :]
    for t in range(tb):
        hn = _layernorm_rows(xbuf[slot, t]) * (1.0 + scale) + shift
        h_scr[t * cb:(t + 1) * cb, :] = hn.astype(BF16)
    h = h_scr[...]
    ut_ref[...] = lax.dot_general(wut_ref[...], h, NT_DIMS, preferred_element_type=F32).astype(BF16)
    if u_only:
        return
    szt_ref[...] = _silu(lax.dot_general(wzt_ref[...], h, NT_DIMS, preferred_element_type=F32)).astype(BF16)
    v = jnp.dot(h, wv_ref[...], preferred_element_type=F32)
    cg = jnp.dot(h, wc_ref[...], preferred_element_type=F32)
    s_ref[...] = (cg * v).astype(BF16)
    bg_ref[...] = jnp.dot(h, wb_ref[...], preferred_element_type=F32).astype(BF16)
    szc_ref[...] = _silu(jnp.dot(h, wzc_ref[...], preferred_element_type=F32)).astype(BF16)


def _inproj(x3, mod, wt, wrow, mod_idx, cb, tb, u_only=False, w_next=None, next_layer=0):
    nc = x3.shape[0]
    n = nc * T
    nb, ntt = nc // cb, T // tb
    r = cb * tb
    col = lambda b, tt: (0, b * ntt + tt)
    row = lambda b, tt: (b * ntt + tt, 0)
    in_specs = [pl.BlockSpec(memory_space=pl.ANY), _layer_spec((3, D), mod_idx, 0, 0),
                _layer_spec((DS, D), 0, 0, 0)]
    out_specs = [pl.BlockSpec((DS, r), col)]
    out_shape = [jax.ShapeDtypeStruct((DS, n), BF16)]
    args = [x3, mod, wt]
    if not u_only:
        in_specs += [_layer_spec((DS, D), 0, 1, 0)] + [_layer_spec((D, DS), 0, 0, k) for k in range(4)]
        out_specs += [pl.BlockSpec((DS, r), col)] + [pl.BlockSpec((r, DS), row)] * 3
        out_shape += [jax.ShapeDtypeStruct((DS, n), BF16)] + [jax.ShapeDtypeStruct((n, DS), BF16)] * 3
        args += [wt, wrow, wrow, wrow, wrow]
    if w_next is not None:
        wr = D // (nb * ntt)
        in_specs += [pl.BlockSpec((None, wr, 6 * DS), lambda b, tt: (next_layer, b * ntt + tt, 0))]
        out_specs += [pl.BlockSpec((None, wr, 4 * DS), lambda b, tt: (0, b * ntt + tt, 0)),
                      pl.BlockSpec((None, 2 * DS, wr), lambda b, tt: (0, 0, b * ntt + tt))]
        out_shape += [jax.ShapeDtypeStruct((1, D, 4 * DS), BF16), jax.ShapeDtypeStruct((1, 2 * DS, D), BF16)]
        args += [w_next]
    return pl.pallas_call(
        functools.partial(_inproj_kernel, cb=cb, tb=tb, nb=nb, ntt=ntt, u_only=u_only,
                          cast_next=w_next is not None),
        grid=(nb, ntt),
        in_specs=in_specs,
        out_specs=out_specs,
        out_shape=out_shape,
        scratch_shapes=[pltpu.VMEM((2, tb, cb, D), F32), pltpu.SemaphoreType.DMA((2, tb)),
                        pltpu.VMEM((r, D), BF16)],
        compiler_params=_cparams(("arbitrary", "arbitrary")),
        name="inproj_u" if u_only else "inproj",
    )(*args)


GB = 8
NCL = 512
NCC = 16
NCT = NCL + NCC
NCP = 640
CBL = 128


def _s5_kernel(ut_ref, utc_ref, w1t_ref, w2_ref, acoef_ref, wo_ref, yt_ref, ytc_ref, wo_bf_ref,
               sre_scr, sim_scr, hre_scr, him_scr, yloc_scr, *, ctx_out):
    nbl = NCL // CBL
    s5_rows = pl.program_id(0) < DS // wo_ref.shape[0]
    wo_bf_ref[...] = (jnp.where(s5_rows, O5_SCALE, 1.0) * wo_ref[...]).astype(BF16)
    for gl in range(GB):
        rows = slice(gl * H, (gl + 1) * H)
        planes = []
        for t in range(T):
            lat = [ut_ref[rows, (b * T + t) * CBL:(b * T + t + 1) * CBL] for b in range(nbl)]
            planes.append(jnp.concatenate(lat + [utc_ref[t, rows, :]], axis=1))
        z = jnp.concatenate(planes, axis=0)
        r1 = jnp.dot(w1t_ref[gl], z, preferred_element_type=F32)
        yloc_scr[gl] = r1[0:TH]
        st = r1[TH:2 * TH].T
        sre_scr[pl.ds(gl, NCP, stride=GB), :] = st[:, 0:LANES]
        sim_scr[pl.ds(gl, NCP, stride=GB), :] = st[:, LANES:2 * LANES]

    ar, ai = acoef_ref[0], acoef_ref[1]
    fwd_lane = lax.broadcasted_iota(jnp.int32, (GB, LANES), 1) < P
    bwd_lane = jnp.logical_not(fwd_lane)
    hre_scr[NCT * GB:NCP * GB, :] = jnp.zeros(((NCP - NCT) * GB, LANES), F32)
    him_scr[NCT * GB:NCP * GB, :] = jnp.zeros(((NCP - NCT) * GB, LANES), F32)

    def step(k, carry, fwd_shift):
        hr, hi = carry
        cf = k + fwd_shift
        cb_ = NCT - 1 - k
        rf = pl.ds(pl.multiple_of(cf * GB, GB), GB)
        rb = pl.ds(pl.multiple_of(cb_ * GB, GB), GB)
        pltpu.store(hre_scr.at[rf, :], hr, mask=fwd_lane)
        pltpu.store(hre_scr.at[rb, :], hr, mask=bwd_lane)
        pltpu.store(him_scr.at[rf, :], hi, mask=fwd_lane)
        pltpu.store(him_scr.at[rb, :], hi, mask=bwd_lane)
        sr = jnp.where(fwd_lane, sre_scr[rf, :], sre_scr[rb, :])
        si = jnp.where(fwd_lane, sim_scr[rf, :], sim_scr[rb, :])
        return ar * hr - ai * hi + sr, ar * hi + ai * hr + si

    zero = jnp.zeros((GB, LANES), F32)
    carry = lax.fori_loop(0, NCC, functools.partial(step, fwd_shift=NCL), (zero, zero))
    lax.fori_loop(NCC, NCT, functools.partial(step, fwd_shift=-NCC), carry, unroll=4)

    for gl in range(GB):
        rows = slice(gl * H, (gl + 1) * H)
        sel = pl.ds(gl, NCP, stride=GB)
        hp = jnp.concatenate([hre_scr[sel, :], him_scr[sel, :]], axis=1)
        y = yloc_scr[gl] + lax.dot_general(w2_ref[gl], hp.astype(BF16), NT_DIMS, preferred_element_type=F32)
        yb = y.astype(BF16)
        for t in range(T):
            for b in range(nbl):
                yt_ref[rows, (b * T + t) * CBL:(b * T + t + 1) * CBL] = \
                    yb[t * H:(t + 1) * H, b * CBL:(b + 1) * CBL]
            if ctx_out:
                ytc_ref[t, rows, :] = yb[t * H:(t + 1) * H, NCL:NCP]
    if not ctx_out:
        ytc_ref[...] = jnp.zeros(ytc_ref.shape, BF16)


def _s5(ut, utc, w1t, w2, acoef, w_out, layer, ctx_out):
    n = ut.shape[1]
    ncx = NCP - NCL
    wr = 2 * DS // (G // GB)
    state = pltpu.VMEM((NCP * GB, LANES), F32)
    return pl.pallas_call(
        functools.partial(_s5_kernel, ctx_out=ctx_out),
        grid=(G // GB,),
        in_specs=[pl.BlockSpec((GB * H, n), lambda i: (i, 0)),
                  pl.BlockSpec((T, GB * H, ncx), lambda i: (0, i, 0)),
                  pl.BlockSpec((None, GB, 2 * TH, TH), lambda i: (layer, i, 0, 0)),
                  pl.BlockSpec((None, GB, TH, TH), lambda i: (layer, i, 0, 0)),
                  pl.BlockSpec((None, 2, GB, LANES), lambda i: (layer, 0, i, 0)),
                  pl.BlockSpec((None, wr, D), lambda i: (layer, i, 0))],
        out_specs=[pl.BlockSpec((GB * H, n), lambda i: (i, 0)),
                   pl.BlockSpec((T, GB * H, ncx), lambda i: (0, i, 0)),
                   pl.BlockSpec((None, wr, D), lambda i: (0, i, 0))],
        out_shape=[jax.ShapeDtypeStruct((DS, n), BF16), jax.ShapeDtypeStruct((T, DS, ncx), BF16),
                   jax.ShapeDtypeStruct((1, 2 * DS, D), BF16)],
        scratch_shapes=[state, state, state, state, pltpu.VMEM((GB, TH, NCP), F32)],
        compiler_params=_cparams(("parallel",)),
        name="s5",
    )(ut, utc, w1t, w2, acoef, w_out)


def _outproj_kernel(x_hbm, mod_ref, yt_ref, szt_ref, s_ref, sprev_ref, snext_ref, bg_ref, szc_ref,
                    wgt_ref, bglu_ref, cw_ref, cb_ref, wo_ref, lng_ref, lnb_ref,
                    o_hbm, xbuf, xsem, obuf, osem, *, cb, tb, nb, ntt, row_chunks, alpha):
    tt = pl.program_id(1)
    step = pl.program_id(0) * ntt + tt
    nsteps = nb * ntt
    slot = _fetch_planes(x_hbm, xbuf, xsem, step, nsteps, cb=cb, tb=tb, ntt=ntt)
    y = yt_ref[...].astype(F32)
    g2 = y * jnp.tanh(y * (GELU_C1 + GELU_C2 * (y * y))) + y
    half_gate = jnp.dot(wgt_ref[...], g2.astype(BF16), preferred_element_type=F32) + bglu_ref[...]
    o5 = (g2 * szt_ref[...].astype(F32) * (jnp.tanh(half_gate) + 1.0)).T.astype(BF16)
    crow = lax.broadcasted_iota(jnp.int32, (cb, 1), 0) % row_chunks
    sp = sprev_ref[...].astype(F32)
    sn = snext_ref[...].astype(F32)
    sp_wrap = jnp.where(crow != 0, pltpu.roll(sp, 1, 0), 0.0)
    sn_wrap = jnp.where(crow != row_chunks - 1, pltpu.roll(sn, cb - 1, 0), 0.0)
    sp = jnp.where(tt == 0, sp_wrap, sp).astype(BF16)
    sn = jnp.where(tt == ntt - 1, sn_wrap, sn).astype(BF16)
    s_all = jnp.concatenate([sp, s_ref[...], sn], axis=0)
    r = cb * tb
    cw = cw_ref[...].astype(BF16)
    conv = (cw[0:1, :] * s_all[0:r] + cw[1:2, :] * s_all[cb:cb + r]
            + cw[2:3, :] * s_all[2 * cb:2 * cb + r] + cb_ref[...].astype(BF16))
    ocv = bg_ref[...] * conv * szc_ref[...]
    sub = jnp.dot(jnp.concatenate([o5, ocv], axis=1), wo_ref[...], preferred_element_type=F32)
    gate_vec = mod_ref[2:3, :] * (1.0 / alpha)
    for t in range(tb):
        v = xbuf[slot, t] + gate_vec * sub[t * cb:(t + 1) * cb]
        obuf[slot, t] = _layernorm_rows(v, LN_EPS / (alpha * alpha)) * lng_ref[...] + lnb_ref[...]
    put = functools.partial(_plane_copies, o_hbm, obuf, osem, cb=cb, tb=tb, ntt=ntt, to_hbm=True)

    @pl.when(step > 0)
    def _():
        for cp in put(step - 1, 1 - slot):
            cp.wait()

    for cp in put(step, slot):
        cp.start()

    @pl.when(step == nsteps - 1)
    def _():
        for cp in put(step, slot):
            cp.wait()


def _outproj(x3, mod, yt, szt, s, bg, szc, wo, wts, layer, mod_idx, cb, tb, row_chunks, alpha):
    nc = x3.shape[0]
    nb, ntt = nc // cb, T // tb
    r = cb * tb
    col = lambda b, tt: (0, b * ntt + tt)
    row = lambda b, tt: (b * ntt + tt, 0)
    prev_plane = lambda b, tt: (b * T + (tt * tb + T - 1) % T, 0)
    next_plane = lambda b, tt: (b * T + (tt * tb + tb) % T, 0)
    wgt, bglu, cw, cbias, lng, lnb = wts
    planes = pltpu.VMEM((2, tb, cb, D), F32)
    return pl.pallas_call(
        functools.partial(_outproj_kernel, cb=cb, tb=tb, nb=nb, ntt=ntt, row_chunks=row_chunks, alpha=alpha),
        grid=(nb, ntt),
        in_specs=[pl.BlockSpec(memory_space=pl.ANY),
                  _layer_spec((3, D), mod_idx, 0, 0),
                  pl.BlockSpec((DS, r), col), pl.BlockSpec((DS, r), col),
                  pl.BlockSpec((r, DS), row),
                  pl.BlockSpec((cb, DS), prev_plane), pl.BlockSpec((cb, DS), next_plane),
                  pl.BlockSpec((r, DS), row), pl.BlockSpec((r, DS), row),
                  _layer_spec((DS, DS), layer, 0, 0), _layer_spec((DS, 1), layer, 0, 0),
                  _layer_spec((3, DS), layer, 0, 0), _layer_spec((1, DS), layer, 0, 0),
                  _layer_spec((2 * DS, D), 0, 0, 0),
                  _layer_spec((1, D), layer, 0, 0), _layer_spec((1, D), layer, 0, 0)],
        out_specs=pl.BlockSpec(memory_space=pl.ANY),
        out_shape=jax.ShapeDtypeStruct(x3.shape, F32),
        scratch_shapes=[planes, pltpu.SemaphoreType.DMA((2, tb)), planes, pltpu.SemaphoreType.DMA((2, tb))],
        compiler_params=_cparams(("arbitrary", "arbitrary")),
        name="outproj",
    )(x3, mod, yt, szt, s, s, s, bg, szc, wgt, bglu, cw, cbias, wo, lng, lnb)


def kernel(x, c, ctx, c_ctx, w_ada, b_ada, w_in, s5_a_re, s5_a_im, s5_log_dt, s5_b_re, s5_b_im,
           s5_c_re, s5_c_im, s5_d, w_glu, b_glu, conv_w, conv_b, w_out, ln_g, ln_b):
    bsz, n, d = x.shape
    n_ctx = ctx.shape[1]
    depth = w_in.shape[0]
    assert bsz == 1 and d == D and n == NCL * T and n_ctx == NCC * T
    alpha = (2.0 * depth) ** 0.25

    cond = jnp.pad(jnp.stack([c[0], c_ctx]), ((0, COND_ROWS - 2), (0, 0)))
    w1t, w2, acoef, w_row, w_t, w_glu_t, mod = _s5prep(s5_a_re, s5_a_im, s5_log_dt, s5_b_re, s5_b_im,
                                                       s5_c_re, s5_c_im, s5_d, w_in, w_glu, cond, w_ada, b_ada)
    mod = mod.reshape(depth * 2, 3, D)
    out_w = (w_glu_t, GLU_B_SCALE * b_glu.reshape(depth, DS, 1), conv_w, conv_b.reshape(depth, 1, DS),
             ln_g.reshape(depth, 1, D), ln_b.reshape(depth, 1, D))

    x3 = x.reshape(n // T, T, D)
    c3 = ctx.reshape(n_ctx // T, T, D)
    tb = 4
    for i in range(depth):
        last = i == depth - 1
        ut, szt, s, bg, szc, *w_next = _inproj(x3, mod, w_t, w_row, 2 * i, CBL, tb,
                                               w_next=None if last else w_in, next_layer=i + 1)
        utc2, *ctx_rest = _inproj(c3, mod, w_t, w_row, 2 * i + 1, NCC, T, u_only=last)
        utc = jnp.pad(utc2.reshape(DS, T, NCC).transpose(1, 0, 2),
                      ((0, 0), (0, 0), (0, NCP - NCT)))
        yt, ytc, wo = _s5(ut, utc, w1t, w2, acoef, w_out, i, ctx_out=not last)
        x3 = _outproj(x3, mod, yt, szt, s, bg, szc, wo, out_w, i, 2 * i, CBL, tb, GRID_W // T, alpha)
        if not last:
            ytc2 = ytc[:, :, 0:NCC].transpose(1, 0, 2).reshape(DS, n_ctx)
            c3 = _outproj(c3, mod, ytc2, *ctx_rest, wo, out_w, i, 2 * i + 1, NCC, T, NCC, alpha)
            w_row, w_t = w_next
    return x3.reshape(bsz, n, D)
```

```python
import functools
import math

import jax
import jax.numpy as jnp
from jax import lax
from jax.experimental import pallas as pl
from jax.experimental.pallas import tpu as pltpu

D = 2048
DS = 1024
G = 64
H = 16
P = 64
T = 16
TH = T * H
GRID_W = 64
LN_EPS = 1e-6
LANES = 128
COND_ROWS = 16
VMEM_LIMIT = 60 * 1024 * 1024

F32 = jnp.float32
BF16 = jnp.bfloat16
NT_DIMS = (((1,), (1,)), ((), ()))
TN_DIMS = (((0,), (0,)), ((), ()))


def _cparams(sem):
    return pltpu.CompilerParams(dimension_semantics=sem, vmem_limit_bytes=VMEM_LIMIT)


def _layer_spec(shape, *idx):
    return pl.BlockSpec((None,) + shape, lambda *_: idx, pipeline_mode=pl.Buffered(1))


GELU_C1 = math.sqrt(2.0 / math.pi)
GELU_C2 = 0.044715 * GELU_C1
GLU_W_SCALE = 0.25
GLU_B_SCALE = 0.5
O5_SCALE = 0.25


def _silu(v):
    h = 0.5 * v
    return h * jnp.tanh(h) + h


def _adaln_rows(cond_ref, w_ref, b_ref, o_ref, s_scr, first, g):
    nblk, _, rb = s_scr.shape

    @pl.when(first)
    def _():
        for k in range(nblk):
            s_scr[k] = _silu(cond_ref[:, k * rb:(k + 1) * rb]).astype(BF16)

    part = jnp.dot(s_scr[g], w_ref[...].astype(BF16), preferred_element_type=F32)[0:2, :]

    @pl.when(g == 0)
    def _():
        o_ref[...] = part + b_ref[...]

    @pl.when(g > 0)
    def _():
        o_ref[...] += part


GP = 8
PRM_ROWS = 40


def _s5prep_kernel(prm_ref, bre_ref, bim_ref, d_ref, win_ref, wglu_ref, cond_ref, wada_ref, bada_ref,
                   w1t_ref, w2_ref, acoef_ref, wrow_ref, wt_ref, wgt_ref, mod_ref, kt_scr, s_scr):
    first = jnp.logical_and(pl.program_id(0) == 0, pl.program_id(1) == 0)
    _adaln_rows(cond_ref, wada_ref, bada_ref, mod_ref, s_scr, first, pl.program_id(1))
    wn = win_ref[...]
    wt_ref[...] = wn[:, 0:2 * DS].T.astype(BF16)
    wrow_ref[...] = wn[:, 2 * DS:].astype(BF16)
    wgt_ref[...] = (GLU_W_SCALE * wglu_ref[...].T).astype(BF16)
    lane = lax.broadcasted_iota(jnp.int32, (1, LANES), 1)
    fwd_lane = lane < P
    krow = lax.broadcasted_iota(jnp.int32, (T, 1), 0)
    fwd_row = lax.broadcasted_iota(jnp.int32, (2 * P, 1), 0) < P
    col256 = lax.broadcasted_iota(jnp.int32, (T, TH), 1)
    r16 = lax.broadcasted_iota(jnp.int32, (T, TH), 0)
    diag_mask = (col256 % H == r16).astype(F32)
    e_rep = diag_mask.astype(BF16)
    e_blk = (col256 // H == r16).astype(BF16)

    def rows16(tab):
        return jnp.broadcast_to(tab[:, None, :], (T, H, tab.shape[1])).reshape(TH, tab.shape[1])

    def tile16(mat):
        return jnp.broadcast_to(mat[None], (T, H, mat.shape[1])).reshape(TH, mat.shape[1])

    def bdot(a, b):
        return jnp.dot(a.astype(BF16), b.astype(BF16), preferred_element_type=F32)

    def expand(tab, e):
        return lax.dot_general(tab.astype(BF16), e, TN_DIMS, preferred_element_type=F32)

    for gi in range(GP):
        ar, ai = prm_ref[0, gi, 0:1, :], prm_ref[0, gi, 1:2, :]
        dt = jnp.exp(prm_ref[0, gi, 2:3, :])
        la, th = ar * dt, ai * dt
        cr = tile16(prm_ref[0, gi, 8:24, :].astype(BF16))
        ci = tile16(prm_ref[0, gi, 24:40, :].astype(BF16))
        b_x = [bdot(jnp.concatenate([b_ref[0, 0, gi], b_ref[0, 1, gi]], axis=0), e_rep)
               for b_ref in (bre_ref, bim_ref)]

        mag1 = jnp.exp(la)
        a1r, a1i = mag1 * jnp.cos(th), mag1 * jnp.sin(th)
        squares = [(a1r, a1i)]
        for _ in range(4):
            sr, si = squares[-1]
            squares.append((sr * sr - si * si, 2.0 * sr * si))

        zr, zi = a1r - 1.0, a1i
        den = ar * ar + ai * ai
        fr, fi = (zr * ar + zi * ai) / den, (zi * ar - zr * ai) / den

        def powrow(kf, kb, with_f):
            k = jnp.where(fwd_lane, kf, kb)
            if with_f:
                pr, pi = jnp.broadcast_to(fr, k.shape), jnp.broadcast_to(fi, k.shape)
            else:
                pr, pi = jnp.ones(k.shape, F32), jnp.zeros(k.shape, F32)
            for b, (sr, si) in enumerate(squares):
                bit = ((k >> b) & 1) == 1
                pr, pi = jnp.where(bit, pr * sr - pi * si, pr), jnp.where(bit, pr * si + pi * sr, pi)
            return pr, pi

        def ctab(kf, kb, with_f):
            pr, pi = powrow(kf, kb, with_f)
            pr, pi = rows16(pr.astype(BF16)), rows16(pi.astype(BF16))
            return cr * pr - ci * pi, cr * pi + ci * pr

        bbr_x, bbi_x = b_x
        ckr, cki = ctab(krow, (T - 1) - krow, True)
        zero = jnp.zeros_like(bbr_x)
        kf = bdot(ckr, jnp.where(fwd_row, bbr_x, zero)) - bdot(cki, jnp.where(fwd_row, bbi_x, zero))
        kb = bdot(ckr, jnp.where(fwd_row, zero, bbr_x)) - bdot(cki, jnp.where(fwd_row, zero, bbi_x))
        kt_scr[0:TH - H, :] = kb[0:TH - H]
        kt_scr[TH - H:TH, :] = kb[TH - H:TH] + kf[0:H] + diag_mask * d_ref[0, gi:gi + 1, :]
        kt_scr[TH:2 * TH - H, :] = kf[H:TH]
        for r0 in range(0, TH, LANES):
            for c0 in range(0, TH, LANES):
                blk = (lax.broadcasted_iota(jnp.int32, (LANES, LANES), 1) + c0) // H
                m = jnp.zeros((LANES, LANES), F32)
                for s in range(c0 // H, (c0 + LANES) // H):
                    w0 = (T - 1 - s) * H + r0
                    m = jnp.where(blk == s, kt_scr[w0:w0 + LANES, c0:c0 + LANES], m)
                w1t_ref[0, gi, r0:r0 + LANES, c0:c0 + LANES] = m.astype(BF16)

        pir, pii = powrow((T - 1) - krow, krow, True)
        pxr, pxi = expand(pir, e_blk), expand(pii, e_blk)
        w1t_ref[0, gi, TH:TH + 2 * P, :] = (pxr * bbr_x - pxi * bbi_x).astype(BF16)
        w1t_ref[0, gi, TH + 2 * P:2 * TH, :] = (pxr * bbi_x + pxi * bbr_x).astype(BF16)

        cor, coi = ctab(krow + 1, T - krow, False)
        w2_ref[0, gi, :, 0:LANES] = cor.astype(BF16)
        w2_ref[0, gi, :, LANES:2 * LANES] = (-coi).astype(BF16)

        acoef_ref[0, 0, gi:gi + 1, :] = squares[4][0]
        acoef_ref[0, 1, gi:gi + 1, :] = squares[4][1]


def _s5prep(a_re, a_im, log_dt, b_re, b_im, c_re, c_im, d_skip, w_in, w_glu, cond, w_ada, b_ada):
    depth = a_re.shape[0]
    nsteps = depth * (G // GP)
    wr = D // nsteps
    gr = DS // (G // GP)
    n3 = w_ada.shape[2]
    ar = D // (G // GP)
    step = lambda l, g: l * (G // GP) + g
    d_cols = jnp.tile(d_skip.reshape(depth, G, H), (1, 1, T))

    def dircat(v):
        return jnp.concatenate([v[:, 0], v[:, 1]], axis=-1)

    ld = jnp.broadcast_to(log_dt[..., None, None], log_dt.shape + (1, P))
    pad = jnp.zeros((depth, G, 5, 2 * P), F32)
    prm = jnp.concatenate([dircat(a_re[:, :, :, None, :]), dircat(a_im[:, :, :, None, :]), dircat(ld), pad,
                           dircat(c_re), dircat(c_im)], axis=2)
    b_spec = pl.BlockSpec((1, 2, GP, P, H), lambda l, g: (l, 0, g, 0, 0))
    return pl.pallas_call(
        _s5prep_kernel,
        grid=(depth, G // GP),
        in_specs=[pl.BlockSpec((1, GP, PRM_ROWS, 2 * P), lambda l, g: (l, g, 0, 0)), b_spec, b_spec,
                  pl.BlockSpec((1, GP, TH), lambda l, g: (l, g, 0)),
                  pl.BlockSpec((None, wr, 6 * DS), lambda l, g: (0, step(l, g), 0)),
                  pl.BlockSpec((None, gr, DS), lambda l, g: (l, g, 0)),
                  pl.BlockSpec((COND_ROWS, D), lambda l, g: (0, 0)),
                  pl.BlockSpec((None, ar, n3), lambda l, g: (l, g, 0)),
                  pl.BlockSpec((None, 1, n3), lambda l, g: (l, 0, 0))],
        out_specs=[pl.BlockSpec((1, GP, 2 * TH, TH), lambda l, g: (l, g, 0, 0)),
                   pl.BlockSpec((1, GP, TH, TH), lambda l, g: (l, g, 0, 0)),
                   pl.BlockSpec((1, 2, GP, 2 * P), lambda l, g: (l, 0, g, 0)),
                   pl.BlockSpec((None, wr, 4 * DS), lambda l, g: (0, step(l, g), 0)),
                   pl.BlockSpec((None, 2 * DS, wr), lambda l, g: (0, 0, step(l, g))),
                   pl.BlockSpec((None, DS, gr), lambda l, g: (l, 0, g)),
                   pl.BlockSpec((None, 2, n3), lambda l, g: (l, 0, 0))],
        out_shape=[jax.ShapeDtypeStruct((depth, G, 2 * TH, TH), BF16),
                   jax.ShapeDtypeStruct((depth, G, TH, TH), BF16),
                   jax.ShapeDtypeStruct((depth, 2, G, 2 * P), F32),
                   jax.ShapeDtypeStruct((1, D, 4 * DS), BF16),
                   jax.ShapeDtypeStruct((1, 2 * DS, D), BF16),
                   jax.ShapeDtypeStruct((depth, DS, DS), BF16),
                   jax.ShapeDtypeStruct((depth, 2, n3), F32)],
        scratch_shapes=[pltpu.VMEM((2 * TH, TH), F32), pltpu.VMEM((D // ar, COND_ROWS, ar), BF16)],
        compiler_params=_cparams(("arbitrary", "arbitrary")),
        name="prep",
    )(prm, b_re, b_im, d_cols, w_in, w_glu, cond, w_ada, b_ada.reshape(depth, 1, n3))


def _plane_copies(hbm3, buf, sem, step, slot, *, cb, tb, ntt, to_hbm):
    b, tt = step // ntt, step % ntt
    copies = []
    for t in range(tb):
        hbm = hbm3.at[pl.ds(b * cb, cb), tt * tb + t, :]
        vm = buf.at[slot, t]
        copies.append(pltpu.make_async_copy(vm, hbm, sem.at[slot, t]) if to_hbm
                      else pltpu.make_async_copy(hbm, vm, sem.at[slot, t]))
    return copies


def _fetch_planes(x_hbm, xbuf, xsem, step, nsteps, **kw):
    slot = step % 2
    mk = functools.partial(_plane_copies, x_hbm, xbuf, xsem, to_hbm=False, **kw)

    @pl.when(step == 0)
    def _():
        for cp in mk(step, slot):
            cp.start()

    @pl.when(step + 1 < nsteps)
    def _():
        for cp in mk(step + 1, 1 - slot):
            cp.start()

    for cp in mk(step, slot):
        cp.wait()
    return slot


def _layernorm_rows(v, eps=LN_EPS):
    mu = jnp.mean(v, axis=-1, keepdims=True)
    vc = v - mu
    var = jnp.mean(vc * vc, axis=-1, keepdims=True)
    return vc * lax.rsqrt(var + eps)


def _inproj_kernel(*refs, cb, tb, nb, ntt, u_only, cast_next):
    if u_only:
        x_hbm, mod_ref, wut_ref, ut_ref, xbuf, xsem, h_scr = refs
    elif cast_next:
        (x_hbm, mod_ref, wut_ref, wzt_ref, wv_ref, wb_ref, wc_ref, wzc_ref, wnext_ref,
         ut_ref, szt_ref, s_ref, bg_ref, szc_ref, wrow_next_ref, wt_next_ref, xbuf, xsem, h_scr) = refs
        wn = wnext_ref[...]
        wt_next_ref[...] = wn[:, 0:2 * DS].T.astype(BF16)
        wrow_next_ref[...] = wn[:, 2 * DS:].astype(BF16)
    else:
        (x_hbm, mod_ref, wut_ref, wzt_ref, wv_ref, wb_ref, wc_ref, wzc_ref,
         ut_ref, szt_ref, s_ref, bg_ref, szc_ref, xbuf, xsem, h_scr) = refs
    step = pl.program_id(0) * ntt + pl.program_id(1)
    slot = _fetch_planes(x_hbm, xbuf, xsem, step, nb * ntt, cb=cb, tb=tb, ntt=ntt)
    shift, scale = mod_ref[0:1, :], mod_ref[1:2, :]
    for t in range(tb):
        hn = _layernorm_rows(xbuf[slot, t]) * (1.0 + scale) + shift
        h_scr[t * cb:(t + 1) * cb, :] = hn.astype(BF16)
    h = h_scr[...]
    ut_ref[...] = lax.dot_general(wut_ref[...], h, NT_DIMS, preferred_element_type=F32).astype(BF16)
    if u_only:
        return
    szt_ref[...] = _silu(lax.dot_general(wzt_ref[...], h, NT_DIMS, preferred_element_type=F32)).astype(BF16)
    v = jnp.dot(h, wv_ref[...], preferred_element_type=F32)
    cg = jnp.dot(h, wc_ref[...], preferred_element_type=F32)
    s_ref[...] = (cg * v).astype(BF16)
    bg_ref[...] = jnp.dot(h, wb_ref[...], preferred_element_type=F32).astype(BF16)
    szc_ref[...] = _silu(jnp.dot(h, wzc_ref[...], preferred_element_type=F32)).astype(BF16)


def _inproj(x3, mod, wt, wrow, mod_idx, cb, tb, u_only=False, w_next=None, next_layer=0):
    nc = x3.shape[0]
    n = nc * T
    nb, ntt = nc // cb, T // tb
    r = cb * tb
    col = lambda b, tt: (0, b * ntt + tt)
    row = lambda b, tt: (b * ntt + tt, 0)
    in_specs = [pl.BlockSpec(memory_space=pl.ANY), _layer_spec((3, D), mod_idx, 0, 0),
                _layer_spec((DS, D), 0, 0, 0)]
    out_specs = [pl.BlockSpec((DS, r), col)]
    out_shape = [jax.ShapeDtypeStruct((DS, n), BF16)]
    args = [x3, mod, wt]
    if not u_only:
        in_specs += [_layer_spec((DS, D), 0, 1, 0)] + [_layer_spec((D, DS), 0, 0, k) for k in range(4)]
        out_specs += [pl.BlockSpec((DS, r), col)] + [pl.BlockSpec((r, DS), row)] * 3
        out_shape += [jax.ShapeDtypeStruct((DS, n), BF16)] + [jax.ShapeDtypeStruct((n, DS), BF16)] * 3
        args += [wt, wrow, wrow, wrow, wrow]
    if w_next is not None:
        wr = D // (nb * ntt)
        in_specs += [pl.BlockSpec((None, wr, 6 * DS), lambda b, tt: (next_layer, b * ntt + tt, 0))]
        out_specs += [pl.BlockSpec((None, wr, 4 * DS), lambda b, tt: (0, b * ntt + tt, 0)),
                      pl.BlockSpec((None, 2 * DS, wr), lambda b, tt: (0, 0, b * ntt + tt))]
        out_shape += [jax.ShapeDtypeStruct((1, D, 4 * DS), BF16), jax.ShapeDtypeStruct((1, 2 * DS, D), BF16)]
        args += [w_next]
    return pl.pallas_call(
        functools.partial(_inproj_kernel, cb=cb, tb=tb, nb=nb, ntt=ntt, u_only=u_only,
                          cast_next=w_next is not None),
        grid=(nb, ntt),
        in_specs=in_specs,
        out_specs=out_specs,
        out_shape=out_shape,
        scratch_shapes=[pltpu.VMEM((2, tb, cb, D), F32), pltpu.SemaphoreType.DMA((2, tb)),
                        pltpu.VMEM((r, D), BF16)],
        compiler_params=_cparams(("arbitrary", "arbitrary")),
        name="inproj_u" if u_only else "inproj",
    )(*args)


GB = 8
NCL = 512
NCC = 16
NCT = NCL + NCC
NCP = 640
CBL = 128


def _s5_kernel(ut_ref, utc_ref, w1t_ref, w2_ref, acoef_ref, wo_ref, yt_ref, ytc_ref, wo_bf_ref,
               sre_scr, sim_scr, hre_scr, him_scr, yloc_scr, *, ctx_out):
    nbl = NCL // CBL
    s5_rows = pl.program_id(0) < DS // wo_ref.shape[0]
    wo_bf_ref[...] = (jnp.where(s5_rows, O5_SCALE, 1.0) * wo_ref[...]).astype(BF16)
    for gl in range(GB):
        rows = slice(gl * H, (gl + 1) * H)
        planes = []
        for t in range(T):
            lat = [ut_ref[rows, (b * T + t) * CBL:(b * T + t + 1) * CBL] for b in range(nbl)]
            planes.append(jnp.concatenate(lat + [utc_ref[t, rows, :]], axis=1))
        z = jnp.concatenate(planes, axis=0)
        r1 = jnp.dot(w1t_ref[gl], z, preferred_element_type=F32)
        yloc_scr[gl] = r1[0:TH]
        st = r1[TH:2 * TH].T
        sre_scr[pl.ds(gl, NCP, stride=GB), :] = st[:, 0:LANES]
        sim_scr[pl.ds(gl, NCP, stride=GB), :] = st[:, LANES:2 * LANES]

    ar, ai = acoef_ref[0], acoef_ref[1]
    fwd_lane = lax.broadcasted_iota(jnp.int32, (GB, LANES), 1) < P
    bwd_lane = jnp.logical_not(fwd_lane)
    hre_scr[NCT * GB:NCP * GB, :] = jnp.zeros(((NCP - NCT) * GB, LANES), F32)
    him_scr[NCT * GB:NCP * GB, :] = jnp.zeros(((NCP - NCT) * GB, LANES), F32)

    def step(k, carry, fwd_shift):
        hr, hi = carry
        cf = k + fwd_shift
        cb_ = NCT - 1 - k
        rf = pl.ds(pl.multiple_of(cf * GB, GB), GB)
        rb = pl.ds(pl.multiple_of(cb_ * GB, GB), GB)
        pltpu.store(hre_scr.at[rf, :], hr, mask=fwd_lane)
        pltpu.store(hre_scr.at[rb, :], hr, mask=bwd_lane)
        pltpu.store(him_scr.at[rf, :], hi, mask=fwd_lane)
        pltpu.store(him_scr.at[rb, :], hi, mask=bwd_lane)
        sr = jnp.where(fwd_lane, sre_scr[rf, :], sre_scr[rb, :])
        si = jnp.where(fwd_lane, sim_scr[rf, :], sim_scr[rb, :])
        return ar * hr - ai * hi + sr, ar * hi + ai * hr + si

    zero = jnp.zeros((GB, LANES), F32)
    carry = lax.fori_loop(0, NCC, functools.partial(step, fwd_shift=NCL), (zero, zero))
    lax.fori_loop(NCC, NCT, functools.partial(step, fwd_shift=-NCC), carry, unroll=4)

    for gl in range(GB):
        rows = slice(gl * H, (gl + 1) * H)
        sel = pl.ds(gl, NCP, stride=GB)
        hp = jnp.concatenate([hre_scr[sel, :], him_scr[sel, :]], axis=1)
        y = yloc_scr[gl] + lax.dot_general(w2_ref[gl], hp.astype(BF16), NT_DIMS, preferred_element_type=F32)
        yb = y.astype(BF16)
        for t in range(T):
            for b in range(nbl):
                yt_ref[rows, (b * T + t) * CBL:(b * T + t + 1) * CBL] = \
                    yb[t * H:(t + 1) * H, b * CBL:(b + 1) * CBL]
            if ctx_out:
                ytc_ref[t, rows, :] = yb[t * H:(t + 1) * H, NCL:NCP]
    if not ctx_out:
        ytc_ref[...] = jnp.zeros(ytc_ref.shape, BF16)


def _s5(ut, utc, w1t, w2, acoef, w_out, layer, ctx_out):
    n = ut.shape[1]
    ncx = NCP - NCL
    wr = 2 * DS // (G // GB)
    state = pltpu.VMEM((NCP * GB, LANES), F32)
    return pl.pallas_call(
        functools.partial(_s5_kernel, ctx_out=ctx_out),
        grid=(G // GB,),
        in_specs=[pl.BlockSpec((GB * H, n), lambda i: (i, 0)),
                  pl.BlockSpec((T, GB * H, ncx), lambda i: (0, i, 0)),
                  pl.BlockSpec((None, GB, 2 * TH, TH), lambda i: (layer, i, 0, 0)),
                  pl.BlockSpec((None, GB, TH, TH), lambda i: (layer, i, 0, 0)),
                  pl.BlockSpec((None, 2, GB, LANES), lambda i: (layer, 0, i, 0)),
                  pl.BlockSpec((None, wr, D), lambda i: (layer, i, 0))],
        out_specs=[pl.BlockSpec((GB * H, n), lambda i: (i, 0)),
                   pl.BlockSpec((T, GB * H, ncx), lambda i: (0, i, 0)),
                   pl.BlockSpec((None, wr, D), lambda i: (0, i, 0))],
        out_shape=[jax.ShapeDtypeStruct((DS, n), BF16), jax.ShapeDtypeStruct((T, DS, ncx), BF16),
                   jax.ShapeDtypeStruct((1, 2 * DS, D), BF16)],
        scratch_shapes=[state, state, state, state, pltpu.VMEM((GB, TH, NCP), F32)],
        compiler_params=_cparams(("parallel",)),
        name="s5",
    )(ut, utc, w1t, w2, acoef, w_out)


def _outproj_kernel(x_hbm, mod_ref, yt_ref, szt_ref, s_ref, sprev_ref, snext_ref, bg_ref, szc_ref,
                    wgt_ref, bglu_ref, cw_ref, cb_ref, wo_ref, lng_ref, lnb_ref,
                    o_hbm, xbuf, xsem, obuf, osem, *, cb, tb, nb, ntt, row_chunks, alpha):
    tt = pl.program_id(1)
    step = pl.program_id(0) * ntt + tt
    nsteps = nb * ntt
    slot = _fetch_planes(x_hbm, xbuf, xsem, step, nsteps, cb=cb, tb=tb, ntt=ntt)
    y = yt_ref[...].astype(F32)
    g2 = y * jnp.tanh(y * (GELU_C1 + GELU_C2 * (y * y))) + y
    half_gate = jnp.dot(wgt_ref[...], g2.astype(BF16), preferred_element_type=F32) + bglu_ref[...]
    o5 = (g2 * szt_ref[...].astype(F32) * (jnp.tanh(half_gate) + 1.0)).T.astype(BF16)
    crow = lax.broadcasted_iota(jnp.int32, (cb, 1), 0) % row_chunks
    sp = sprev_ref[...].astype(F32)
    sn = snext_ref[...].astype(F32)
    sp_wrap = jnp.where(crow != 0, pltpu.roll(sp, 1, 0), 0.0)
    sn_wrap = jnp.where(crow != row_chunks - 1, pltpu.roll(sn, cb - 1, 0), 0.0)
    sp = jnp.where(tt == 0, sp_wrap, sp).astype(BF16)
    sn = jnp.where(tt == ntt - 1, sn_wrap, sn).astype(BF16)
    s_all = jnp.concatenate([sp, s_ref[...], sn], axis=0)
    r = cb * tb
    cw = cw_ref[...].astype(BF16)
    conv = (cw[0:1, :] * s_all[0:r] + cw[1:2, :] * s_all[cb:cb + r]
            + cw[2:3, :] * s_all[2 * cb:2 * cb + r] + cb_ref[...].astype(BF16))
    ocv = bg_ref[...] * conv * szc_ref[...]
    sub = jnp.dot(jnp.concatenate([o5, ocv], axis=1), wo_ref[...], preferred_element_type=F32)
    gate_vec = mod_ref[2:3, :] * (1.0 / alpha)
    for t in range(tb):
        v = xbuf[slot, t] + gate_vec * sub[t * cb:(t + 1) * cb]
        obuf[slot, t] = _layernorm_rows(v, LN_EPS / (alpha * alpha)) * lng_ref[...] + lnb_ref[...]
    put = functools.partial(_plane_copies, o_hbm, obuf, osem, cb=cb, tb=tb, ntt=ntt, to_hbm=True)

    @pl.when(step > 0)
    def _():
        for cp in put(step - 1, 1 - slot):
            cp.wait()

    for cp in put(step, slot):
        cp.start()

    @pl.when(step == nsteps - 1)
    def _():
        for cp in put(step, slot):
            cp.wait()


def _outproj(x3, mod, yt, szt, s, bg, szc, wo, wts, layer, mod_idx, cb, tb, row_chunks, alpha):
    nc = x3.shape[0]
    nb, ntt = nc // cb, T // tb
    r = cb * tb
    col = lambda b, tt: (0, b * ntt + tt)
    row = lambda b, tt: (b * ntt + tt, 0)
    prev_plane = lambda b, tt: (b * T + (tt * tb + T - 1) % T, 0)
    next_plane = lambda b, tt: (b * T + (tt * tb + tb) % T, 0)
    wgt, bglu, cw, cbias, lng, lnb = wts
    planes = pltpu.VMEM((2, tb, cb, D), F32)
    return pl.pallas_call(
        functools.partial(_outproj_kernel, cb=cb, tb=tb, nb=nb, ntt=ntt, row_chunks=row_chunks, alpha=alpha),
        grid=(nb, ntt),
        in_specs=[pl.BlockSpec(memory_space=pl.ANY),
                  _layer_spec((3, D), mod_idx, 0, 0),
                  pl.BlockSpec((DS, r), col), pl.BlockSpec((DS, r), col),
                  pl.BlockSpec((r, DS), row),
                  pl.BlockSpec((cb, DS), prev_plane), pl.BlockSpec((cb, DS), next_plane),
                  pl.BlockSpec((r, DS), row), pl.BlockSpec((r, DS), row),
                  _layer_spec((DS, DS), layer, 0, 0), _layer_spec((DS, 1), layer, 0, 0),
                  _layer_spec((3, DS), layer, 0, 0), _layer_spec((1, DS), layer, 0, 0),
                  _layer_spec((2 * DS, D), 0, 0, 0),
                  _layer_spec((1, D), layer, 0, 0), _layer_spec((1, D), layer, 0, 0)],
        out_specs=pl.BlockSpec(memory_space=pl.ANY),
        out_shape=jax.ShapeDtypeStruct(x3.shape, F32),
        scratch_shapes=[planes, pltpu.SemaphoreType.DMA((2, tb)), planes, pltpu.SemaphoreType.DMA((2, tb))],
        compiler_params=_cparams(("arbitrary", "arbitrary")),
        name="outproj",
    )(x3, mod, yt, szt, s, s, s, bg, szc, wgt, bglu, cw, cbias, wo, lng, lnb)


def kernel(x, c, ctx, c_ctx, w_ada, b_ada, w_in, s5_a_re, s5_a_im, s5_log_dt, s5_b_re, s5_b_im,
           s5_c_re, s5_c_im, s5_d, w_glu, b_glu, conv_w, conv_b, w_out, ln_g, ln_b):
    bsz, n, d = x.shape
    n_ctx = ctx.shape[1]
    depth = w_in.shape[0]
    assert bsz == 1 and d == D and n == NCL * T and n_ctx == NCC * T
    alpha = (2.0 * depth) ** 0.25

    cond = jnp.pad(jnp.stack([c[0], c_ctx]), ((0, COND_ROWS - 2), (0, 0)))
    w1t, w2, acoef, w_row, w_t, w_glu_t, mod = _s5prep(s5_a_re, s5_a_im, s5_log_dt, s5_b_re, s5_b_im,
                                                       s5_c_re, s5_c_im, s5_d, w_in, w_glu, cond, w_ada, b_ada)
    mod = mod.reshape(depth * 2, 3, D)
    out_w = (w_glu_t, GLU_B_SCALE * b_glu.reshape(depth, DS, 1), conv_w, conv_b.reshape(depth, 1, DS),
             ln_g.reshape(depth, 1, D), ln_b.reshape(depth, 1, D))

    x3 = x.reshape(n // T, T, D)
    c3 = ctx.reshape(n_ctx // T, T, D)
    tb = 4
    for i in range(depth):
        last = i == depth - 1
        ut, szt, s, bg, szc, *w_next = _inproj(x3, mod, w_t, w_row, 2 * i, CBL, tb,
                                               w_next=None if last else w_in, next_layer=i + 1)
        utc2, *ctx_rest = _inproj(c3, mod, w_t, w_row, 2 * i + 1, NCC, T, u_only=last)
        utc = jnp.pad(utc2.reshape(DS, T, NCC).transpose(1, 0, 2),
                      ((0, 0), (0, 0), (0, NCP - NCT)))
        yt, ytc, wo = _s5(ut, utc, w1t, w2, acoef, w_out, i, ctx_out=not last)
        x3 = _outproj(x3, mod, yt, szt, s, bg, szc, wo, out_w, i, 2 * i, CBL, tb, GRID_W // T, alpha)
        if not last:
            ytc2 = ytc[:, :, 0:NCC].transpose(1, 0, 2).reshape(DS, n_ctx)
            c3 = _outproj(c3, mod, ytc2, *ctx_rest, wo, out_w, i, 2 * i + 1, NCC, T, NCC, alpha)
            w_row, w_t = w_next
    return x3.reshape(bsz, n, D)
```

```python
import functools
import math

import jax
import jax.numpy as jnp
from jax import lax
from jax.experimental import pallas as pl
from jax.experimental.pallas import tpu as pltpu

D = 2048
DS = 1024
G = 64
H = 16
P = 64
T = 16
TH = T * H
GRID_W = 64
LN_EPS = 1e-6
LANES = 128
COND_ROWS = 16
VMEM_LIMIT = 60 * 1024 * 1024

F32 = jnp.float32
BF16 = jnp.bfloat16
NT_DIMS = (((1,), (1,)), ((), ()))
TN_DIMS = (((0,), (0,)), ((), ()))


def _cparams(sem):
    return pltpu.CompilerParams(dimension_semantics=sem, vmem_limit_bytes=VMEM_LIMIT)


def _layer_spec(shape, *idx):
    return pl.BlockSpec((None,) + shape, lambda *_: idx, pipeline_mode=pl.Buffered(1))


GELU_C1 = math.sqrt(2.0 / math.pi)
GELU_C2 = 0.044715 * GELU_C1
GLU_W_SCALE = 0.25
GLU_B_SCALE = 0.5
O5_SCALE = 0.25


def _silu_of_twice(h):
    return h * jnp.tanh(h) + h


def _silu(v):
    return _silu_of_twice(0.5 * v)


def _cast_in_weight(wn, wt_ref, wrow_ref):
    wt_ref[0:DS, :] = wn[:, 0:DS].T.astype(BF16)
    wt_ref[DS:2 * DS, :] = (0.5 * wn[:, DS:2 * DS]).T.astype(BF16)
    wrow_ref[:, 0:3 * DS] = wn[:, 2 * DS:5 * DS].astype(BF16)
    wrow_ref[:, 3 * DS:4 * DS] = (0.5 * wn[:, 5 * DS:6 * DS]).astype(BF16)


def _adaln_block(cond_ref, w_ref, b_ref, o_ref, s_scr, first):
    @pl.when(first)
    def _():
        s_scr[...] = _silu(cond_ref[...]).astype(BF16)

    mod = jnp.dot(s_scr[...], w_ref[...].astype(BF16), preferred_element_type=F32)
    o_ref[...] = mod[0:2, :] + b_ref[...]


GP = 8
PRM_ROWS = 40


def _s5prep_kernel(prm_ref, bre_ref, bim_ref, d_ref, win_ref, wglu_ref, cond_ref, wada_ref, bada_ref,
                   w1t_ref, w2_ref, acoef_ref, wrow_ref, wt_ref, wgt_ref, mod_ref, kt_scr, s_scr):
    first = jnp.logical_and(pl.program_id(0) == 0, pl.program_id(1) == 0)
    _adaln_block(cond_ref, wada_ref, bada_ref, mod_ref, s_scr, first)
    _cast_in_weight(win_ref[...], wt_ref, wrow_ref)
    wgt_ref[...] = (GLU_W_SCALE * wglu_ref[...].T).astype(BF16)
    lane = lax.broadcasted_iota(jnp.int32, (1, LANES), 1)
    fwd_lane = lane < P
    krow = lax.broadcasted_iota(jnp.int32, (T, 1), 0)
    fwd_row = lax.broadcasted_iota(jnp.int32, (2 * P, 1), 0) < P
    col256 = lax.broadcasted_iota(jnp.int32, (T, TH), 1)
    r16 = lax.broadcasted_iota(jnp.int32, (T, TH), 0)
    diag_mask = (col256 % H == r16).astype(F32)
    e_rep = diag_mask.astype(BF16)
    e_blk = (col256 // H == r16).astype(BF16)

    def rows16(tab):
        return jnp.broadcast_to(tab[:, None, :], (T, H, tab.shape[1])).reshape(TH, tab.shape[1])

    def tile16(mat):
        return jnp.broadcast_to(mat[None], (T, H, mat.shape[1])).reshape(TH, mat.shape[1])

    def bdot(a, b):
        return jnp.dot(a.astype(BF16), b.astype(BF16), preferred_element_type=F32)

    def expand(tab, e):
        return lax.dot_general(tab.astype(BF16), e, TN_DIMS, preferred_element_type=F32)

    for gi in range(GP):
        ar, ai = prm_ref[0, gi, 0:1, :], prm_ref[0, gi, 1:2, :]
        dt = jnp.exp(prm_ref[0, gi, 2:3, :])
        la, th = ar * dt, ai * dt
        cr = tile16(prm_ref[0, gi, 8:24, :].astype(BF16))
        ci = tile16(prm_ref[0, gi, 24:40, :].astype(BF16))
        b_x = [bdot(jnp.concatenate([b_ref[0, 0, gi], b_ref[0, 1, gi]], axis=0), e_rep)
               for b_ref in (bre_ref, bim_ref)]

        mag1 = jnp.exp(la)
        a1r, a1i = mag1 * jnp.cos(th), mag1 * jnp.sin(th)
        squares = [(a1r, a1i)]
        for _ in range(4):
            sr, si = squares[-1]
            squares.append((sr * sr - si * si, 2.0 * sr * si))

        zr, zi = a1r - 1.0, a1i
        den = ar * ar + ai * ai
        fr, fi = (zr * ar + zi * ai) / den, (zi * ar - zr * ai) / den

        def powrow(kf, kb, with_f):
            k = jnp.where(fwd_lane, kf, kb)
            if with_f:
                pr, pi = jnp.broadcast_to(fr, k.shape), jnp.broadcast_to(fi, k.shape)
            else:
                pr, pi = jnp.ones(k.shape, F32), jnp.zeros(k.shape, F32)
            for b, (sr, si) in enumerate(squares):
                bit = ((k >> b) & 1) == 1
                pr, pi = jnp.where(bit, pr * sr - pi * si, pr), jnp.where(bit, pr * si + pi * sr, pi)
            return pr, pi

        def ctab(kf, kb, with_f):
            pr, pi = powrow(kf, kb, with_f)
            pr, pi = rows16(pr.astype(BF16)), rows16(pi.astype(BF16))
            return cr * pr - ci * pi, cr * pi + ci * pr

        bbr_x, bbi_x = b_x
        ckr, cki = ctab(krow, (T - 1) - krow, True)
        br, bi = bbr_x.astype(BF16), (-bbi_x).astype(BF16)
        zero = jnp.zeros_like(br)
        rhs = jnp.concatenate(
            [jnp.concatenate([jnp.where(fwd_row, b, zero), jnp.where(fwd_row, zero, b)], axis=1) for b in (br, bi)],
            axis=0)
        kfb = jnp.dot(jnp.concatenate([ckr, cki], axis=1), rhs, preferred_element_type=F32)
        kf, kb = kfb[:, 0:TH], kfb[:, TH:2 * TH]
        kt_scr[0:TH - H, :] = kb[0:TH - H]
        kt_scr[TH - H:TH, :] = kb[TH - H:TH] + kf[0:H] + diag_mask * d_ref[0, gi:gi + 1, :]
        kt_scr[TH:2 * TH - H, :] = kf[H:TH]
        for r0 in range(0, TH, LANES):
            for c0 in range(0, TH, LANES):
                blk = (lax.broadcasted_iota(jnp.int32, (LANES, LANES), 1) + c0) // H
                m = jnp.zeros((LANES, LANES), F32)
                for s in range(c0 // H, (c0 + LANES) // H):
                    w0 = (T - 1 - s) * H + r0
                    m = jnp.where(blk == s, kt_scr[w0:w0 + LANES, c0:c0 + LANES], m)
                w1t_ref[0, gi, r0:r0 + LANES, c0:c0 + LANES] = m.astype(BF16)

        pir, pii = powrow((T - 1) - krow, krow, True)
        pxr, pxi = expand(pir, e_blk), expand(pii, e_blk)
        w1t_ref[0, gi, TH:TH + 2 * P, :] = (pxr * bbr_x - pxi * bbi_x).astype(BF16)
        w1t_ref[0, gi, TH + 2 * P:2 * TH, :] = (pxr * bbi_x + pxi * bbr_x).astype(BF16)

        cor, coi = ctab(krow + 1, T - krow, False)
        w2_ref[0, gi, :, 0:LANES] = cor.astype(BF16)
        w2_ref[0, gi, :, LANES:2 * LANES] = (-coi).astype(BF16)

        acoef_ref[0, 0, gi:gi + 1, :] = squares[4][0]
        acoef_ref[0, 1, gi:gi + 1, :] = squares[4][1]


def _s5prep(a_re, a_im, log_dt, b_re, b_im, c_re, c_im, d_skip, w_in, w_glu, cond, w_ada, b_ada):
    depth = a_re.shape[0]
    nsteps = depth * (G // GP)
    wr = D // nsteps
    gr = DS // (G // GP)
    n3 = w_ada.shape[2]
    mc = n3 // (G // GP)
    step = lambda l, g: l * (G // GP) + g
    d_cols = jnp.tile(d_skip.reshape(depth, G, H), (1, 1, T))

    def dircat(v):
        return jnp.concatenate([v[:, 0], v[:, 1]], axis=-1)

    ld = jnp.broadcast_to(log_dt[..., None, None], log_dt.shape + (1, P))
    pad = jnp.zeros((depth, G, 5, 2 * P), F32)
    prm = jnp.concatenate([dircat(a_re[:, :, :, None, :]), dircat(a_im[:, :, :, None, :]), dircat(ld), pad,
                           dircat(c_re), dircat(c_im)], axis=2)
    b_spec = pl.BlockSpec((1, 2, GP, P, H), lambda l, g: (l, 0, g, 0, 0))
    return pl.pallas_call(
        _s5prep_kernel,
        grid=(depth, G // GP),
        in_specs=[pl.BlockSpec((1, GP, PRM_ROWS, 2 * P), lambda l, g: (l, g, 0, 0)), b_spec, b_spec,
                  pl.BlockSpec((1, GP, TH), lambda l, g: (l, g, 0)),
                  pl.BlockSpec((None, wr, 6 * DS), lambda l, g: (0, step(l, g), 0)),
                  pl.BlockSpec((None, gr, DS), lambda l, g: (l, g, 0)),
                  pl.BlockSpec((COND_ROWS, D), lambda l, g: (0, 0)),
                  pl.BlockSpec((None, D, mc), lambda l, g: (l, 0, g)),
                  pl.BlockSpec((None, 1, mc), lambda l, g: (l, 0, g))],
        out_specs=[pl.BlockSpec((1, GP, 2 * TH, TH), lambda l, g: (l, g, 0, 0)),
                   pl.BlockSpec((1, GP, TH, TH), lambda l, g: (l, g, 0, 0)),
                   pl.BlockSpec((1, 2, GP, 2 * P), lambda l, g: (l, 0, g, 0)),
                   pl.BlockSpec((None, wr, 4 * DS), lambda l, g: (0, step(l, g), 0)),
                   pl.BlockSpec((None, 2 * DS, wr), lambda l, g: (0, 0, step(l, g))),
                   pl.BlockSpec((None, DS, gr), lambda l, g: (l, 0, g)),
                   pl.BlockSpec((None, 2, mc), lambda l, g: (l, 0, g))],
        out_shape=[jax.ShapeDtypeStruct((depth, G, 2 * TH, TH), BF16),
                   jax.ShapeDtypeStruct((depth, G, TH, TH), BF16),
                   jax.ShapeDtypeStruct((depth, 2, G, 2 * P), F32),
                   jax.ShapeDtypeStruct((1, D, 4 * DS), BF16),
                   jax.ShapeDtypeStruct((1, 2 * DS, D), BF16),
                   jax.ShapeDtypeStruct((depth, DS, DS), BF16),
                   jax.ShapeDtypeStruct((depth, 2, n3), F32)],
        scratch_shapes=[pltpu.VMEM((2 * TH, TH), F32), pltpu.VMEM((COND_ROWS, D), BF16)],
        compiler_params=_cparams(("arbitrary", "arbitrary")),
        name="prep",
    )(prm, b_re, b_im, d_cols, w_in, w_glu, cond, w_ada, b_ada.reshape(depth, 1, n3))


def _plane_copies(hbm3, buf, sem, step, slot, *, cb, tb, ntt, to_hbm):
    b, tt = step // ntt, step % ntt
    copies = []
    for t in range(tb):
        hbm = hbm3.at[pl.ds(b * cb, cb), tt * tb + t, :]
        vm = buf.at[slot, t]
        copies.append(pltpu.make_async_copy(vm, hbm, sem.at[slot, t]) if to_hbm
                      else pltpu.make_async_copy(hbm, vm, sem.at[slot, t]))
    return copies


def _fetch_planes(x_hbm, xbuf, xsem, step, nsteps, **kw):
    slot = step % 2
    mk = functools.partial(_plane_copies, x_hbm, xbuf, xsem, to_hbm=False, **kw)

    @pl.when(step == 0)
    def _():
        for cp in mk(step, slot):
            cp.start()

    @pl.when(step + 1 < nsteps)
    def _():
        for cp in mk(step + 1, 1 - slot):
            cp.start()

    for cp in mk(step, slot):
        cp.wait()
    return slot


def _layernorm_rows(v, eps=LN_EPS):
    mu = jnp.mean(v, axis=-1, keepdims=True)
    vc = v - mu
    var = jnp.mean(vc * vc, axis=-1, keepdims=True)
    return vc * lax.rsqrt(var + eps)


def _inproj_kernel(*refs, cb, tb, nb, ntt, u_only, cast_next):
    if u_only:
        x_hbm, mod_ref, wut_ref, ut_ref, xbuf, xsem, h_scr = refs
    elif cast_next:
        (x_hbm, mod_ref, wut_ref, wzt_ref, wv_ref, wb_ref, wc_ref, wzc_ref, wnext_ref,
         ut_ref, szt_ref, s_ref, bg_ref, szc_ref, wrow_next_ref, wt_next_ref, xbuf, xsem, h_scr) = refs
        _cast_in_weight(wnext_ref[...], wt_next_ref, wrow_next_ref)
    else:
        (x_hbm, mod_ref, wut_ref, wzt_ref, wv_ref, wb_ref, wc_ref, wzc_ref,
         ut_ref, szt_ref, s_ref, bg_ref, szc_ref, xbuf, xsem, h_scr) = refs
    step = pl.program_id(0) * ntt + pl.program_id(1)
    slot = _fetch_planes(x_hbm, xbuf, xsem, step, nb * ntt, cb=cb, tb=tb, ntt=ntt)
    shift, scale = mod_ref[0:1, :], mod_ref[1:2, :]
    for t in range(tb):
        hn = _layernorm_rows(xbuf[slot, t]) * (1.0 + scale) + shift
        h_scr[t * cb:(t + 1) * cb, :] = hn.astype(BF16)
    h = h_scr[...]
    ut_ref[...] = lax.dot_general(wut_ref[...], h, NT_DIMS, preferred_element_type=F32).astype(BF16)
    if u_only:
        return
    szt_ref[...] = _silu_of_twice(lax.dot_general(wzt_ref[...], h, NT_DIMS,
                                                  preferred_element_type=F32)).astype(BF16)
    v = jnp.dot(h, wv_ref[...], preferred_element_type=F32)
    cg = jnp.dot(h, wc_ref[...], preferred_element_type=F32)
    s_ref[...] = (cg * v).astype(BF16)
    bg_ref[...] = jnp.dot(h, wb_ref[...], preferred_element_type=F32).astype(BF16)
    szc_ref[...] = _silu_of_twice(jnp.dot(h, wzc_ref[...], preferred_element_type=F32)).astype(BF16)


def _inproj(x3, mod, wt, wrow, mod_idx, cb, tb, u_only=False, w_next=None, next_layer=0):
    nc = x3.shape[0]
    n = nc * T
    nb, ntt = nc // cb, T // tb
    r = cb * tb
    col = lambda b, tt: (0, b * ntt + tt)
    row = lambda b, tt: (b * ntt + tt, 0)
    in_specs = [pl.BlockSpec(memory_space=pl.ANY), _layer_spec((3, D), mod_idx, 0, 0),
                _layer_spec((DS, D), 0, 0, 0)]
    out_specs = [pl.BlockSpec((DS, r), col)]
    out_shape = [jax.ShapeDtypeStruct((DS, n), BF16)]
    args = [x3, mod, wt]
    if not u_only:
        in_specs += [_layer_spec((DS, D), 0, 1, 0)] + [_layer_spec((D, DS), 0, 0, k) for k in range(4)]
        out_specs += [pl.BlockSpec((DS, r), col)] + [pl.BlockSpec((r, DS), row)] * 3
        out_shape += [jax.ShapeDtypeStruct((DS, n), BF16)] + [jax.ShapeDtypeStruct((n, DS), BF16)] * 3
        args += [wt, wrow, wrow, wrow, wrow]
    if w_next is not None:
        wr = D // (nb * ntt)
        in_specs += [pl.BlockSpec((None, wr, 6 * DS), lambda b, tt: (next_layer, b * ntt + tt, 0))]
        out_specs += [pl.BlockSpec((None, wr, 4 * DS), lambda b, tt: (0, b * ntt + tt, 0)),
                      pl.BlockSpec((None, 2 * DS, wr), lambda b, tt: (0, 0, b * ntt + tt))]
        out_shape += [jax.ShapeDtypeStruct((1, D, 4 * DS), BF16), jax.ShapeDtypeStruct((1, 2 * DS, D), BF16)]
        args += [w_next]
    return pl.pallas_call(
        functools.partial(_inproj_kernel, cb=cb, tb=tb, nb=nb, ntt=ntt, u_only=u_only,
                          cast_next=w_next is not None),
        grid=(nb, ntt),
        in_specs=in_specs,
        out_specs=out_specs,
        out_shape=out_shape,
        scratch_shapes=[pltpu.VMEM((2, tb, cb, D), F32), pltpu.SemaphoreType.DMA((2, tb)),
                        pltpu.VMEM((r, D), BF16)],
        compiler_params=_cparams(("arbitrary", "arbitrary")),
        name="inproj_u" if u_only else "inproj",
    )(*args)


GB = 8
NCL = 512
NCC = 16
NCT = NCL + NCC
NCP = 640
CBL = 128


def _s5_kernel(ut_ref, utc_ref, w1t_ref, w2_ref, acoef_ref, wo_ref, yt_ref, ytc_ref, wo_bf_ref,
               sre_scr, sim_scr, hre_scr, him_scr, yloc_scr, *, ctx_out):
    nbl = NCL // CBL
    s5_rows = pl.program_id(0) < DS // wo_ref.shape[0]
    wo_bf_ref[...] = (jnp.where(s5_rows, O5_SCALE, 1.0) * wo_ref[...]).astype(BF16)
    for gl in range(GB):
        rows = slice(gl * H, (gl + 1) * H)
        planes = []
        for t in range(T):
            lat = [ut_ref[rows, (b * T + t) * CBL:(b * T + t + 1) * CBL] for b in range(nbl)]
            planes.append(jnp.concatenate(lat + [utc_ref[t, rows, :]], axis=1))
        z = jnp.concatenate(planes, axis=0)
        r1 = jnp.dot(w1t_ref[gl], z, preferred_element_type=F32)
        yloc_scr[gl] = r1[0:TH]
        st = r1[TH:2 * TH].T
        sre_scr[pl.ds(gl, NCP, stride=GB), :] = st[:, 0:LANES]
        sim_scr[pl.ds(gl, NCP, stride=GB), :] = st[:, LANES:2 * LANES]

    ar, ai = acoef_ref[0], acoef_ref[1]
    fwd_lane = lax.broadcasted_iota(jnp.int32, (GB, LANES), 1) < P
    bwd_lane = jnp.logical_not(fwd_lane)
    hre_scr[NCT * GB:NCP * GB, :] = jnp.zeros(((NCP - NCT) * GB, LANES), F32)
    him_scr[NCT * GB:NCP * GB, :] = jnp.zeros(((NCP - NCT) * GB, LANES), F32)

    def step(k, carry, fwd_shift):
        hr, hi = carry
        cf = k + fwd_shift
        cb_ = NCT - 1 - k
        rf = pl.ds(pl.multiple_of(cf * GB, GB), GB)
        rb = pl.ds(pl.multiple_of(cb_ * GB, GB), GB)
        pltpu.store(hre_scr.at[rf, :], hr, mask=fwd_lane)
        pltpu.store(hre_scr.at[rb, :], hr, mask=bwd_lane)
        pltpu.store(him_scr.at[rf, :], hi, mask=fwd_lane)
        pltpu.store(him_scr.at[rb, :], hi, mask=bwd_lane)
        sr = jnp.where(fwd_lane, sre_scr[rf, :], sre_scr[rb, :])
        si = jnp.where(fwd_lane, sim_scr[rf, :], sim_scr[rb, :])
        return ar * hr - ai * hi + sr, ar * hi + ai * hr + si

    zero = jnp.zeros((GB, LANES), F32)
    carry = lax.fori_loop(0, NCC, functools.partial(step, fwd_shift=NCL), (zero, zero))
    lax.fori_loop(NCC, NCT, functools.partial(step, fwd_shift=-NCC), carry, unroll=4)

    for gl in range(GB):
        rows = slice(gl * H, (gl + 1) * H)
        sel = pl.ds(gl, NCP, stride=GB)
        hp = jnp.concatenate([hre_scr[sel, :], him_scr[sel, :]], axis=1)
        y = yloc_scr[gl] + lax.dot_general(w2_ref[gl], hp.astype(BF16), NT_DIMS, preferred_element_type=F32)
        yb = y.astype(BF16)
        for t in range(T):
            for b in range(nbl):
                yt_ref[rows, (b * T + t) * CBL:(b * T + t + 1) * CBL] = \
                    yb[t * H:(t + 1) * H, b * CBL:(b + 1) * CBL]
            if ctx_out:
                ytc_ref[t, rows, :] = yb[t * H:(t + 1) * H, NCL:NCP]
    if not ctx_out:
        ytc_ref[...] = jnp.zeros(ytc_ref.shape, BF16)


def _s5(ut, utc, w1t, w2, acoef, w_out, layer, ctx_out):
    n = ut.shape[1]
    ncx = NCP - NCL
    wr = 2 * DS // (G // GB)
    state = pltpu.VMEM((NCP * GB, LANES), F32)
    return pl.pallas_call(
        functools.partial(_s5_kernel, ctx_out=ctx_out),
        grid=(G // GB,),
        in_specs=[pl.BlockSpec((GB * H, n), lambda i: (i, 0)),
                  pl.BlockSpec((T, GB * H, ncx), lambda i: (0, i, 0)),
                  pl.BlockSpec((None, GB, 2 * TH, TH), lambda i: (layer, i, 0, 0)),
                  pl.BlockSpec((None, GB, TH, TH), lambda i: (layer, i, 0, 0)),
                  pl.BlockSpec((None, 2, GB, LANES), lambda i: (layer, 0, i, 0)),
                  pl.BlockSpec((None, wr, D), lambda i: (layer, i, 0))],
        out_specs=[pl.BlockSpec((GB * H, n), lambda i: (i, 0)),
                   pl.BlockSpec((T, GB * H, ncx), lambda i: (0, i, 0)),
                   pl.BlockSpec((None, wr, D), lambda i: (0, i, 0))],
        out_shape=[jax.ShapeDtypeStruct((DS, n), BF16), jax.ShapeDtypeStruct((T, DS, ncx), BF16),
                   jax.ShapeDtypeStruct((1, 2 * DS, D), BF16)],
        scratch_shapes=[state, state, state, state, pltpu.VMEM((GB, TH, NCP), F32)],
        compiler_params=_cparams(("parallel",)),
        name="s5",
    )(ut, utc, w1t, w2, acoef, w_out)


def _outproj_kernel(x_hbm, mod_ref, yt_ref, szt_ref, s_ref, sprev_ref, snext_ref, bg_ref, szc_ref,
                    wgt_ref, bglu_ref, cw_ref, cb_ref, wo_ref, lng_ref, lnb_ref,
                    o_hbm, xbuf, xsem, obuf, osem, *, cb, tb, nb, ntt, row_chunks, alpha):
    tt = pl.program_id(1)
    step = pl.program_id(0) * ntt + tt
    nsteps = nb * ntt
    slot = _fetch_planes(x_hbm, xbuf, xsem, step, nsteps, cb=cb, tb=tb, ntt=ntt)
    y = yt_ref[...].astype(F32)
    g2 = y * jnp.tanh(y * (GELU_C1 + GELU_C2 * (y * y))) + y
    half_gate = jnp.dot(wgt_ref[...], g2.astype(BF16), preferred_element_type=F32) + bglu_ref[...]
    o5 = (g2 * szt_ref[...].astype(F32) * (jnp.tanh(half_gate) + 1.0)).T.astype(BF16)
    crow = lax.broadcasted_iota(jnp.int32, (cb, 1), 0) % row_chunks
    sp = sprev_ref[...].astype(F32)
    sn = snext_ref[...].astype(F32)
    sp_wrap = jnp.where(crow != 0, pltpu.roll(sp, 1, 0), 0.0)
    sn_wrap = jnp.where(crow != row_chunks - 1, pltpu.roll(sn, cb - 1, 0), 0.0)
    sp = jnp.where(tt == 0, sp_wrap, sp).astype(BF16)
    sn = jnp.where(tt == ntt - 1, sn_wrap, sn).astype(BF16)
    s_all = jnp.concatenate([sp, s_ref[...], sn], axis=0)
    r = cb * tb
    cw = cw_ref[...].astype(BF16)
    conv = (cw[0:1, :] * s_all[0:r] + cw[1:2, :] * s_all[cb:cb + r]
            + cw[2:3, :] * s_all[2 * cb:2 * cb + r] + cb_ref[...].astype(BF16))
    ocv = bg_ref[...] * conv * szc_ref[...]
    sub = jnp.dot(jnp.concatenate([o5, ocv], axis=1), wo_ref[...], preferred_element_type=F32)
    gate_vec = mod_ref[2:3, :] * (1.0 / alpha)
    for t in range(tb):
        v = xbuf[slot, t] + gate_vec * sub[t * cb:(t + 1) * cb]
        obuf[slot, t] = _layernorm_rows(v, LN_EPS / (alpha * alpha)) * lng_ref[...] + lnb_ref[...]
    put = functools.partial(_plane_copies, o_hbm, obuf, osem, cb=cb, tb=tb, ntt=ntt, to_hbm=True)

    @pl.when(step > 0)
    def _():
        for cp in put(step - 1, 1 - slot):
            cp.wait()

    for cp in put(step, slot):
        cp.start()

    @pl.when(step == nsteps - 1)
    def _():
        for cp in put(step, slot):
            cp.wait()


def _outproj(x3, mod, yt, szt, s, bg, szc, wo, wts, layer, mod_idx, cb, tb, row_chunks, alpha):
    nc = x3.shape[0]
    nb, ntt = nc // cb, T // tb
    r = cb * tb
    col = lambda b, tt: (0, b * ntt + tt)
    row = lambda b, tt: (b * ntt + tt, 0)
    prev_plane = lambda b, tt: (b * T + (tt * tb + T - 1) % T, 0)
    next_plane = lambda b, tt: (b * T + (tt * tb + tb) % T, 0)
    wgt, bglu, cw, cbias, lng, lnb = wts
    planes = pltpu.VMEM((2, tb, cb, D), F32)
    return pl.pallas_call(
        functools.partial(_outproj_kernel, cb=cb, tb=tb, nb=nb, ntt=ntt, row_chunks=row_chunks, alpha=alpha),
        grid=(nb, ntt),
        in_specs=[pl.BlockSpec(memory_space=pl.ANY),
                  _layer_spec((3, D), mod_idx, 0, 0),
                  pl.BlockSpec((DS, r), col), pl.BlockSpec((DS, r), col),
                  pl.BlockSpec((r, DS), row),
                  pl.BlockSpec((cb, DS), prev_plane), pl.BlockSpec((cb, DS), next_plane),
                  pl.BlockSpec((r, DS), row), pl.BlockSpec((r, DS), row),
                  _layer_spec((DS, DS), layer, 0, 0), _layer_spec((DS, 1), layer, 0, 0),
                  _layer_spec((3, DS), layer, 0, 0), _layer_spec((1, DS), layer, 0, 0),
                  _layer_spec((2 * DS, D), 0, 0, 0),
                  _layer_spec((1, D), layer, 0, 0), _layer_spec((1, D), layer, 0, 0)],
        out_specs=pl.BlockSpec(memory_space=pl.ANY),
        out_shape=jax.ShapeDtypeStruct(x3.shape, F32),
        scratch_shapes=[planes, pltpu.SemaphoreType.DMA((2, tb)), planes, pltpu.SemaphoreType.DMA((2, tb))],
        compiler_params=_cparams(("arbitrary", "arbitrary")),
        name="outproj",
    )(x3, mod, yt, szt, s, s, s, bg, szc, wgt, bglu, cw, cbias, wo, lng, lnb)


def kernel(x, c, ctx, c_ctx, w_ada, b_ada, w_in, s5_a_re, s5_a_im, s5_log_dt, s5_b_re, s5_b_im,
           s5_c_re, s5_c_im, s5_d, w_glu, b_glu, conv_w, conv_b, w_out, ln_g, ln_b):
    bsz, n, d = x.shape
    n_ctx = ctx.shape[1]
    depth = w_in.shape[0]
    assert bsz == 1 and d == D and n == NCL * T and n_ctx == NCC * T
    alpha = (2.0 * depth) ** 0.25

    cond = jnp.pad(jnp.stack([c[0], c_ctx]), ((0, COND_ROWS - 2), (0, 0)))
    w1t, w2, acoef, w_row, w_t, w_glu_t, mod = _s5prep(s5_a_re, s5_a_im, s5_log_dt, s5_b_re, s5_b_im,
                                                       s5_c_re, s5_c_im, s5_d, w_in, w_glu, cond, w_ada, b_ada)
    mod = mod.reshape(depth * 2, 3, D)
    out_w = (w_glu_t, GLU_B_SCALE * b_glu.reshape(depth, DS, 1), conv_w, conv_b.reshape(depth, 1, DS),
             ln_g.reshape(depth, 1, D), ln_b.reshape(depth, 1, D))

    x3 = x.reshape(n // T, T, D)
    c3 = ctx.reshape(n_ctx // T, T, D)
    tb = 4
    for i in range(depth):
        last = i == depth - 1
        ut, szt, s, bg, szc, *w_next = _inproj(x3, mod, w_t, w_row, 2 * i, CBL, tb,
                                               w_next=None if last else w_in, next_layer=i + 1)
        utc2, *ctx_rest = _inproj(c3, mod, w_t, w_row, 2 * i + 1, NCC, T, u_only=last)
        utc = jnp.pad(utc2.reshape(DS, T, NCC).transpose(1, 0, 2),
                      ((0, 0), (0, 0), (0, NCP - NCT)))
        yt, ytc, wo = _s5(ut, utc, w1t, w2, acoef, w_out, i, ctx_out=not last)
        x3 = _outproj(x3, mod, yt, szt, s, bg, szc, wo, out_w, i, 2 * i, CBL, tb, GRID_W // T, alpha)
        if not last:
            ytc2 = ytc[:, :, 0:NCC].transpose(1, 0, 2).reshape(DS, n_ctx)
            c3 = _outproj(c3, mod, ytc2, *ctx_rest, wo, out_w, i, 2 * i + 1, NCC, T, NCC, alpha)
            w_row, w_t = w_next
    return x3.reshape(bsz, n, D)
```

```python
import functools
import math

import jax
import jax.numpy as jnp
from jax import lax
from jax.experimental import pallas as pl
from jax.experimental.pallas import tpu as pltpu

D = 2048
DS = 1024
G = 64
H = 16
P = 64
T = 16
TH = T * H
GRID_W = 64
LN_EPS = 1e-6
LANES = 128
COND_ROWS = 16
VMEM_LIMIT = 60 * 1024 * 1024

F32 = jnp.float32
BF16 = jnp.bfloat16
NT_DIMS = (((1,), (1,)), ((), ()))
TN_DIMS = (((0,), (0,)), ((), ()))


def _cparams(sem):
    return pltpu.CompilerParams(dimension_semantics=sem, vmem_limit_bytes=VMEM_LIMIT)


def _layer_spec(shape, *idx):
    return pl.BlockSpec((None,) + shape, lambda *_: idx, pipeline_mode=pl.Buffered(1))


GELU_C1 = math.sqrt(2.0 / math.pi)
GELU_C2 = 0.044715 * GELU_C1
GLU_W_SCALE = 0.25
GLU_B_SCALE = 0.5
O5_SCALE = 0.25


def _silu_of_twice(h):
    return h * jnp.tanh(h) + h


def _silu(v):
    return _silu_of_twice(0.5 * v)


def _cast_in_weight(wn, wt_ref, wrow_ref):
    wt_ref[0:DS, :] = wn[:, 0:DS].T.astype(BF16)
    wt_ref[DS:2 * DS, :] = (0.5 * wn[:, DS:2 * DS]).T.astype(BF16)
    wrow_ref[:, 0:3 * DS] = wn[:, 2 * DS:5 * DS].astype(BF16)
    wrow_ref[:, 3 * DS:4 * DS] = (0.5 * wn[:, 5 * DS:6 * DS]).astype(BF16)


def _adaln_block(cond_ref, w_ref, b_ref, o_ref, s_scr, first):
    @pl.when(first)
    def _():
        s_scr[...] = _silu(cond_ref[...]).astype(BF16)

    mod = jnp.dot(s_scr[...], w_ref[...].astype(BF16), preferred_element_type=F32)
    o_ref[...] = mod[0:2, :] + b_ref[...]


GP = 8
PRM_ROWS = 40


def _s5prep_kernel(prm_ref, bre_ref, bim_ref, d_ref, win_ref, wglu_ref, cond_ref, wada_ref, bada_ref,
                   w1t_ref, w2_ref, acoef_ref, wrow_ref, wt_ref, wgt_ref, mod_ref, kt_scr, s_scr):
    first = jnp.logical_and(pl.program_id(0) == 0, pl.program_id(1) == 0)
    _adaln_block(cond_ref, wada_ref, bada_ref, mod_ref, s_scr, first)
    _cast_in_weight(win_ref[...], wt_ref, wrow_ref)
    wgt_ref[...] = (GLU_W_SCALE * wglu_ref[...].T).astype(BF16)
    lane = lax.broadcasted_iota(jnp.int32, (1, LANES), 1)
    fwd_lane = lane < P
    krow = lax.broadcasted_iota(jnp.int32, (T, 1), 0)
    fwd_row = lax.broadcasted_iota(jnp.int32, (2 * P, 1), 0) < P
    col256 = lax.broadcasted_iota(jnp.int32, (T, TH), 1)
    r16 = lax.broadcasted_iota(jnp.int32, (T, TH), 0)
    diag_mask = (col256 % H == r16).astype(F32)
    e_rep = diag_mask.astype(BF16)
    e_blk = (col256 // H == r16).astype(BF16)

    def rows16(tab):
        return jnp.broadcast_to(tab[:, None, :], (T, H, tab.shape[1])).reshape(TH, tab.shape[1])

    def tile16(mat):
        return jnp.broadcast_to(mat[None], (T, H, mat.shape[1])).reshape(TH, mat.shape[1])

    def bdot(a, b):
        return jnp.dot(a.astype(BF16), b.astype(BF16), preferred_element_type=F32)

    def expand(tab, e):
        return lax.dot_general(tab.astype(BF16), e, TN_DIMS, preferred_element_type=F32)

    for gi in range(GP):
        ar, ai = prm_ref[0, gi, 0:1, :], prm_ref[0, gi, 1:2, :]
        dt = jnp.exp(prm_ref[0, gi, 2:3, :])
        la, th = ar * dt, ai * dt
        cr = tile16(prm_ref[0, gi, 8:24, :].astype(BF16))
        ci = tile16(prm_ref[0, gi, 24:40, :].astype(BF16))
        b_x = [bdot(jnp.concatenate([b_ref[0, 0, gi], b_ref[0, 1, gi]], axis=0), e_rep)
               for b_ref in (bre_ref, bim_ref)]

        mag1 = jnp.exp(la)
        a1r, a1i = mag1 * jnp.cos(th), mag1 * jnp.sin(th)
        squares = [(a1r, a1i)]
        for _ in range(4):
            sr, si = squares[-1]
            squares.append((sr * sr - si * si, 2.0 * sr * si))

        zr, zi = a1r - 1.0, a1i
        den = ar * ar + ai * ai
        fr, fi = (zr * ar + zi * ai) / den, (zi * ar - zr * ai) / den

        def powrow(kf, kb, with_f):
            k = jnp.where(fwd_lane, kf, kb)
            if with_f:
                pr, pi = jnp.broadcast_to(fr, k.shape), jnp.broadcast_to(fi, k.shape)
            else:
                pr, pi = jnp.ones(k.shape, F32), jnp.zeros(k.shape, F32)
            for b, (sr, si) in enumerate(squares):
                bit = ((k >> b) & 1) == 1
                pr, pi = jnp.where(bit, pr * sr - pi * si, pr), jnp.where(bit, pr * si + pi * sr, pi)
            return pr, pi

        def ctab(kf, kb, with_f):
            pr, pi = powrow(kf, kb, with_f)
            pr, pi = rows16(pr.astype(BF16)), rows16(pi.astype(BF16))
            return cr * pr - ci * pi, cr * pi + ci * pr

        bbr_x, bbi_x = b_x
        ckr, cki = ctab(krow, (T - 1) - krow, True)
        br, bi = bbr_x.astype(BF16), (-bbi_x).astype(BF16)
        zero = jnp.zeros_like(br)
        rhs = jnp.concatenate(
            [jnp.concatenate([jnp.where(fwd_row, b, zero), jnp.where(fwd_row, zero, b)], axis=1) for b in (br, bi)],
            axis=0)
        kfb = jnp.dot(jnp.concatenate([ckr, cki], axis=1), rhs, preferred_element_type=F32)
        kf, kb = kfb[:, 0:TH], kfb[:, TH:2 * TH]
        kt_scr[0:TH - H, :] = kb[0:TH - H]
        kt_scr[TH - H:TH, :] = kb[TH - H:TH] + kf[0:H] + diag_mask * d_ref[0, gi:gi + 1, :]
        kt_scr[TH:2 * TH - H, :] = kf[H:TH]
        for r0 in range(0, TH, LANES):
            for c0 in range(0, TH, LANES):
                blk = (lax.broadcasted_iota(jnp.int32, (LANES, LANES), 1) + c0) // H
                m = jnp.zeros((LANES, LANES), F32)
                for s in range(c0 // H, (c0 + LANES) // H):
                    w0 = (T - 1 - s) * H + r0
                    m = jnp.where(blk == s, kt_scr[w0:w0 + LANES, c0:c0 + LANES], m)
                w1t_ref[0, gi, r0:r0 + LANES, c0:c0 + LANES] = m.astype(BF16)

        pir, pii = powrow((T - 1) - krow, krow, True)
        pxr, pxi = expand(pir, e_blk), expand(pii, e_blk)
        w1t_ref[0, gi, TH:TH + 2 * P, :] = (pxr * bbr_x - pxi * bbi_x).astype(BF16)
        w1t_ref[0, gi, TH + 2 * P:2 * TH, :] = (pxr * bbi_x + pxi * bbr_x).astype(BF16)

        cor, coi = ctab(krow + 1, T - krow, False)
        w2_ref[0, gi, :, 0:LANES] = cor.astype(BF16)
        w2_ref[0, gi, :, LANES:2 * LANES] = (-coi).astype(BF16)

        acoef_ref[0, 0, gi:gi + 1, :] = squares[4][0]
        acoef_ref[0, 1, gi:gi + 1, :] = squares[4][1]


def _s5prep(a_re, a_im, log_dt, b_re, b_im, c_re, c_im, d_skip, w_in, w_glu, cond, w_ada, b_ada):
    depth = a_re.shape[0]
    nsteps = depth * (G // GP)
    wr = D // nsteps
    gr = DS // (G // GP)
    n3 = w_ada.shape[2]
    mc = n3 // (G // GP)
    step = lambda l, g: l * (G // GP) + g
    d_cols = jnp.tile(d_skip.reshape(depth, G, H), (1, 1, T))

    def dircat(v):
        return jnp.concatenate([v[:, 0], v[:, 1]], axis=-1)

    ld = jnp.broadcast_to(log_dt[..., None, None], log_dt.shape + (1, P))
    pad = jnp.zeros((depth, G, 5, 2 * P), F32)
    prm = jnp.concatenate([dircat(a_re[:, :, :, None, :]), dircat(a_im[:, :, :, None, :]), dircat(ld), pad,
                           dircat(c_re), dircat(c_im)], axis=2)
    b_spec = pl.BlockSpec((1, 2, GP, P, H), lambda l, g: (l, 0, g, 0, 0))
    return pl.pallas_call(
        _s5prep_kernel,
        grid=(depth, G // GP),
        in_specs=[pl.BlockSpec((1, GP, PRM_ROWS, 2 * P), lambda l, g: (l, g, 0, 0)), b_spec, b_spec,
                  pl.BlockSpec((1, GP, TH), lambda l, g: (l, g, 0)),
                  pl.BlockSpec((None, wr, 6 * DS), lambda l, g: (0, step(l, g), 0)),
                  pl.BlockSpec((None, gr, DS), lambda l, g: (l, g, 0)),
                  pl.BlockSpec((COND_ROWS, D), lambda l, g: (0, 0)),
                  pl.BlockSpec((None, D, mc), lambda l, g: (l, 0, g)),
                  pl.BlockSpec((None, 1, mc), lambda l, g: (l, 0, g))],
        out_specs=[pl.BlockSpec((1, GP, 2 * TH, TH), lambda l, g: (l, g, 0, 0)),
                   pl.BlockSpec((1, GP, TH, TH), lambda l, g: (l, g, 0, 0)),
                   pl.BlockSpec((1, 2, GP, 2 * P), lambda l, g: (l, 0, g, 0)),
                   pl.BlockSpec((None, wr, 4 * DS), lambda l, g: (0, step(l, g), 0)),
                   pl.BlockSpec((None, 2 * DS, wr), lambda l, g: (0, 0, step(l, g))),
                   pl.BlockSpec((None, DS, gr), lambda l, g: (l, 0, g)),
                   pl.BlockSpec((None, 2, mc), lambda l, g: (l, 0, g))],
        out_shape=[jax.ShapeDtypeStruct((depth, G, 2 * TH, TH), BF16),
                   jax.ShapeDtypeStruct((depth, G, TH, TH), BF16),
                   jax.ShapeDtypeStruct((depth, 2, G, 2 * P), F32),
                   jax.ShapeDtypeStruct((1, D, 4 * DS), BF16),
                   jax.ShapeDtypeStruct((1, 2 * DS, D), BF16),
                   jax.ShapeDtypeStruct((depth, DS, DS), BF16),
                   jax.ShapeDtypeStruct((depth, 2, n3), F32)],
        scratch_shapes=[pltpu.VMEM((2 * TH, TH), F32), pltpu.VMEM((COND_ROWS, D), BF16)],
        compiler_params=_cparams(("arbitrary", "arbitrary")),
        name="prep",
    )(prm, b_re, b_im, d_cols, w_in, w_glu, cond, w_ada, b_ada.reshape(depth, 1, n3))


def _plane_copies(hbm3, buf, sem, step, slot, *, cb, tb, ntt, to_hbm):
    b, tt = step // ntt, step % ntt
    copies = []
    for t in range(tb):
        hbm = hbm3.at[pl.ds(b * cb, cb), tt * tb + t, :]
        vm = buf.at[slot, t]
        copies.append(pltpu.make_async_copy(vm, hbm, sem.at[slot, t]) if to_hbm
                      else pltpu.make_async_copy(hbm, vm, sem.at[slot, t]))
    return copies


def _fetch_planes(x_hbm, xbuf, xsem, step, nsteps, **kw):
    slot = step % 2
    mk = functools.partial(_plane_copies, x_hbm, xbuf, xsem, to_hbm=False, **kw)

    @pl.when(step == 0)
    def _():
        for cp in mk(step, slot):
            cp.start()

    @pl.when(step + 1 < nsteps)
    def _():
        for cp in mk(step + 1, 1 - slot):
            cp.start()

    for cp in mk(step, slot):
        cp.wait()
    return slot


def _layernorm_rows(v, eps=LN_EPS):
    mu = jnp.mean(v, axis=-1, keepdims=True)
    vc = v - mu
    var = jnp.mean(vc * vc, axis=-1, keepdims=True)
    return vc * lax.rsqrt(var + eps)


def _inproj_kernel(*refs, cb, tb, nb, ntt, u_only, cast_next):
    if u_only:
        x_hbm, mod_ref, wut_ref, ut_ref, xbuf, xsem, h_scr = refs
    elif cast_next:
        (x_hbm, mod_ref, wut_ref, wzt_ref, wv_ref, wb_ref, wc_ref, wzc_ref, wnext_ref,
         ut_ref, szt_ref, s_ref, bg_ref, szc_ref, wrow_next_ref, wt_next_ref, xbuf, xsem, h_scr) = refs
        _cast_in_weight(wnext_ref[...], wt_next_ref, wrow_next_ref)
    else:
        (x_hbm, mod_ref, wut_ref, wzt_ref, wv_ref, wb_ref, wc_ref, wzc_ref,
         ut_ref, szt_ref, s_ref, bg_ref, szc_ref, xbuf, xsem, h_scr) = refs
    step = pl.program_id(0) * ntt + pl.program_id(1)
    slot = _fetch_planes(x_hbm, xbuf, xsem, step, nb * ntt, cb=cb, tb=tb, ntt=ntt)
    shift, scale = mod_ref[0:1, :], mod_ref[1:2, :]
    for t in range(tb):
        hn = _layernorm_rows(xbuf[slot, t]) * (1.0 + scale) + shift
        h_scr[t * cb:(t + 1) * cb, :] = hn.astype(BF16)
    h = h_scr[...]
    ut = lax.dot_general(wut_ref[...], h, NT_DIMS, preferred_element_type=F32)
    if cb == LANES:
        ut_ref[...] = ut.astype(BF16)
    else:
        lane = lax.broadcasted_iota(jnp.int32, (1, LANES), 1)
        for t in range(tb):
            tile = ut[:, (t * cb) // LANES * LANES:((t * cb) // LANES + 1) * LANES]
            off = (t * cb) % LANES
            piece = pltpu.roll(tile, LANES - off, 1) if off else tile
            ut_ref[t] = jnp.where(lane < cb, piece, 0.0).astype(BF16)
    if u_only:
        return
    szt_ref[...] = _silu_of_twice(lax.dot_general(wzt_ref[...], h, NT_DIMS,
                                                  preferred_element_type=F32)).astype(BF16)
    v = jnp.dot(h, wv_ref[...], preferred_element_type=F32)
    cg = jnp.dot(h, wc_ref[...], preferred_element_type=F32)
    s_ref[...] = (cg * v).astype(BF16)
    bg_ref[...] = jnp.dot(h, wb_ref[...], preferred_element_type=F32).astype(BF16)
    szc_ref[...] = _silu_of_twice(jnp.dot(h, wzc_ref[...], preferred_element_type=F32)).astype(BF16)


def _inproj(x3, mod, wt, wrow, mod_idx, cb, tb, u_only=False, w_next=None, next_layer=0):
    nc = x3.shape[0]
    n = nc * T
    nb, ntt = nc // cb, T // tb
    r = cb * tb
    col = lambda b, tt: (0, b * ntt + tt)
    row = lambda b, tt: (b * ntt + tt, 0)
    in_specs = [pl.BlockSpec(memory_space=pl.ANY), _layer_spec((3, D), mod_idx, 0, 0),
                _layer_spec((DS, D), 0, 0, 0)]
    if cb == LANES:
        out_specs = [pl.BlockSpec((DS, r), col)]
        out_shape = [jax.ShapeDtypeStruct((DS, n), BF16)]
    else:
        assert nb == 1 and ntt == 1 and LANES % cb == 0
        out_specs = [pl.BlockSpec((T, DS, LANES), lambda b, tt: (0, 0, 0))]
        out_shape = [jax.ShapeDtypeStruct((T, DS, LANES), BF16)]
    args = [x3, mod, wt]
    if not u_only:
        in_specs += [_layer_spec((DS, D), 0, 1, 0)] + [_layer_spec((D, DS), 0, 0, k) for k in range(4)]
        out_specs += [pl.BlockSpec((DS, r), col)] + [pl.BlockSpec((r, DS), row)] * 3
        out_shape += [jax.ShapeDtypeStruct((DS, n), BF16)] + [jax.ShapeDtypeStruct((n, DS), BF16)] * 3
        args += [wt, wrow, wrow, wrow, wrow]
    if w_next is not None:
        wr = D // (nb * ntt)
        in_specs += [pl.BlockSpec((None, wr, 6 * DS), lambda b, tt: (next_layer, b * ntt + tt, 0))]
        out_specs += [pl.BlockSpec((None, wr, 4 * DS), lambda b, tt: (0, b * ntt + tt, 0)),
                      pl.BlockSpec((None, 2 * DS, wr), lambda b, tt: (0, 0, b * ntt + tt))]
        out_shape += [jax.ShapeDtypeStruct((1, D, 4 * DS), BF16), jax.ShapeDtypeStruct((1, 2 * DS, D), BF16)]
        args += [w_next]
    return pl.pallas_call(
        functools.partial(_inproj_kernel, cb=cb, tb=tb, nb=nb, ntt=ntt, u_only=u_only,
                          cast_next=w_next is not None),
        grid=(nb, ntt),
        in_specs=in_specs,
        out_specs=out_specs,
        out_shape=out_shape,
        scratch_shapes=[pltpu.VMEM((2, tb, cb, D), F32), pltpu.SemaphoreType.DMA((2, tb)),
                        pltpu.VMEM((r, D), BF16)],
        compiler_params=_cparams(("arbitrary", "arbitrary")),
        name="inproj_u" if u_only else "inproj",
    )(*args)


GB = 8
NCL = 512
NCC = 16
NCT = NCL + NCC
NCP = 640
CBL = 128


def _s5_kernel(ut_ref, utc_ref, w1t_ref, w2_ref, acoef_ref, wo_ref, yt_ref, ytc_ref, wo_bf_ref,
               sre_scr, sim_scr, hre_scr, him_scr, yloc_scr, *, ctx_out):
    nbl = NCL // CBL
    s5_rows = pl.program_id(0) < DS // wo_ref.shape[0]
    wo_bf_ref[...] = (jnp.where(s5_rows, O5_SCALE, 1.0) * wo_ref[...]).astype(BF16)
    for gl in range(GB):
        rows = slice(gl * H, (gl + 1) * H)
        planes = []
        for t in range(T):
            lat = [ut_ref[rows, (b * T + t) * CBL:(b * T + t + 1) * CBL] for b in range(nbl)]
            planes.append(jnp.concatenate(lat + [utc_ref[t, rows, :]], axis=1))
        z = jnp.concatenate(planes, axis=0)
        r1 = jnp.dot(w1t_ref[gl], z, preferred_element_type=F32)
        yloc_scr[gl] = r1[0:TH]
        st = r1[TH:2 * TH].T
        sre_scr[pl.ds(gl, NCP, stride=GB), :] = st[:, 0:LANES]
        sim_scr[pl.ds(gl, NCP, stride=GB), :] = st[:, LANES:2 * LANES]

    ar, ai = acoef_ref[0], acoef_ref[1]
    fwd_lane = lax.broadcasted_iota(jnp.int32, (GB, LANES), 1) < P
    bwd_lane = jnp.logical_not(fwd_lane)
    hre_scr[NCT * GB:NCP * GB, :] = jnp.zeros(((NCP - NCT) * GB, LANES), F32)
    him_scr[NCT * GB:NCP * GB, :] = jnp.zeros(((NCP - NCT) * GB, LANES), F32)

    def step(k, carry, fwd_shift):
        hr, hi = carry
        cf = k + fwd_shift
        cb_ = NCT - 1 - k
        rf = pl.ds(pl.multiple_of(cf * GB, GB), GB)
        rb = pl.ds(pl.multiple_of(cb_ * GB, GB), GB)
        pltpu.store(hre_scr.at[rf, :], hr, mask=fwd_lane)
        pltpu.store(hre_scr.at[rb, :], hr, mask=bwd_lane)
        pltpu.store(him_scr.at[rf, :], hi, mask=fwd_lane)
        pltpu.store(him_scr.at[rb, :], hi, mask=bwd_lane)
        sr = jnp.where(fwd_lane, sre_scr[rf, :], sre_scr[rb, :])
        si = jnp.where(fwd_lane, sim_scr[rf, :], sim_scr[rb, :])
        return ar * hr - ai * hi + sr, ar * hi + ai * hr + si

    zero = jnp.zeros((GB, LANES), F32)
    carry = lax.fori_loop(0, NCC, functools.partial(step, fwd_shift=NCL), (zero, zero))
    lax.fori_loop(NCC, NCT, functools.partial(step, fwd_shift=-NCC), carry, unroll=4)

    for gl in range(GB):
        rows = slice(gl * H, (gl + 1) * H)
        sel = pl.ds(gl, NCP, stride=GB)
        hp = jnp.concatenate([hre_scr[sel, :], him_scr[sel, :]], axis=1)
        y = yloc_scr[gl] + lax.dot_general(w2_ref[gl], hp.astype(BF16), NT_DIMS, preferred_element_type=F32)
        yb = y.astype(BF16)
        for t in range(T):
            for b in range(nbl):
                yt_ref[rows, (b * T + t) * CBL:(b * T + t + 1) * CBL] = \
                    yb[t * H:(t + 1) * H, b * CBL:(b + 1) * CBL]
            if ctx_out:
                ytc_ref[t, rows, :] = yb[t * H:(t + 1) * H, NCL:NCP]
    if not ctx_out:
        ytc_ref[...] = jnp.zeros(ytc_ref.shape, BF16)


def _s5(ut, utc, w1t, w2, acoef, w_out, layer, ctx_out):
    n = ut.shape[1]
    ncx = NCP - NCL
    wr = 2 * DS // (G // GB)
    state = pltpu.VMEM((NCP * GB, LANES), F32)
    return pl.pallas_call(
        functools.partial(_s5_kernel, ctx_out=ctx_out),
        grid=(G // GB,),
        in_specs=[pl.BlockSpec((GB * H, n), lambda i: (i, 0)),
                  pl.BlockSpec((T, GB * H, ncx), lambda i: (0, i, 0)),
                  pl.BlockSpec((None, GB, 2 * TH, TH), lambda i: (layer, i, 0, 0)),
                  pl.BlockSpec((None, GB, TH, TH), lambda i: (layer, i, 0, 0)),
                  pl.BlockSpec((None, 2, GB, LANES), lambda i: (layer, 0, i, 0)),
                  pl.BlockSpec((None, wr, D), lambda i: (layer, i, 0))],
        out_specs=[pl.BlockSpec((GB * H, n), lambda i: (i, 0)),
                   pl.BlockSpec((T, GB * H, ncx), lambda i: (0, i, 0)),
                   pl.BlockSpec((None, wr, D), lambda i: (0, i, 0))],
        out_shape=[jax.ShapeDtypeStruct((DS, n), BF16), jax.ShapeDtypeStruct((T, DS, ncx), BF16),
                   jax.ShapeDtypeStruct((1, 2 * DS, D), BF16)],
        scratch_shapes=[state, state, state, state, pltpu.VMEM((GB, TH, NCP), F32)],
        compiler_params=_cparams(("parallel",)),
        name="s5",
    )(ut, utc, w1t, w2, acoef, w_out)


def _outproj_kernel(x_hbm, mod_ref, yt_ref, szt_ref, s_ref, sprev_ref, snext_ref, bg_ref, szc_ref,
                    wgt_ref, bglu_ref, cw_ref, cb_ref, wo_ref, lng_ref, lnb_ref,
                    o_hbm, xbuf, xsem, obuf, osem, *, cb, tb, nb, ntt, row_chunks, alpha):
    tt = pl.program_id(1)
    step = pl.program_id(0) * ntt + tt
    nsteps = nb * ntt
    slot = _fetch_planes(x_hbm, xbuf, xsem, step, nsteps, cb=cb, tb=tb, ntt=ntt)
    if cb == LANES:
        y = yt_ref[...].astype(F32)
    else:
        lane = lax.broadcasted_iota(jnp.int32, (1, LANES), 1)
        per = LANES // cb
        tiles = []
        for q in range(tb // per):
            acc = jnp.where(lane < cb, yt_ref[q * per].astype(F32), 0.0)
            for k in range(1, per):
                piece = jnp.where(lane < cb, yt_ref[q * per + k].astype(F32), 0.0)
                acc = acc + pltpu.roll(piece, k * cb, 1)
            tiles.append(acc)
        y = jnp.concatenate(tiles, axis=1)
    g2 =y * jnp.tanh(y * (GELU_C1 + GELU_C2 * (y * y))) + y
    half_gate = jnp.dot(wgt_ref[...], g2.astype(BF16), preferred_element_type=F32) + bglu_ref[...]
    o5 = (g2 * szt_ref[...].astype(F32) * (jnp.tanh(half_gate) + 1.0)).T.astype(BF16)
    crow = lax.broadcasted_iota(jnp.int32, (cb, 1), 0) % row_chunks
    sp = sprev_ref[...].astype(F32)
    sn = snext_ref[...].astype(F32)
    sp_wrap = jnp.where(crow != 0, pltpu.roll(sp, 1, 0), 0.0)
    sn_wrap = jnp.where(crow != row_chunks - 1, pltpu.roll(sn, cb - 1, 0), 0.0)
    sp = jnp.where(tt == 0, sp_wrap, sp).astype(BF16)
    sn = jnp.where(tt == ntt - 1, sn_wrap, sn).astype(BF16)
    s_all = jnp.concatenate([sp, s_ref[...], sn], axis=0)
    r = cb * tb
    cw = cw_ref[...].astype(BF16)
    conv = (cw[0:1, :] * s_all[0:r] + cw[1:2, :] * s_all[cb:cb + r]
            + cw[2:3, :] * s_all[2 * cb:2 * cb + r] + cb_ref[...].astype(BF16))
    ocv = bg_ref[...] * conv * szc_ref[...]
    sub = jnp.dot(jnp.concatenate([o5, ocv], axis=1), wo_ref[...], preferred_element_type=F32)
    gate_vec = mod_ref[2:3, :] * (1.0 / alpha)
    for t in range(tb):
        v = xbuf[slot, t] + gate_vec * sub[t * cb:(t + 1) * cb]
        obuf[slot, t] = _layernorm_rows(v, LN_EPS / (alpha * alpha)) * lng_ref[...] + lnb_ref[...]
    put = functools.partial(_plane_copies, o_hbm, obuf, osem, cb=cb, tb=tb, ntt=ntt, to_hbm=True)

    @pl.when(step > 0)
    def _():
        for cp in put(step - 1, 1 - slot):
            cp.wait()

    for cp in put(step, slot):
        cp.start()

    @pl.when(step == nsteps - 1)
    def _():
        for cp in put(step, slot):
            cp.wait()


def _outproj(x3, mod, yt, szt, s, bg, szc, wo, wts, layer, mod_idx, cb, tb, row_chunks, alpha):
    nc = x3.shape[0]
    nb, ntt = nc // cb, T // tb
    r = cb * tb
    col = lambda b, tt: (0, b * ntt + tt)
    row = lambda b, tt: (b * ntt + tt, 0)
    prev_plane = lambda b, tt: (b * T + (tt * tb + T - 1) % T, 0)
    next_plane = lambda b, tt: (b * T + (tt * tb + tb) % T, 0)
    wgt, bglu, cw, cbias, lng, lnb = wts
    planes = pltpu.VMEM((2, tb, cb, D), F32)
    return pl.pallas_call(
        functools.partial(_outproj_kernel, cb=cb, tb=tb, nb=nb, ntt=ntt, row_chunks=row_chunks, alpha=alpha),
        grid=(nb, ntt),
        in_specs=[pl.BlockSpec(memory_space=pl.ANY),
                  _layer_spec((3, D), mod_idx, 0, 0),
                  pl.BlockSpec((DS, r), col) if cb == LANES else pl.BlockSpec((T, DS, LANES), lambda b, tt: (0, 0, 0)),
                  pl.BlockSpec((DS, r), col),
                  pl.BlockSpec((r, DS), row),
                  pl.BlockSpec((cb, DS), prev_plane), pl.BlockSpec((cb, DS), next_plane),
                  pl.BlockSpec((r, DS), row), pl.BlockSpec((r, DS), row),
                  _layer_spec((DS, DS), layer, 0, 0), _layer_spec((DS, 1), layer, 0, 0),
                  _layer_spec((3, DS), layer, 0, 0), _layer_spec((1, DS), layer, 0, 0),
                  _layer_spec((2 * DS, D), 0, 0, 0),
                  _layer_spec((1, D), layer, 0, 0), _layer_spec((1, D), layer, 0, 0)],
        out_specs=pl.BlockSpec(memory_space=pl.ANY),
        out_shape=jax.ShapeDtypeStruct(x3.shape, F32),
        scratch_shapes=[planes, pltpu.SemaphoreType.DMA((2, tb)), planes, pltpu.SemaphoreType.DMA((2, tb))],
        compiler_params=_cparams(("arbitrary", "arbitrary")),
        name="outproj",
    )(x3, mod, yt, szt, s, s, s, bg, szc, wgt, bglu, cw, cbias, wo, lng, lnb)


def kernel(x, c, ctx, c_ctx, w_ada, b_ada, w_in, s5_a_re, s5_a_im, s5_log_dt, s5_b_re, s5_b_im,
           s5_c_re, s5_c_im, s5_d, w_glu, b_glu, conv_w, conv_b, w_out, ln_g, ln_b):
    bsz, n, d = x.shape
    n_ctx = ctx.shape[1]
    depth = w_in.shape[0]
    assert bsz == 1 and d == D and n == NCL * T and n_ctx == NCC * T
    alpha = (2.0 * depth) ** 0.25

    cond = jnp.pad(jnp.stack([c[0], c_ctx]), ((0, COND_ROWS - 2), (0, 0)))
    w1t, w2, acoef, w_row, w_t, w_glu_t, mod = _s5prep(s5_a_re, s5_a_im, s5_log_dt, s5_b_re, s5_b_im,
                                                       s5_c_re, s5_c_im, s5_d, w_in, w_glu, cond, w_ada, b_ada)
    mod = mod.reshape(depth * 2, 3, D)
    out_w = (w_glu_t, GLU_B_SCALE * b_glu.reshape(depth, DS, 1), conv_w, conv_b.reshape(depth, 1, DS),
             ln_g.reshape(depth, 1, D), ln_b.reshape(depth, 1, D))

    x3 = x.reshape(n // T, T, D)
    c3 = ctx.reshape(n_ctx // T, T, D)
    tb = 4
    for i in range(depth):
        last = i == depth - 1
        ut, szt, s, bg, szc, *w_next = _inproj(x3, mod, w_t, w_row, 2 * i, CBL, tb,
                                               w_next=None if last else w_in, next_layer=i + 1)
        utc, *ctx_rest = _inproj(c3, mod, w_t, w_row, 2 * i + 1, NCC, T, u_only=last)
        yt, ytc, wo = _s5(ut, utc, w1t, w2, acoef, w_out, i, ctx_out=not last)
        x3 = _outproj(x3, mod, yt, szt, s, bg, szc, wo, out_w, i, 2 * i, CBL, tb, GRID_W // T, alpha)
        if not last:
            c3 = _outproj(c3, mod, ytc, *ctx_rest, wo, out_w, i, 2 * i + 1, NCC, T, NCC, alpha)
            w_row, w_t = w_next
    return x3.reshape(bsz, n, D)
```

```python
import functools
import math

import jax
import jax.numpy as jnp
from jax import lax
from jax.experimental import pallas as pl
from jax.experimental.pallas import tpu as pltpu

D = 2048
DS = 1024
G = 64
H = 16
P = 64
T = 16
TH = T * H
GRID_W = 64
LN_EPS = 1e-6
LANES = 128
COND_ROWS = 16
VMEM_LIMIT = 60 * 1024 * 1024

F32 = jnp.float32
BF16 = jnp.bfloat16
NT_DIMS = (((1,), (1,)), ((), ()))
TN_DIMS = (((0,), (0,)), ((), ()))


def _cparams(sem):
    return pltpu.CompilerParams(dimension_semantics=sem, vmem_limit_bytes=VMEM_LIMIT)


def _layer_spec(shape, *idx):
    return pl.BlockSpec((None,) + shape, lambda *_: idx, pipeline_mode=pl.Buffered(1))


GELU_C1 = math.sqrt(2.0 / math.pi)
GELU_C2 = 0.044715 * GELU_C1
GLU_W_SCALE = 0.25
GLU_B_SCALE = 0.5
O5_SCALE = 0.25


def _silu_of_twice(h):
    return h * jnp.tanh(h) + h


def _silu(v):
    return _silu_of_twice(0.5 * v)


def _cast_in_weight(wn, wt_ref, wrow_ref):
    wt_ref[0:DS, :] = wn[:, 0:DS].T.astype(BF16)
    wt_ref[DS:2 * DS, :] = (0.5 * wn[:, DS:2 * DS]).T.astype(BF16)
    wrow_ref[:, 0:3 * DS] = wn[:, 2 * DS:5 * DS].astype(BF16)
    wrow_ref[:, 3 * DS:4 * DS] = (0.5 * wn[:, 5 * DS:6 * DS]).astype(BF16)


def _adaln_block(cond_ref, w_ref, b_ref, o_ref, s_scr, first):
    @pl.when(first)
    def _():
        s_scr[...] = _silu(cond_ref[...]).astype(BF16)

    mod = jnp.dot(s_scr[...], w_ref[...].astype(BF16), preferred_element_type=F32)
    o_ref[...] = mod[0:2, :] + b_ref[...]


GP = 8


def _s5prep_kernel(are_ref, aim_ref, ldt_ref, cre_ref, cim_ref, bre_ref, bim_ref, d_ref,
                   win_ref, wglu_ref, cond_ref, wada_ref, bada_ref,
                   w1t_ref, w2_ref, acoef_ref, wrow_ref, wt_ref, wgt_ref, mod_ref, kt_scr, s_scr):
    first = jnp.logical_and(pl.program_id(0) == 0, pl.program_id(1) == 0)
    _adaln_block(cond_ref, wada_ref, bada_ref, mod_ref, s_scr, first)
    _cast_in_weight(win_ref[...], wt_ref, wrow_ref)
    wgt_ref[...] = (GLU_W_SCALE * wglu_ref[...].T).astype(BF16)
    lane = lax.broadcasted_iota(jnp.int32, (1, LANES), 1)
    fwd_lane = lane < P
    krow = lax.broadcasted_iota(jnp.int32, (T, 1), 0)
    fwd_row = lax.broadcasted_iota(jnp.int32, (2 * P, 1), 0) < P
    col256 = lax.broadcasted_iota(jnp.int32, (T, TH), 1)
    r16 = lax.broadcasted_iota(jnp.int32, (T, TH), 0)
    diag_mask = (col256 % H == r16).astype(F32)
    e_rep = diag_mask.astype(BF16)
    e_blk = (col256 // H == r16).astype(BF16)

    def rows16(tab):
        return jnp.broadcast_to(tab[:, None, :], (T, H, tab.shape[1])).reshape(TH, tab.shape[1])

    def tile16(mat):
        return jnp.broadcast_to(mat[None], (T, H, mat.shape[1])).reshape(TH, mat.shape[1])

    def bdot(a, b):
        return jnp.dot(a.astype(BF16), b.astype(BF16), preferred_element_type=F32)

    def expand(tab, e):
        return lax.dot_general(tab.astype(BF16), e, TN_DIMS, preferred_element_type=F32)

    for gi in range(GP):
        def dircat(ref):
            return jnp.concatenate([ref[0, 0, gi], ref[0, 1, gi]], axis=-1)

        ar, ai = dircat(are_ref), dircat(aim_ref)
        dt = jnp.exp(jnp.where(fwd_lane, ldt_ref[0, 0, gi], ldt_ref[0, 1, gi]))
        la, th = ar * dt, ai * dt
        cr = tile16(dircat(cre_ref).astype(BF16))
        ci = tile16(dircat(cim_ref).astype(BF16))
        b_x = [bdot(jnp.concatenate([b_ref[0, 0, gi], b_ref[0, 1, gi]], axis=0), e_rep)
               for b_ref in (bre_ref, bim_ref)]

        mag1 = jnp.exp(la)
        a1r, a1i = mag1 * jnp.cos(th), mag1 * jnp.sin(th)
        squares = [(a1r, a1i)]
        for _ in range(4):
            sr, si = squares[-1]
            squares.append((sr * sr - si * si, 2.0 * sr * si))

        zr, zi = a1r - 1.0, a1i
        den = ar * ar + ai * ai
        fr, fi = (zr * ar + zi * ai) / den, (zi * ar - zr * ai) / den

        def powrow(kf, kb, with_f):
            k = jnp.where(fwd_lane, kf, kb)
            if with_f:
                pr, pi = jnp.broadcast_to(fr, k.shape), jnp.broadcast_to(fi, k.shape)
            else:
                pr, pi = jnp.ones(k.shape, F32), jnp.zeros(k.shape, F32)
            for b, (sr, si) in enumerate(squares):
                bit = ((k >> b) & 1) == 1
                pr, pi = jnp.where(bit, pr * sr - pi * si, pr), jnp.where(bit, pr * si + pi * sr, pi)
            return pr, pi

        def ctab(kf, kb, with_f):
            pr, pi = powrow(kf, kb, with_f)
            pr, pi = rows16(pr.astype(BF16)), rows16(pi.astype(BF16))
            return cr * pr - ci * pi, cr * pi + ci * pr

        bbr_x, bbi_x = b_x
        ckr, cki = ctab(krow, (T - 1) - krow, True)
        br, bi = bbr_x.astype(BF16), (-bbi_x).astype(BF16)
        zero = jnp.zeros_like(br)
        rhs = jnp.concatenate(
            [jnp.concatenate([jnp.where(fwd_row, b, zero), jnp.where(fwd_row, zero, b)], axis=1) for b in (br, bi)],
            axis=0)
        kfb = jnp.dot(jnp.concatenate([ckr, cki], axis=1), rhs, preferred_element_type=F32)
        kf, kb = kfb[:, 0:TH], kfb[:, TH:2 * TH]
        kt_scr[0:TH - H, :] = kb[0:TH - H]
        kt_scr[TH - H:TH, :] = kb[TH - H:TH] + kf[0:H] + diag_mask * d_ref[0, gi:gi + 1, :]
        kt_scr[TH:2 * TH - H, :] = kf[H:TH]
        for r0 in range(0, TH, LANES):
            for c0 in range(0, TH, LANES):
                blk = (lax.broadcasted_iota(jnp.int32, (LANES, LANES), 1) + c0) // H
                m = jnp.zeros((LANES, LANES), F32)
                for s in range(c0 // H, (c0 + LANES) // H):
                    w0 = (T - 1 - s) * H + r0
                    m = jnp.where(blk == s, kt_scr[w0:w0 + LANES, c0:c0 + LANES], m)
                w1t_ref[0, gi, r0:r0 + LANES, c0:c0 + LANES] = m.astype(BF16)

        pir, pii = powrow((T - 1) - krow, krow, True)
        pxr, pxi = expand(pir, e_blk), expand(pii, e_blk)
        w1t_ref[0, gi, TH:TH + 2 * P, :] = (pxr * bbr_x - pxi * bbi_x).astype(BF16)
        w1t_ref[0, gi, TH + 2 * P:2 * TH, :] = (pxr * bbi_x + pxi * bbr_x).astype(BF16)

        cor, coi = ctab(krow + 1, T - krow, False)
        w2_ref[0, gi, :, 0:LANES] = cor.astype(BF16)
        w2_ref[0, gi, :, LANES:2 * LANES] = (-coi).astype(BF16)

        acoef_ref[0, 0, gi:gi + 1, :] = squares[4][0]
        acoef_ref[0, 1, gi:gi + 1, :] = squares[4][1]


def _s5prep(a_re, a_im, log_dt, b_re, b_im, c_re, c_im, d_skip, w_in, w_glu, cond, w_ada, b_ada):
    depth = a_re.shape[0]
    nsteps = depth * (G // GP)
    wr = D // nsteps
    gr = DS // (G // GP)
    n3 = w_ada.shape[2]
    mc = n3 // (G // GP)
    step = lambda l, g: l * (G // GP) + g
    d_cols = jnp.tile(d_skip.reshape(depth, G, H), (1, 1, T))

    dir_spec = lambda r, n: pl.BlockSpec((1, 2, GP, r, n), lambda l, g: (l, 0, g, 0, 0))
    return pl.pallas_call(
        _s5prep_kernel,
        grid=(depth, G // GP),
        in_specs=[dir_spec(1, P), dir_spec(1, P), dir_spec(1, 1), dir_spec(H, P), dir_spec(H, P),
                  dir_spec(P, H), dir_spec(P, H),
                  pl.BlockSpec((1, GP, TH), lambda l, g: (l, g, 0)),
                  pl.BlockSpec((None, wr, 6 * DS), lambda l, g: (0, step(l, g), 0)),
                  pl.BlockSpec((None, gr, DS), lambda l, g: (l, g, 0)),
                  pl.BlockSpec((COND_ROWS, D), lambda l, g: (0, 0)),
                  pl.BlockSpec((None, D, mc), lambda l, g: (l, 0, g)),
                  pl.BlockSpec((None, 1, mc), lambda l, g: (l, 0, g))],
        out_specs=[pl.BlockSpec((1, GP, 2 * TH, TH), lambda l, g: (l, g, 0, 0)),
                   pl.BlockSpec((1, GP, TH, TH), lambda l, g: (l, g, 0, 0)),
                   pl.BlockSpec((1, 2, GP, 2 * P), lambda l, g: (l, 0, g, 0)),
                   pl.BlockSpec((None, wr, 4 * DS), lambda l, g: (0, step(l, g), 0)),
                   pl.BlockSpec((None, 2 * DS, wr), lambda l, g: (0, 0, step(l, g))),
                   pl.BlockSpec((None, DS, gr), lambda l, g: (l, 0, g)),
                   pl.BlockSpec((None, 2, mc), lambda l, g: (l, 0, g))],
        out_shape=[jax.ShapeDtypeStruct((depth, G, 2 * TH, TH), BF16),
                   jax.ShapeDtypeStruct((depth, G, TH, TH), BF16),
                   jax.ShapeDtypeStruct((depth, 2, G, 2 * P), F32),
                   jax.ShapeDtypeStruct((1, D, 4 * DS), BF16),
                   jax.ShapeDtypeStruct((1, 2 * DS, D), BF16),
                   jax.ShapeDtypeStruct((depth, DS, DS), BF16),
                   jax.ShapeDtypeStruct((depth, 2, n3), F32)],
        scratch_shapes=[pltpu.VMEM((2 * TH, TH), F32), pltpu.VMEM((COND_ROWS, D), BF16)],
        compiler_params=_cparams(("arbitrary", "arbitrary")),
        name="prep",
    )(a_re[:, :, :, None, :], a_im[:, :, :, None, :], log_dt[:, :, :, None, None], c_re, c_im, b_re, b_im,
      d_cols, w_in, w_glu, cond, w_ada, b_ada.reshape(depth, 1, n3))


def _plane_copies(hbm3, buf, sem, step, slot, *, cb, tb, ntt, to_hbm):
    b, tt = step // ntt, step % ntt
    copies = []
    for t in range(tb):
        hbm = hbm3.at[pl.ds(b * cb, cb), tt * tb + t, :]
        vm = buf.at[slot, t]
        copies.append(pltpu.make_async_copy(vm, hbm, sem.at[slot, t]) if to_hbm
                      else pltpu.make_async_copy(hbm, vm, sem.at[slot, t]))
    return copies


def _fetch_planes(x_hbm, xbuf, xsem, step, nsteps, **kw):
    slot = step % 2
    mk = functools.partial(_plane_copies, x_hbm, xbuf, xsem, to_hbm=False, **kw)

    @pl.when(step == 0)
    def _():
        for cp in mk(step, slot):
            cp.start()

    @pl.when(step + 1 < nsteps)
    def _():
        for cp in mk(step + 1, 1 - slot):
            cp.start()

    for cp in mk(step, slot):
        cp.wait()
    return slot


def _layernorm_rows(v, eps=LN_EPS):
    mu = jnp.mean(v, axis=-1, keepdims=True)
    vc = v - mu
    var = jnp.mean(vc * vc, axis=-1, keepdims=True)
    return vc * lax.rsqrt(var + eps)


def _inproj_kernel(*refs, cb, tb, nb, ntt, u_only, cast_next):
    if u_only:
        x_hbm, mod_ref, wut_ref, ut_ref, xbuf, xsem, h_scr = refs
    elif cast_next:
        (x_hbm, mod_ref, wut_ref, wzt_ref, wv_ref, wb_ref, wc_ref, wzc_ref, wnext_ref,
         ut_ref, szt_ref, s_ref, bg_ref, szc_ref, wrow_next_ref, wt_next_ref, xbuf, xsem, h_scr) = refs
        _cast_in_weight(wnext_ref[...], wt_next_ref, wrow_next_ref)
    else:
        (x_hbm, mod_ref, wut_ref, wzt_ref, wv_ref, wb_ref, wc_ref, wzc_ref,
         ut_ref, szt_ref, s_ref, bg_ref, szc_ref, xbuf, xsem, h_scr) = refs
    step = pl.program_id(0) * ntt + pl.program_id(1)
    slot = _fetch_planes(x_hbm, xbuf, xsem, step, nb * ntt, cb=cb, tb=tb, ntt=ntt)
    shift, scale = mod_ref[0:1, :], mod_ref[1:2, :]
    for t in range(tb):
        hn = _layernorm_rows(xbuf[slot, t]) * (1.0 + scale) + shift
        h_scr[t * cb:(t + 1) * cb, :] = hn.astype(BF16)
    h = h_scr[...]
    ut = lax.dot_general(wut_ref[...], h, NT_DIMS, preferred_element_type=F32)
    if cb == LANES:
        ut_ref[...] = ut.astype(BF16)
    else:
        lane = lax.broadcasted_iota(jnp.int32, (1, LANES), 1)
        for t in range(tb):
            tile = ut[:, (t * cb) // LANES * LANES:((t * cb) // LANES + 1) * LANES]
            off = (t * cb) % LANES
            piece = pltpu.roll(tile, LANES - off, 1) if off else tile
            ut_ref[t] = jnp.where(lane < cb, piece, 0.0).astype(BF16)
    if u_only:
        return
    szt_ref[...] = _silu_of_twice(lax.dot_general(wzt_ref[...], h, NT_DIMS,
                                                  preferred_element_type=F32)).astype(BF16)
    v = jnp.dot(h, wv_ref[...], preferred_element_type=F32)
    cg = jnp.dot(h, wc_ref[...], preferred_element_type=F32)
    s_ref[...] = (cg * v).astype(BF16)
    bg_ref[...] = jnp.dot(h, wb_ref[...], preferred_element_type=F32).astype(BF16)
    szc_ref[...] = _silu_of_twice(jnp.dot(h, wzc_ref[...], preferred_element_type=F32)).astype(BF16)


def _inproj(x3, mod, wt, wrow, mod_idx, cb, tb, u_only=False, w_next=None, next_layer=0):
    nc = x3.shape[0]
    n = nc * T
    nb, ntt = nc // cb, T // tb
    r = cb * tb
    col = lambda b, tt: (0, b * ntt + tt)
    row = lambda b, tt: (b * ntt + tt, 0)
    in_specs = [pl.BlockSpec(memory_space=pl.ANY), _layer_spec((3, D), mod_idx, 0, 0),
                _layer_spec((DS, D), 0, 0, 0)]
    if cb == LANES:
        out_specs = [pl.BlockSpec((DS, r), col)]
        out_shape = [jax.ShapeDtypeStruct((DS, n), BF16)]
    else:
        assert nb == 1 and ntt == 1 and LANES % cb == 0
        out_specs = [pl.BlockSpec((T, DS, LANES), lambda b, tt: (0, 0, 0))]
        out_shape = [jax.ShapeDtypeStruct((T, DS, LANES), BF16)]
    args = [x3, mod, wt]
    if not u_only:
        in_specs += [_layer_spec((DS, D), 0, 1, 0)] + [_layer_spec((D, DS), 0, 0, k) for k in range(4)]
        out_specs += [pl.BlockSpec((DS, r), col)] + [pl.BlockSpec((r, DS), row)] * 3
        out_shape += [jax.ShapeDtypeStruct((DS, n), BF16)] + [jax.ShapeDtypeStruct((n, DS), BF16)] * 3
        args += [wt, wrow, wrow, wrow, wrow]
    if w_next is not None:
        wr = D // (nb * ntt)
        in_specs += [pl.BlockSpec((None, wr, 6 * DS), lambda b, tt: (next_layer, b * ntt + tt, 0))]
        out_specs += [pl.BlockSpec((None, wr, 4 * DS), lambda b, tt: (0, b * ntt + tt, 0)),
                      pl.BlockSpec((None, 2 * DS, wr), lambda b, tt: (0, 0, b * ntt + tt))]
        out_shape += [jax.ShapeDtypeStruct((1, D, 4 * DS), BF16), jax.ShapeDtypeStruct((1, 2 * DS, D), BF16)]
        args += [w_next]
    return pl.pallas_call(
        functools.partial(_inproj_kernel, cb=cb, tb=tb, nb=nb, ntt=ntt, u_only=u_only,
                          cast_next=w_next is not None),
        grid=(nb, ntt),
        in_specs=in_specs,
        out_specs=out_specs,
        out_shape=out_shape,
        scratch_shapes=[pltpu.VMEM((2, tb, cb, D), F32), pltpu.SemaphoreType.DMA((2, tb)),
                        pltpu.VMEM((r, D), BF16)],
        compiler_params=_cparams(("arbitrary", "arbitrary")),
        name="inproj_u" if u_only else "inproj",
    )(*args)


GB = 8
NCL = 512
NCC = 16
NCT = NCL + NCC
NCP = 640
CBL = 128


def _s5_kernel(ut_ref, utc_ref, w1t_ref, w2_ref, acoef_ref, wo_ref, yt_ref, ytc_ref, wo_bf_ref,
               sre_scr, sim_scr, hre_scr, him_scr, yloc_scr, *, ctx_out):
    nbl = NCL // CBL
    s5_rows = pl.program_id(0) < DS // wo_ref.shape[0]
    wo_bf_ref[...] = (jnp.where(s5_rows, O5_SCALE, 1.0) * wo_ref[...]).astype(BF16)
    for gl in range(GB):
        rows = slice(gl * H, (gl + 1) * H)
        planes = []
        for t in range(T):
            lat = [ut_ref[rows, (b * T + t) * CBL:(b * T + t + 1) * CBL] for b in range(nbl)]
            planes.append(jnp.concatenate(lat + [utc_ref[t, rows, :]], axis=1))
        z = jnp.concatenate(planes, axis=0)
        r1 = jnp.dot(w1t_ref[gl], z, preferred_element_type=F32)
        yloc_scr[gl] = r1[0:TH]
        st = r1[TH:2 * TH].T
        sre_scr[pl.ds(gl, NCP, stride=GB), :] = st[:, 0:LANES]
        sim_scr[pl.ds(gl, NCP, stride=GB), :] = st[:, LANES:2 * LANES]

    ar, ai = acoef_ref[0], acoef_ref[1]
    fwd_lane = lax.broadcasted_iota(jnp.int32, (GB, LANES), 1) < P
    bwd_lane = jnp.logical_not(fwd_lane)
    hre_scr[NCT * GB:NCP * GB, :] = jnp.zeros(((NCP - NCT) * GB, LANES), F32)
    him_scr[NCT * GB:NCP * GB, :] = jnp.zeros(((NCP - NCT) * GB, LANES), F32)

    def step(k, carry, fwd_shift):
        hr, hi = carry
        cf = k + fwd_shift
        cb_ = NCT - 1 - k
        rf = pl.ds(pl.multiple_of(cf * GB, GB), GB)
        rb = pl.ds(pl.multiple_of(cb_ * GB, GB), GB)
        pltpu.store(hre_scr.at[rf, :], hr, mask=fwd_lane)
        pltpu.store(hre_scr.at[rb, :], hr, mask=bwd_lane)
        pltpu.store(him_scr.at[rf, :], hi, mask=fwd_lane)
        pltpu.store(him_scr.at[rb, :], hi, mask=bwd_lane)
        sr = jnp.where(fwd_lane, sre_scr[rf, :], sre_scr[rb, :])
        si = jnp.where(fwd_lane, sim_scr[rf, :], sim_scr[rb, :])
        return ar * hr - ai * hi + sr, ar * hi + ai * hr + si

    zero = jnp.zeros((GB, LANES), F32)
    carry = lax.fori_loop(0, NCC, functools.partial(step, fwd_shift=NCL), (zero, zero))
    lax.fori_loop(NCC, NCT, functools.partial(step, fwd_shift=-NCC), carry, unroll=4)

    for gl in range(GB):
        rows = slice(gl * H, (gl + 1) * H)
        sel = pl.ds(gl, NCP, stride=GB)
        hp = jnp.concatenate([hre_scr[sel, :], him_scr[sel, :]], axis=1)
        y = yloc_scr[gl] + lax.dot_general(w2_ref[gl], hp.astype(BF16), NT_DIMS, preferred_element_type=F32)
        yb = y.astype(BF16)
        for t in range(T):
            for b in range(nbl):
                yt_ref[rows, (b * T + t) * CBL:(b * T + t + 1) * CBL] = \
                    yb[t * H:(t + 1) * H, b * CBL:(b + 1) * CBL]
            if ctx_out:
                ytc_ref[t, rows, :] = yb[t * H:(t + 1) * H, NCL:NCP]
    if not ctx_out:
        ytc_ref[...] = jnp.zeros(ytc_ref.shape, BF16)


def _s5(ut, utc, w1t, w2, acoef, w_out, layer, ctx_out):
    n = ut.shape[1]
    ncx = NCP - NCL
    wr = 2 * DS // (G // GB)
    state = pltpu.VMEM((NCP * GB, LANES), F32)
    return pl.pallas_call(
        functools.partial(_s5_kernel, ctx_out=ctx_out),
        grid=(G // GB,),
        in_specs=[pl.BlockSpec((GB * H, n), lambda i: (i, 0)),
                  pl.BlockSpec((T, GB * H, ncx), lambda i: (0, i, 0)),
                  pl.BlockSpec((None, GB, 2 * TH, TH), lambda i: (layer, i, 0, 0)),
                  pl.BlockSpec((None, GB, TH, TH), lambda i: (layer, i, 0, 0)),
                  pl.BlockSpec((None, 2, GB, LANES), lambda i: (layer, 0, i, 0)),
                  pl.BlockSpec((None, wr, D), lambda i: (layer, i, 0))],
        out_specs=[pl.BlockSpec((GB * H, n), lambda i: (i, 0)),
                   pl.BlockSpec((T, GB * H, ncx), lambda i: (0, i, 0)),
                   pl.BlockSpec((None, wr, D), lambda i: (0, i, 0))],
        out_shape=[jax.ShapeDtypeStruct((DS, n), BF16), jax.ShapeDtypeStruct((T, DS, ncx), BF16),
                   jax.ShapeDtypeStruct((1, 2 * DS, D), BF16)],
        scratch_shapes=[state, state, state, state, pltpu.VMEM((GB, TH, NCP), F32)],
        compiler_params=_cparams(("parallel",)),
        name="s5",
    )(ut, utc, w1t, w2, acoef, w_out)


def _outproj_kernel(x_hbm, mod_ref, yt_ref, szt_ref, s_ref, sprev_ref, snext_ref, bg_ref, szc_ref,
                    wgt_ref, bglu_ref, cw_ref, cb_ref, wo_ref, lng_ref, lnb_ref,
                    o_hbm, xbuf, xsem, obuf, osem, *, cb, tb, nb, ntt, row_chunks, alpha):
    tt = pl.program_id(1)
    step = pl.program_id(0) * ntt + tt
    nsteps = nb * ntt
    slot = _fetch_planes(x_hbm, xbuf, xsem, step, nsteps, cb=cb, tb=tb, ntt=ntt)
    if cb == LANES:
        y = yt_ref[...].astype(F32)
    else:
        lane = lax.broadcasted_iota(jnp.int32, (1, LANES), 1)
        per = LANES // cb
        tiles = []
        for q in range(tb // per):
            acc = jnp.where(lane < cb, yt_ref[q * per].astype(F32), 0.0)
            for k in range(1, per):
                piece = jnp.where(lane < cb, yt_ref[q * per + k].astype(F32), 0.0)
                acc = acc + pltpu.roll(piece, k * cb, 1)
            tiles.append(acc)
        y = jnp.concatenate(tiles, axis=1)
    g2 =y * jnp.tanh(y * (GELU_C1 + GELU_C2 * (y * y))) + y
    half_gate = jnp.dot(wgt_ref[...], g2.astype(BF16), preferred_element_type=F32) + bglu_ref[...]
    o5 = (g2 * szt_ref[...].astype(F32) * (jnp.tanh(half_gate) + 1.0)).T.astype(BF16)
    crow = lax.broadcasted_iota(jnp.int32, (cb, 1), 0) % row_chunks
    sp = sprev_ref[...].astype(F32)
    sn = snext_ref[...].astype(F32)
    sp_wrap = jnp.where(crow != 0, pltpu.roll(sp, 1, 0), 0.0)
    sn_wrap = jnp.where(crow != row_chunks - 1, pltpu.roll(sn, cb - 1, 0), 0.0)
    sp = jnp.where(tt == 0, sp_wrap, sp).astype(BF16)
    sn = jnp.where(tt == ntt - 1, sn_wrap, sn).astype(BF16)
    s_all = jnp.concatenate([sp, s_ref[...], sn], axis=0)
    r = cb * tb
    cw = cw_ref[...].astype(BF16)
    conv = (cw[0:1, :] * s_all[0:r] + cw[1:2, :] * s_all[cb:cb + r]
            + cw[2:3, :] * s_all[2 * cb:2 * cb + r] + cb_ref[...].astype(BF16))
    ocv = bg_ref[...] * conv * szc_ref[...]
    sub = jnp.dot(jnp.concatenate([o5, ocv], axis=1), wo_ref[...], preferred_element_type=F32)
    gate_vec = mod_ref[2:3, :] * (1.0 / alpha)
    for t in range(tb):
        v = xbuf[slot, t] + gate_vec * sub[t * cb:(t + 1) * cb]
        obuf[slot, t] = _layernorm_rows(v, LN_EPS / (alpha * alpha)) * lng_ref[...] + lnb_ref[...]
    put = functools.partial(_plane_copies, o_hbm, obuf, osem, cb=cb, tb=tb, ntt=ntt, to_hbm=True)

    @pl.when(step > 0)
    def _():
        for cp in put(step - 1, 1 - slot):
            cp.wait()

    for cp in put(step, slot):
        cp.start()

    @pl.when(step == nsteps - 1)
    def _():
        for cp in put(step, slot):
            cp.wait()


def _outproj(x3, mod, yt, szt, s, bg, szc, wo, wts, layer, mod_idx, cb, tb, row_chunks, alpha):
    nc = x3.shape[0]
    nb, ntt = nc // cb, T // tb
    r = cb * tb
    col = lambda b, tt: (0, b * ntt + tt)
    row = lambda b, tt: (b * ntt + tt, 0)
    prev_plane = lambda b, tt: (b * T + (tt * tb + T - 1) % T, 0)
    next_plane = lambda b, tt: (b * T + (tt * tb + tb) % T, 0)
    wgt, bglu, cw, cbias, lng, lnb = wts
    planes = pltpu.VMEM((2, tb, cb, D), F32)
    return pl.pallas_call(
        functools.partial(_outproj_kernel, cb=cb, tb=tb, nb=nb, ntt=ntt, row_chunks=row_chunks, alpha=alpha),
        grid=(nb, ntt),
        in_specs=[pl.BlockSpec(memory_space=pl.ANY),
                  _layer_spec((3, D), mod_idx, 0, 0),
                  pl.BlockSpec((DS, r), col) if cb == LANES else pl.BlockSpec((T, DS, LANES), lambda b, tt: (0, 0, 0)),
                  pl.BlockSpec((DS, r), col),
                  pl.BlockSpec((r, DS), row),
                  pl.BlockSpec((cb, DS), prev_plane), pl.BlockSpec((cb, DS), next_plane),
                  pl.BlockSpec((r, DS), row), pl.BlockSpec((r, DS), row),
                  _layer_spec((DS, DS), layer, 0, 0), _layer_spec((DS, 1), layer, 0, 0),
                  _layer_spec((3, DS), layer, 0, 0), _layer_spec((1, DS), layer, 0, 0),
                  _layer_spec((2 * DS, D), 0, 0, 0),
                  _layer_spec((1, D), layer, 0, 0), _layer_spec((1, D), layer, 0, 0)],
        out_specs=pl.BlockSpec(memory_space=pl.ANY),
        out_shape=jax.ShapeDtypeStruct(x3.shape, F32),
        scratch_shapes=[planes, pltpu.SemaphoreType.DMA((2, tb)), planes, pltpu.SemaphoreType.DMA((2, tb))],
        compiler_params=_cparams(("arbitrary", "arbitrary")),
        name="outproj",
    )(x3, mod, yt, szt, s, s, s, bg, szc, wgt, bglu, cw, cbias, wo, lng, lnb)


def kernel(x, c, ctx, c_ctx, w_ada, b_ada, w_in, s5_a_re, s5_a_im, s5_log_dt, s5_b_re, s5_b_im,
           s5_c_re, s5_c_im, s5_d, w_glu, b_glu, conv_w, conv_b, w_out, ln_g, ln_b):
    bsz, n, d = x.shape
    n_ctx = ctx.shape[1]
    depth = w_in.shape[0]
    assert bsz == 1 and d == D and n == NCL * T and n_ctx == NCC * T
    alpha = (2.0 * depth) ** 0.25

    cond = jnp.pad(jnp.stack([c[0], c_ctx]), ((0, COND_ROWS - 2), (0, 0)))
    w1t, w2, acoef, w_row, w_t, w_glu_t, mod = _s5prep(s5_a_re, s5_a_im, s5_log_dt, s5_b_re, s5_b_im,
                                                       s5_c_re, s5_c_im, s5_d, w_in, w_glu, cond, w_ada, b_ada)
    mod = mod.reshape(depth * 2, 3, D)
    out_w = (w_glu_t, GLU_B_SCALE * b_glu.reshape(depth, DS, 1), conv_w, conv_b.reshape(depth, 1, DS),
             ln_g.reshape(depth, 1, D), ln_b.reshape(depth, 1, D))

    x3 = x.reshape(n // T, T, D)
    c3 = ctx.reshape(n_ctx // T, T, D)
    tb = 4
    for i in range(depth):
        last = i == depth - 1
        ut, szt, s, bg, szc, *w_next = _inproj(x3, mod, w_t, w_row, 2 * i, CBL, tb,
                                               w_next=None if last else w_in, next_layer=i + 1)
        utc, *ctx_rest = _inproj(c3, mod, w_t, w_row, 2 * i + 1, NCC, T, u_only=last)
        yt, ytc, wo = _s5(ut, utc, w1t, w2, acoef, w_out, i, ctx_out=not last)
        x3 = _outproj(x3, mod, yt, szt, s, bg, szc, wo, out_w, i, 2 * i, CBL, tb, GRID_W // T, alpha)
        if not last:
            c3 = _outproj(c3, mod, ytc, *ctx_rest, wo, out_w, i, 2 * i + 1, NCC, T, NCC, alpha)
            w_row, w_t = w_next
    return x3.reshape(bsz, n, D)
```

```python
import functools
import math

import jax
import jax.numpy as jnp
from jax import lax
from jax.experimental import pallas as pl
from jax.experimental.pallas import tpu as pltpu

D = 2048
DS = 1024
G = 64
H = 16
P = 64
T = 16
TH = T * H
GRID_W = 64
LN_EPS = 1e-6
LANES = 128
COND_ROWS = 16
VMEM_LIMIT = 60 * 1024 * 1024

F32 = jnp.float32
BF16 = jnp.bfloat16
NT_DIMS = (((1,), (1,)), ((), ()))
TN_DIMS = (((0,), (0,)), ((), ()))


def _cparams(sem):
    return pltpu.CompilerParams(dimension_semantics=sem, vmem_limit_bytes=VMEM_LIMIT)


def _layer_spec(shape, *idx):
    return pl.BlockSpec((None,) + shape, lambda *_: idx, pipeline_mode=pl.Buffered(1))


GELU_C1 = math.sqrt(2.0 / math.pi)
GELU_C2 = 0.044715 * GELU_C1
GLU_W_SCALE = 0.25
GLU_B_SCALE = 0.5
O5_SCALE = 0.25


def _silu_of_twice(h):
    return h * jnp.tanh(h) + h


def _silu(v):
    return _silu_of_twice(0.5 * v)


def _cast_in_weight(wn, wt_ref, wrow_ref):
    wt_ref[0:DS, :] = wn[:, 0:DS].T.astype(BF16)
    wt_ref[DS:2 * DS, :] = (0.5 * wn[:, DS:2 * DS]).T.astype(BF16)
    wrow_ref[:, 0:3 * DS] = wn[:, 2 * DS:5 * DS].astype(BF16)
    wrow_ref[:, 3 * DS:4 * DS] = (0.5 * wn[:, 5 * DS:6 * DS]).astype(BF16)


def _adaln_block(cond_ref, w_ref, b_ref, o_ref, s_scr, first):
    @pl.when(first)
    def _():
        s_scr[...] = _silu(cond_ref[...]).astype(BF16)

    mod = jnp.dot(s_scr[...], w_ref[...].astype(BF16), preferred_element_type=F32)
    o_ref[...] = mod[0:2, :] + b_ref[...]


GP = 8


def _s5prep_kernel(are_ref, aim_ref, ldt_ref, cre_ref, cim_ref, bre_ref, bim_ref, d_ref,
                   win_ref, wglu_ref, cond_ref, wada_ref, bada_ref,
                   w1t_ref, w2_ref, acoef_ref, wrow_ref, wt_ref, wgt_ref, mod_ref, kt_scr, s_scr):
    first = jnp.logical_and(pl.program_id(0) == 0, pl.program_id(1) == 0)
    _adaln_block(cond_ref, wada_ref, bada_ref, mod_ref, s_scr, first)
    _cast_in_weight(win_ref[...], wt_ref, wrow_ref)
    wgt_ref[...] = (GLU_W_SCALE * wglu_ref[...].T).astype(BF16)
    lane = lax.broadcasted_iota(jnp.int32, (1, LANES), 1)
    fwd_lane = lane < P
    krow = lax.broadcasted_iota(jnp.int32, (T, 1), 0)
    fwd_row = lax.broadcasted_iota(jnp.int32, (2 * P, 1), 0) < P
    col256 = lax.broadcasted_iota(jnp.int32, (T, TH), 1)
    r16 = lax.broadcasted_iota(jnp.int32, (T, TH), 0)
    diag_mask = (col256 % H == r16).astype(F32)
    e_rep = diag_mask.astype(BF16)
    e_blk = (col256 // H == r16).astype(BF16)

    def rows16(tab):
        return jnp.broadcast_to(tab[:, None, :], (T, H, tab.shape[1])).reshape(TH, tab.shape[1])

    def tile16(mat):
        return jnp.broadcast_to(mat[None], (T, H, mat.shape[1])).reshape(TH, mat.shape[1])

    def bdot(a, b):
        return jnp.dot(a.astype(BF16), b.astype(BF16), preferred_element_type=F32)

    def expand(tab, e):
        return lax.dot_general(tab.astype(BF16), e, TN_DIMS, preferred_element_type=F32)

    for gi in range(GP):
        def dircat(ref):
            return jnp.concatenate([ref[0, 0, gi], ref[0, 1, gi]], axis=-1)

        ar, ai = dircat(are_ref), dircat(aim_ref)
        dt = jnp.exp(jnp.where(fwd_lane, ldt_ref[0, 0, gi], ldt_ref[0, 1, gi]))
        la, th = ar * dt, ai * dt
        cr = tile16(dircat(cre_ref).astype(BF16))
        ci = tile16(dircat(cim_ref).astype(BF16))
        b_x = [bdot(jnp.concatenate([b_ref[0, 0, gi], b_ref[0, 1, gi]], axis=0), e_rep)
               for b_ref in (bre_ref, bim_ref)]

        mag1 = jnp.exp(la)
        a1r, a1i = mag1 * jnp.cos(th), mag1 * jnp.sin(th)
        squares = [(a1r, a1i)]
        for _ in range(4):
            sr, si = squares[-1]
            squares.append((sr * sr - si * si, 2.0 * sr * si))

        zr, zi = a1r - 1.0, a1i
        den = ar * ar + ai * ai
        fr, fi = (zr * ar + zi * ai) / den, (zi * ar - zr * ai) / den

        def powrow(kf, kb, with_f):
            k = jnp.where(fwd_lane, kf, kb)
            if with_f:
                pr, pi = jnp.broadcast_to(fr, k.shape), jnp.broadcast_to(fi, k.shape)
            else:
                pr, pi = jnp.ones(k.shape, F32), jnp.zeros(k.shape, F32)
            for b, (sr, si) in enumerate(squares):
                bit = ((k >> b) & 1) == 1
                pr, pi = jnp.where(bit, pr * sr - pi * si, pr), jnp.where(bit, pr * si + pi * sr, pi)
            return pr, pi

        def ctab(kf, kb, with_f):
            pr, pi = powrow(kf, kb, with_f)
            pr, pi = rows16(pr.astype(BF16)), rows16(pi.astype(BF16))
            return cr * pr - ci * pi, cr * pi + ci * pr

        bbr_x, bbi_x = b_x
        ckr, cki = ctab(krow, (T - 1) - krow, True)
        br, bi = bbr_x.astype(BF16), (-bbi_x).astype(BF16)
        zero = jnp.zeros_like(br)
        rhs = jnp.concatenate(
            [jnp.concatenate([jnp.where(fwd_row, b, zero), jnp.where(fwd_row, zero, b)], axis=1) for b in (br, bi)],
            axis=0)
        kfb = jnp.dot(jnp.concatenate([ckr, cki], axis=1), rhs, preferred_element_type=F32)
        kf, kb = kfb[:, 0:TH], kfb[:, TH:2 * TH]
        kt_scr[0:TH - H, :] = kb[0:TH - H]
        kt_scr[TH - H:TH, :] = kb[TH - H:TH] + kf[0:H] + diag_mask * d_ref[0, gi:gi + 1, :]
        kt_scr[TH:2 * TH - H, :] = kf[H:TH]
        for r0 in range(0, TH, LANES):
            for c0 in range(0, TH, LANES):
                blk = (lax.broadcasted_iota(jnp.int32, (LANES, LANES), 1) + c0) // H
                m = jnp.zeros((LANES, LANES), F32)
                for s in range(c0 // H, (c0 + LANES) // H):
                    w0 = (T - 1 - s) * H + r0
                    m = jnp.where(blk == s, kt_scr[w0:w0 + LANES, c0:c0 + LANES], m)
                w1t_ref[0, gi, r0:r0 + LANES, c0:c0 + LANES] = m.astype(BF16)

        pir, pii = powrow((T - 1) - krow, krow, True)
        pxr, pxi = expand(pir, e_blk), expand(pii, e_blk)
        w1t_ref[0, gi, TH:TH + 2 * P, :] = (pxr * bbr_x - pxi * bbi_x).astype(BF16)
        w1t_ref[0, gi, TH + 2 * P:2 * TH, :] = (pxr * bbi_x + pxi * bbr_x).astype(BF16)

        cor, coi = ctab(krow + 1, T - krow, False)
        w2_ref[0, gi, :, 0:LANES] = cor.astype(BF16)
        w2_ref[0, gi, :, LANES:2 * LANES] = (-coi).astype(BF16)

        acoef_ref[0, 0, gi:gi + 1, :] = squares[4][0]
        acoef_ref[0, 1, gi:gi + 1, :] = squares[4][1]


def _s5prep(a_re, a_im, log_dt, b_re, b_im, c_re, c_im, d_skip, w_in, w_glu, cond, w_ada, b_ada):
    depth = a_re.shape[0]
    nsteps = depth * (G // GP)
    wr = D // nsteps
    gr = DS // (G // GP)
    n3 = w_ada.shape[2]
    mc = n3 // (G // GP)
    step = lambda l, g: l * (G // GP) + g
    d_cols = jnp.tile(d_skip.reshape(depth, G, H), (1, 1, T))

    dir_spec = lambda r, n: pl.BlockSpec((1, 2, GP, r, n), lambda l, g: (l, 0, g, 0, 0))
    return pl.pallas_call(
        _s5prep_kernel,
        grid=(depth, G // GP),
        in_specs=[dir_spec(1, P), dir_spec(1, P), dir_spec(1, 1), dir_spec(H, P), dir_spec(H, P),
                  dir_spec(P, H), dir_spec(P, H),
                  pl.BlockSpec((1, GP, TH), lambda l, g: (l, g, 0)),
                  pl.BlockSpec((None, wr, 6 * DS), lambda l, g: (0, step(l, g), 0)),
                  pl.BlockSpec((None, gr, DS), lambda l, g: (l, g, 0)),
                  pl.BlockSpec((COND_ROWS, D), lambda l, g: (0, 0)),
                  pl.BlockSpec((None, D, mc), lambda l, g: (l, 0, g)),
                  pl.BlockSpec((None, 1, mc), lambda l, g: (l, 0, g))],
        out_specs=[pl.BlockSpec((1, GP, 2 * TH, TH), lambda l, g: (l, g, 0, 0)),
                   pl.BlockSpec((1, GP, TH, TH), lambda l, g: (l, g, 0, 0)),
                   pl.BlockSpec((1, 2, GP, 2 * P), lambda l, g: (l, 0, g, 0)),
                   pl.BlockSpec((None, wr, 4 * DS), lambda l, g: (0, step(l, g), 0)),
                   pl.BlockSpec((None, 2 * DS, wr), lambda l, g: (0, 0, step(l, g))),
                   pl.BlockSpec((None, DS, gr), lambda l, g: (l, 0, g)),
                   pl.BlockSpec((None, 2, mc), lambda l, g: (l, 0, g))],
        out_shape=[jax.ShapeDtypeStruct((depth, G, 2 * TH, TH), BF16),
                   jax.ShapeDtypeStruct((depth, G, TH, TH), BF16),
                   jax.ShapeDtypeStruct((depth, 2, G, 2 * P), F32),
                   jax.ShapeDtypeStruct((1, D, 4 * DS), BF16),
                   jax.ShapeDtypeStruct((1, 2 * DS, D), BF16),
                   jax.ShapeDtypeStruct((depth, DS, DS), BF16),
                   jax.ShapeDtypeStruct((depth, 2, n3), F32)],
        scratch_shapes=[pltpu.VMEM((2 * TH, TH), F32), pltpu.VMEM((COND_ROWS, D), BF16)],
        compiler_params=_cparams(("arbitrary", "arbitrary")),
        name="prep",
    )(a_re[:, :, :, None, :], a_im[:, :, :, None, :], log_dt[:, :, :, None, None], c_re, c_im, b_re, b_im,
      d_cols, w_in, w_glu, cond, w_ada, b_ada.reshape(depth, 1, n3))


def _plane_copies(hbm3, buf, sem, step, slot, *, cb, tb, ntt, to_hbm):
    b, tt = step // ntt, step % ntt
    copies = []
    for t in range(tb):
        hbm = hbm3.at[pl.ds(b * cb, cb), tt * tb + t, :]
        vm = buf.at[slot, t]
        copies.append(pltpu.make_async_copy(vm, hbm, sem.at[slot, t]) if to_hbm
                      else pltpu.make_async_copy(hbm, vm, sem.at[slot, t]))
    return copies


def _fetch_planes(x_hbm, xbuf, xsem, step, nsteps, **kw):
    slot = step % 2
    mk = functools.partial(_plane_copies, x_hbm, xbuf, xsem, to_hbm=False, **kw)

    @pl.when(step == 0)
    def _():
        for cp in mk(step, slot):
            cp.start()

    @pl.when(step + 1 < nsteps)
    def _():
        for cp in mk(step + 1, 1 - slot):
            cp.start()

    for cp in mk(step, slot):
        cp.wait()
    return slot


def _layernorm_rows(v, eps=LN_EPS):
    mu = jnp.mean(v, axis=-1, keepdims=True)
    vc = v - mu
    var = jnp.mean(vc * vc, axis=-1, keepdims=True)
    return vc * lax.rsqrt(var + eps)


def _store_short_planes(ut, ut_ref, cb, tb):
    lane = lax.broadcasted_iota(jnp.int32, (1, LANES), 1)
    for t in range(tb):
        tile = ut[:, (t * cb) // LANES * LANES:((t * cb) // LANES + 1) * LANES]
        off = (t * cb) % LANES
        piece = pltpu.roll(tile, LANES - off, 1) if off else tile
        ut_ref[t] = jnp.where(lane < cb, piece, 0.0).astype(BF16)


def _inproj_kernel(*refs, cb, tb, nb, ntt, u_only, cast_next):
    if u_only:
        x_hbm, mod_ref, wut_ref, ut_ref, xbuf, xsem, h_scr = refs
    elif cast_next:
        (x_hbm, mod_ref, wut_ref, wzt_ref, wv_ref, wb_ref, wc_ref, wzc_ref, wnext_ref,
         ut_ref, szt_ref, s_ref, bg_ref, szc_ref, wrow_next_ref, wt_next_ref, xbuf, xsem, h_scr) = refs
        _cast_in_weight(wnext_ref[...], wt_next_ref, wrow_next_ref)
    else:
        (x_hbm, mod_ref, wut_ref, wzt_ref, wv_ref, wb_ref, wc_ref, wzc_ref,
         ut_ref, szt_ref, s_ref, bg_ref, szc_ref, xbuf, xsem, h_scr) = refs
    step = pl.program_id(0) * ntt + pl.program_id(1)
    slot = _fetch_planes(x_hbm, xbuf, xsem, step, nb * ntt, cb=cb, tb=tb, ntt=ntt)
    shift, scale = mod_ref[0:1, :], mod_ref[1:2, :]
    for t in range(tb):
        hn = _layernorm_rows(xbuf[slot, t]) * (1.0 + scale) + shift
        h_scr[t * cb:(t + 1) * cb, :] = hn.astype(BF16)
    h = h_scr[...]
    ut = lax.dot_general(wut_ref[...], h, NT_DIMS, preferred_element_type=F32)
    if cb == LANES:
        ut_ref[...] = ut.astype(BF16)
    else:
        _store_short_planes(ut, ut_ref, cb, tb)
    if u_only:
        return
    szt_ref[...] = _silu_of_twice(lax.dot_general(wzt_ref[...], h, NT_DIMS,
                                                  preferred_element_type=F32)).astype(BF16)
    v = jnp.dot(h, wv_ref[...], preferred_element_type=F32)
    cg = jnp.dot(h, wc_ref[...], preferred_element_type=F32)
    s_ref[...] = (cg * v).astype(BF16)
    bg_ref[...] = jnp.dot(h, wb_ref[...], preferred_element_type=F32).astype(BF16)
    szc_ref[...] = _silu_of_twice(jnp.dot(h, wzc_ref[...], preferred_element_type=F32)).astype(BF16)


def _inproj(x3, mod, wt, wrow, mod_idx, cb, tb, u_only=False, w_next=None, next_layer=0):
    nc = x3.shape[0]
    n = nc * T
    nb, ntt = nc // cb, T // tb
    r = cb * tb
    col = lambda b, tt: (0, b * ntt + tt)
    row = lambda b, tt: (b * ntt + tt, 0)
    in_specs = [pl.BlockSpec(memory_space=pl.ANY), _layer_spec((3, D), mod_idx, 0, 0),
                _layer_spec((DS, D), 0, 0, 0)]
    if cb == LANES:
        out_specs = [pl.BlockSpec((DS, r), col)]
        out_shape = [jax.ShapeDtypeStruct((DS, n), BF16)]
    else:
        assert nb == 1 and ntt == 1 and LANES % cb == 0
        out_specs = [pl.BlockSpec((T, DS, LANES), lambda b, tt: (0, 0, 0))]
        out_shape = [jax.ShapeDtypeStruct((T, DS, LANES), BF16)]
    args = [x3, mod, wt]
    if not u_only:
        in_specs += [_layer_spec((DS, D), 0, 1, 0)] + [_layer_spec((D, DS), 0, 0, k) for k in range(4)]
        out_specs += [pl.BlockSpec((DS, r), col)] + [pl.BlockSpec((r, DS), row)] * 3
        out_shape += [jax.ShapeDtypeStruct((DS, n), BF16)] + [jax.ShapeDtypeStruct((n, DS), BF16)] * 3
        args += [wt, wrow, wrow, wrow, wrow]
    if w_next is not None:
        wr = D // (nb * ntt)
        in_specs += [pl.BlockSpec((None, wr, 6 * DS), lambda b, tt: (next_layer, b * ntt + tt, 0))]
        out_specs += [pl.BlockSpec((None, wr, 4 * DS), lambda b, tt: (0, b * ntt + tt, 0)),
                      pl.BlockSpec((None, 2 * DS, wr), lambda b, tt: (0, 0, b * ntt + tt))]
        out_shape += [jax.ShapeDtypeStruct((1, D, 4 * DS), BF16), jax.ShapeDtypeStruct((1, 2 * DS, D), BF16)]
        args += [w_next]
    return pl.pallas_call(
        functools.partial(_inproj_kernel, cb=cb, tb=tb, nb=nb, ntt=ntt, u_only=u_only,
                          cast_next=w_next is not None),
        grid=(nb, ntt),
        in_specs=in_specs,
        out_specs=out_specs,
        out_shape=out_shape,
        scratch_shapes=[pltpu.VMEM((2, tb, cb, D), F32), pltpu.SemaphoreType.DMA((2, tb)),
                        pltpu.VMEM((r, D), BF16)],
        compiler_params=_cparams(("arbitrary", "arbitrary")),
        name="inproj_u" if u_only else "inproj",
    )(*args)


GB = 8
NCL = 512
NCC = 16
NCT = NCL + NCC
NCP = 640
CBL = 128
TB_LATENT = 4


def _s5_kernel(ut_ref, utc_ref, w1t_ref, w2_ref, acoef_ref, wo_ref, yt_ref, ytc_ref, wo_bf_ref,
               sre_scr, sim_scr, hre_scr, him_scr, yloc_scr, *, ctx_out):
    nbl = NCL // CBL
    s5_rows = pl.program_id(0) < DS // wo_ref.shape[0]
    wo_bf_ref[...] = (jnp.where(s5_rows, O5_SCALE, 1.0) * wo_ref[...]).astype(BF16)
    for gl in range(GB):
        rows = slice(gl * H, (gl + 1) * H)
        planes = []
        for t in range(T):
            lat = [ut_ref[rows, (b * T + t) * CBL:(b * T + t + 1) * CBL] for b in range(nbl)]
            planes.append(jnp.concatenate(lat + [utc_ref[t, rows, :]], axis=1))
        z = jnp.concatenate(planes, axis=0)
        r1 = jnp.dot(w1t_ref[gl], z, preferred_element_type=F32)
        yloc_scr[gl] = r1[0:TH]
        st = r1[TH:2 * TH].T
        sre_scr[pl.ds(gl, NCP, stride=GB), :] = st[:, 0:LANES]
        sim_scr[pl.ds(gl, NCP, stride=GB), :] = st[:, LANES:2 * LANES]

    ar, ai = acoef_ref[0], acoef_ref[1]
    fwd_lane = lax.broadcasted_iota(jnp.int32, (GB, LANES), 1) < P
    bwd_lane = jnp.logical_not(fwd_lane)
    hre_scr[NCT * GB:NCP * GB, :] = jnp.zeros(((NCP - NCT) * GB, LANES), F32)
    him_scr[NCT * GB:NCP * GB, :] = jnp.zeros(((NCP - NCT) * GB, LANES), F32)

    def step(k, carry, fwd_shift):
        hr, hi = carry
        cf = k + fwd_shift
        cb_ = NCT - 1 - k
        rf = pl.ds(pl.multiple_of(cf * GB, GB), GB)
        rb = pl.ds(pl.multiple_of(cb_ * GB, GB), GB)
        pltpu.store(hre_scr.at[rf, :], hr, mask=fwd_lane)
        pltpu.store(hre_scr.at[rb, :], hr, mask=bwd_lane)
        pltpu.store(him_scr.at[rf, :], hi, mask=fwd_lane)
        pltpu.store(him_scr.at[rb, :], hi, mask=bwd_lane)
        sr = jnp.where(fwd_lane, sre_scr[rf, :], sre_scr[rb, :])
        si = jnp.where(fwd_lane, sim_scr[rf, :], sim_scr[rb, :])
        return ar * hr - ai * hi + sr, ar * hi + ai * hr + si

    zero = jnp.zeros((GB, LANES), F32)
    carry = lax.fori_loop(0, NCC, functools.partial(step, fwd_shift=NCL), (zero, zero))
    lax.fori_loop(NCC, NCT, functools.partial(step, fwd_shift=-NCC), carry, unroll=4)

    for gl in range(GB):
        rows = slice(gl * H, (gl + 1) * H)
        sel = pl.ds(gl, NCP, stride=GB)
        hp = jnp.concatenate([hre_scr[sel, :], him_scr[sel, :]], axis=1)
        y = yloc_scr[gl] + lax.dot_general(w2_ref[gl], hp.astype(BF16), NT_DIMS, preferred_element_type=F32)
        yb = y.astype(BF16)
        for t in range(T):
            for b in range(nbl):
                yt_ref[rows, (b * T + t) * CBL:(b * T + t + 1) * CBL] = \
                    yb[t * H:(t + 1) * H, b * CBL:(b + 1) * CBL]
            if ctx_out:
                ytc_ref[t, rows, :] = yb[t * H:(t + 1) * H, NCL:NCP]
    if not ctx_out:
        ytc_ref[...] = jnp.zeros(ytc_ref.shape, BF16)


def _s5(ut, utc, w1t, w2, acoef, w_out, layer, ctx_out):
    n = ut.shape[1]
    ncx = NCP - NCL
    wr = 2 * DS // (G // GB)
    state = pltpu.VMEM((NCP * GB, LANES), F32)
    return pl.pallas_call(
        functools.partial(_s5_kernel, ctx_out=ctx_out),
        grid=(G // GB,),
        in_specs=[pl.BlockSpec((GB * H, n), lambda i: (i, 0)),
                  pl.BlockSpec((T, GB * H, ncx), lambda i: (0, i, 0)),
                  pl.BlockSpec((None, GB, 2 * TH, TH), lambda i: (layer, i, 0, 0)),
                  pl.BlockSpec((None, GB, TH, TH), lambda i: (layer, i, 0, 0)),
                  pl.BlockSpec((None, 2, GB, LANES), lambda i: (layer, 0, i, 0)),
                  pl.BlockSpec((None, wr, D), lambda i: (layer, i, 0))],
        out_specs=[pl.BlockSpec((GB * H, n), lambda i: (i, 0)),
                   pl.BlockSpec((T, GB * H, ncx), lambda i: (0, i, 0)),
                   pl.BlockSpec((None, wr, D), lambda i: (0, i, 0))],
        out_shape=[jax.ShapeDtypeStruct((DS, n), BF16), jax.ShapeDtypeStruct((T, DS, ncx), BF16),
                   jax.ShapeDtypeStruct((1, 2 * DS, D), BF16)],
        scratch_shapes=[state, state, state, state, pltpu.VMEM((GB, TH, NCP), F32)],
        compiler_params=_cparams(("parallel",)),
        name="s5",
    )(ut, utc, w1t, w2, acoef, w_out)


def _outproj_kernel(*refs, cb, tb, nb, ntt, row_chunks, alpha, next_u):
    (x_hbm, mod_ref, yt_ref, szt_ref, s_ref, sprev_ref, snext_ref, bg_ref, szc_ref,
     wgt_ref, bglu_ref, cw_ref, cb_ref, wo_ref, lng_ref, lnb_ref) = refs[:16]
    if next_u:
        modn_ref, wutn_ref, o_hbm, utn_ref, xbuf, xsem, obuf, osem = refs[16:]
    else:
        o_hbm, xbuf, xsem, obuf, osem = refs[16:]
    tt = pl.program_id(1)
    step = pl.program_id(0) * ntt + tt
    nsteps = nb * ntt
    slot = _fetch_planes(x_hbm, xbuf, xsem, step, nsteps, cb=cb, tb=tb, ntt=ntt)
    if cb == LANES:
        y = yt_ref[...].astype(F32)
    else:
        lane = lax.broadcasted_iota(jnp.int32, (1, LANES), 1)
        per = LANES // cb
        tiles = []
        for q in range(tb // per):
            acc = jnp.where(lane < cb, yt_ref[q * per].astype(F32), 0.0)
            for k in range(1, per):
                piece = jnp.where(lane < cb, yt_ref[q * per + k].astype(F32), 0.0)
                acc = acc + pltpu.roll(piece, k * cb, 1)
            tiles.append(acc)
        y = jnp.concatenate(tiles, axis=1)
    g2 = y * jnp.tanh(y * (GELU_C1 + GELU_C2 * (y * y))) + y
    half_gate = jnp.dot(wgt_ref[...], g2.astype(BF16), preferred_element_type=F32) + bglu_ref[...]
    o5 = (g2 * szt_ref[...].astype(F32) * (jnp.tanh(half_gate) + 1.0)).T.astype(BF16)
    crow = lax.broadcasted_iota(jnp.int32, (cb, 1), 0) % row_chunks
    sp = sprev_ref[...].astype(F32)
    sn = snext_ref[...].astype(F32)
    sp_wrap = jnp.where(crow != 0, pltpu.roll(sp, 1, 0), 0.0)
    sn_wrap = jnp.where(crow != row_chunks - 1, pltpu.roll(sn, cb - 1, 0), 0.0)
    sp = jnp.where(tt == 0, sp_wrap, sp).astype(BF16)
    sn = jnp.where(tt == ntt - 1, sn_wrap, sn).astype(BF16)
    s_all = jnp.concatenate([sp, s_ref[...], sn], axis=0)
    r = cb * tb
    cw = cw_ref[...].astype(BF16)
    conv = (cw[0:1, :] * s_all[0:r] + cw[1:2, :] * s_all[cb:cb + r]
            + cw[2:3, :] * s_all[2 * cb:2 * cb + r] + cb_ref[...].astype(BF16))
    ocv = bg_ref[...] * conv * szc_ref[...]
    sub = jnp.dot(jnp.concatenate([o5, ocv], axis=1), wo_ref[...], preferred_element_type=F32)
    gate_vec = mod_ref[2:3, :] * (1.0 / alpha)
    for t in range(tb):
        v = xbuf[slot, t] + gate_vec * sub[t * cb:(t + 1) * cb]
        obuf[slot, t] = _layernorm_rows(v, LN_EPS / (alpha * alpha)) * lng_ref[...] + lnb_ref[...]
    if next_u:
        shift, scale = modn_ref[0:1, :], modn_ref[1:2, :]
        hn = jnp.concatenate([(_layernorm_rows(obuf[slot, t]) * (1.0 + scale) + shift).astype(BF16)
                              for t in range(tb)], axis=0)
        _store_short_planes(lax.dot_general(wutn_ref[...], hn, NT_DIMS, preferred_element_type=F32),
                            utn_ref, cb, tb)
    put = functools.partial(_plane_copies, o_hbm, obuf, osem, cb=cb, tb=tb, ntt=ntt, to_hbm=True)

    @pl.when(step > 0)
    def _():
        for cp in put(step - 1, 1 - slot):
            cp.wait()

    for cp in put(step, slot):
        cp.start()

    @pl.when(step == nsteps - 1)
    def _():
        for cp in put(step, slot):
            cp.wait()


def _outproj(x3, mod, yt, szt, s, bg, szc, wo, wts, layer, mod_idx, cb, tb, row_chunks, alpha, next_u=None):
    nc = x3.shape[0]
    nb, ntt = nc // cb, T // tb
    r = cb * tb
    extra_in, extra_out, extra_shape, extra_args = [], [], [], []
    if next_u is not None:
        assert cb != LANES and nb == 1 and ntt == 1
        extra_in = [_layer_spec((3, D), next_u[0], 0, 0), _layer_spec((DS, D), 0, 0, 0)]
        extra_out = [pl.BlockSpec((T, DS, LANES), lambda b, tt: (0, 0, 0))]
        extra_shape = [jax.ShapeDtypeStruct((T, DS, LANES), BF16)]
        extra_args = [mod, next_u[1]]
    col = lambda b, tt: (0, b * ntt + tt)
    row = lambda b, tt: (b * ntt + tt, 0)
    prev_plane = lambda b, tt: (b * T + (tt * tb + T - 1) % T, 0)
    next_plane = lambda b, tt: (b * T + (tt * tb + tb) % T, 0)
    wgt, bglu, cw, cbias, lng, lnb = wts
    planes = pltpu.VMEM((2, tb, cb, D), F32)
    return pl.pallas_call(
        functools.partial(_outproj_kernel, cb=cb, tb=tb, nb=nb, ntt=ntt, row_chunks=row_chunks, alpha=alpha,
                          next_u=next_u is not None),
        grid=(nb, ntt),
        in_specs=[pl.BlockSpec(memory_space=pl.ANY),
                  _layer_spec((3, D), mod_idx, 0, 0),
                  pl.BlockSpec((DS, r), col) if cb == LANES else pl.BlockSpec((T, DS, LANES), lambda b, tt: (0, 0, 0)),
                  pl.BlockSpec((DS, r), col),
                  pl.BlockSpec((r, DS), row),
                  pl.BlockSpec((cb, DS), prev_plane), pl.BlockSpec((cb, DS), next_plane),
                  pl.BlockSpec((r, DS), row), pl.BlockSpec((r, DS), row),
                  _layer_spec((DS, DS), layer, 0, 0), _layer_spec((DS, 1), layer, 0, 0),
                  _layer_spec((3, DS), layer, 0, 0), _layer_spec((1, DS), layer, 0, 0),
                  _layer_spec((2 * DS, D), 0, 0, 0),
                  _layer_spec((1, D), layer, 0, 0), _layer_spec((1, D), layer, 0, 0)] + extra_in,
        out_specs=[pl.BlockSpec(memory_space=pl.ANY)] + extra_out,
        out_shape=[jax.ShapeDtypeStruct(x3.shape, F32)] + extra_shape,
        scratch_shapes=[planes, pltpu.SemaphoreType.DMA((2, tb)), planes, pltpu.SemaphoreType.DMA((2, tb))],
        compiler_params=_cparams(("arbitrary", "arbitrary")),
        name="outproj",
    )(x3, mod, yt, szt, s, s, s, bg, szc, wgt, bglu, cw, cbias, wo, lng, lnb, *extra_args)


def kernel(x, c, ctx, c_ctx, w_ada, b_ada, w_in, s5_a_re, s5_a_im, s5_log_dt, s5_b_re, s5_b_im,
           s5_c_re, s5_c_im, s5_d, w_glu, b_glu, conv_w, conv_b, w_out, ln_g, ln_b):
    bsz, n, d = x.shape
    n_ctx = ctx.shape[1]
    depth = w_in.shape[0]
    assert bsz == 1 and d == D and n == NCL * T and n_ctx == NCC * T
    alpha = (2.0 * depth) ** 0.25

    cond = jnp.pad(jnp.stack([c[0], c_ctx]), ((0, COND_ROWS - 2), (0, 0)))
    w1t, w2, acoef, w_row, w_t, w_glu_t, mod = _s5prep(s5_a_re, s5_a_im, s5_log_dt, s5_b_re, s5_b_im,
                                                       s5_c_re, s5_c_im, s5_d, w_in, w_glu, cond, w_ada, b_ada)
    mod = mod.reshape(depth * 2, 3, D)
    out_w = (w_glu_t, GLU_B_SCALE * b_glu.reshape(depth, DS, 1), conv_w, conv_b.reshape(depth, 1, DS),
             ln_g.reshape(depth, 1, D), ln_b.reshape(depth, 1, D))

    x3 = x.reshape(n // T, T, D)
    c3 = ctx.reshape(n_ctx // T, T, D)
    utc = None
    for i in range(depth):
        last = i == depth - 1
        ut, szt, s, bg, szc, *w_next = _inproj(x3, mod, w_t, w_row, 2 * i, CBL, TB_LATENT,
                                               w_next=None if last else w_in, next_layer=i + 1)
        if utc is None:
            utc, *ctx_rest = _inproj(c3, mod, w_t, w_row, 2 * i + 1, NCC, T, u_only=last)
        yt, ytc, wo = _s5(ut, utc, w1t, w2, acoef, w_out, i, ctx_out=not last)
        utc = None
        (x3,) = _outproj(x3, mod, yt, szt, s, bg, szc, wo, out_w, i, 2 * i, CBL, TB_LATENT, GRID_W // T, alpha)
        if not last:
            w_row, w_t = w_next
            next_u = (2 * (i + 1) + 1, w_t) if i + 1 == depth - 1 else None
            c3, *next_utc = _outproj(c3, mod, ytc, *ctx_rest, wo, out_w, i, 2 * i + 1, NCC, T, NCC, alpha,
                                     next_u=next_u)
            utc = next_utc[0] if next_utc else None
    return x3.reshape(bsz, n, D)
```

```python
import functools
import math

import jax
import jax.numpy as jnp
from jax import lax
from jax.experimental import pallas as pl
from jax.experimental.pallas import tpu as pltpu

D = 2048
DS = 1024
G = 64
H = 16
P = 64
T = 16
TH = T * H
GRID_W = 64
LN_EPS = 1e-6
LANES = 128
COND_ROWS = 16
VMEM_LIMIT = 60 * 1024 * 1024

F32 = jnp.float32
BF16 = jnp.bfloat16
NT_DIMS = (((1,), (1,)), ((), ()))
TN_DIMS = (((0,), (0,)), ((), ()))


def _cparams(sem):
    return pltpu.CompilerParams(dimension_semantics=sem, vmem_limit_bytes=VMEM_LIMIT)


def _layer_spec(shape, *idx):
    return pl.BlockSpec((None,) + shape, lambda *_: idx, pipeline_mode=pl.Buffered(1))


GELU_C1 = math.sqrt(2.0 / math.pi)
GELU_C2 = 0.044715 * GELU_C1
GLU_W_SCALE = 0.25
GLU_B_SCALE = 0.5
O5_SCALE = 0.25


def _silu_of_twice(h):
    return h * jnp.tanh(h) + h


def _silu(v):
    return _silu_of_twice(0.5 * v)


def _cast_in_weight(wn, wt_ref, wrow_ref):
    wt_ref[0:DS, :] = wn[:, 0:DS].T.astype(BF16)
    wt_ref[DS:2 * DS, :] = (0.5 * wn[:, DS:2 * DS]).T.astype(BF16)
    wrow_ref[:, 0:3 * DS] = wn[:, 2 * DS:5 * DS].astype(BF16)
    wrow_ref[:, 3 * DS:4 * DS] = (0.5 * wn[:, 5 * DS:6 * DS]).astype(BF16)


def _adaln_block(cond_ref, w_ref, b_ref, o_ref, s_scr, first):
    @pl.when(first)
    def _():
        s_scr[...] = _silu(cond_ref[...]).astype(BF16)

    mod = jnp.dot(s_scr[...], w_ref[...].astype(BF16), preferred_element_type=F32)
    o_ref[...] = mod[0:2, :] + b_ref[...]


GP = 8


def _s5prep_kernel(are_ref, aim_ref, ldt_ref, cre_ref, cim_ref, bre_ref, bim_ref, d_ref,
                   win_ref, wglu_ref, cond_ref, wada_ref, bada_ref,
                   w1t_ref, w2_ref, acoef_ref, wrow_ref, wt_ref, wgt_ref, mod_ref, kt_scr, s_scr):
    first = jnp.logical_and(pl.program_id(0) == 0, pl.program_id(1) == 0)
    _adaln_block(cond_ref, wada_ref, bada_ref, mod_ref, s_scr, first)
    _cast_in_weight(win_ref[...], wt_ref, wrow_ref)
    wgt_ref[...] = (GLU_W_SCALE * wglu_ref[...].T).astype(BF16)
    lane = lax.broadcasted_iota(jnp.int32, (1, LANES), 1)
    fwd_lane = lane < P
    krow = lax.broadcasted_iota(jnp.int32, (T, 1), 0)
    fwd_row = lax.broadcasted_iota(jnp.int32, (2 * P, 1), 0) < P
    col256 = lax.broadcasted_iota(jnp.int32, (T, TH), 1)
    r16 = lax.broadcasted_iota(jnp.int32, (T, TH), 0)
    diag_mask = (col256 % H == r16).astype(F32)
    e_rep = diag_mask.astype(BF16)
    e_blk = (col256 // H == r16).astype(BF16)

    def rows16(tab):
        return jnp.broadcast_to(tab[:, None, :], (T, H, tab.shape[1])).reshape(TH, tab.shape[1])

    def tile16(mat):
        return jnp.broadcast_to(mat[None], (T, H, mat.shape[1])).reshape(TH, mat.shape[1])

    def expand(tab, e):
        return lax.dot_general(tab.astype(BF16), e, TN_DIMS, preferred_element_type=F32)

    for gi in range(GP):
        def dircat(ref):
            return jnp.concatenate([ref[0, 0, gi], ref[0, 1, gi]], axis=-1)

        ar, ai = dircat(are_ref), dircat(aim_ref)
        dt = jnp.exp(jnp.where(fwd_lane, ldt_ref[0, 0, gi], ldt_ref[0, 1, gi]))
        la, th = ar * dt, ai * dt
        cr = tile16(dircat(cre_ref).astype(BF16))
        ci = tile16(dircat(cim_ref).astype(BF16))
        b_x = [expand(dircat(b_ref), e_rep) for b_ref in (bre_ref, bim_ref)]

        mag1 = jnp.exp(la)
        a1r, a1i = mag1 * jnp.cos(th), mag1 * jnp.sin(th)
        squares = [(a1r, a1i)]
        for _ in range(4):
            sr, si = squares[-1]
            squares.append((sr * sr - si * si, 2.0 * sr * si))

        zr, zi = a1r - 1.0, a1i
        den = ar * ar + ai * ai
        fr, fi = (zr * ar + zi * ai) / den, (zi * ar - zr * ai) / den

        def powrow(kf, kb, with_f):
            k = jnp.where(fwd_lane, kf, kb)
            if with_f:
                pr, pi = jnp.broadcast_to(fr, k.shape), jnp.broadcast_to(fi, k.shape)
            else:
                pr, pi = jnp.ones(k.shape, F32), jnp.zeros(k.shape, F32)
            for b, (sr, si) in enumerate(squares):
                bit = ((k >> b) & 1) == 1
                pr, pi = jnp.where(bit, pr * sr - pi * si, pr), jnp.where(bit, pr * si + pi * sr, pi)
            return pr, pi

        def ctab(kf, kb, with_f):
            pr, pi = powrow(kf, kb, with_f)
            pr, pi = rows16(pr.astype(BF16)), rows16(pi.astype(BF16))
            return cr * pr - ci * pi, cr * pi + ci * pr

        bbr_x, bbi_x = b_x
        ckr, cki = ctab(krow, (T - 1) - krow, True)
        br, bi = bbr_x.astype(BF16), (-bbi_x).astype(BF16)
        zero = jnp.zeros_like(br)
        rhs = jnp.concatenate(
            [jnp.concatenate([jnp.where(fwd_row, b, zero), jnp.where(fwd_row, zero, b)], axis=1) for b in (br, bi)],
            axis=0)
        kfb = jnp.dot(jnp.concatenate([ckr, cki], axis=1), rhs, preferred_element_type=F32)
        kf, kb = kfb[:, 0:TH], kfb[:, TH:2 * TH]
        kt_scr[0:TH - H, :] = kb[0:TH - H]
        kt_scr[TH - H:TH, :] = kb[TH - H:TH] + kf[0:H] + diag_mask * d_ref[0, gi:gi + 1, :]
        kt_scr[TH:2 * TH - H, :] = kf[H:TH]
        for r0 in range(0, TH, LANES):
            for c0 in range(0, TH, LANES):
                blk = (lax.broadcasted_iota(jnp.int32, (LANES, LANES), 1) + c0) // H
                m = jnp.zeros((LANES, LANES), F32)
                for s in range(c0 // H, (c0 + LANES) // H):
                    w0 = (T - 1 - s) * H + r0
                    m = jnp.where(blk == s, kt_scr[w0:w0 + LANES, c0:c0 + LANES], m)
                w1t_ref[0, gi, r0:r0 + LANES, c0:c0 + LANES] = m.astype(BF16)

        pir, pii = powrow((T - 1) - krow, krow, True)
        pxr, pxi = expand(pir, e_blk), expand(pii, e_blk)
        w1t_ref[0, gi, TH:TH + 2 * P, :] = (pxr * bbr_x - pxi * bbi_x).astype(BF16)
        w1t_ref[0, gi, TH + 2 * P:2 * TH, :] = (pxr * bbi_x + pxi * bbr_x).astype(BF16)

        cor, coi = ctab(krow + 1, T - krow, False)
        w2_ref[0, gi, :, 0:LANES] = cor.astype(BF16)
        w2_ref[0, gi, :, LANES:2 * LANES] = (-coi).astype(BF16)

        acoef_ref[0, 0, gi:gi + 1, :] = squares[4][0]
        acoef_ref[0, 1, gi:gi + 1, :] = squares[4][1]


def _s5prep(a_re, a_im, log_dt, b_re, b_im, c_re, c_im, d_skip, w_in, w_glu, cond, w_ada, b_ada):
    depth = a_re.shape[0]
    nsteps = depth * (G // GP)
    wr = D // nsteps
    gr = DS // (G // GP)
    n3 = w_ada.shape[2]
    mc = n3 // (G // GP)
    step = lambda l, g: l * (G // GP) + g
    d_cols = jnp.tile(d_skip.reshape(depth, G, H), (1, 1, T))

    dir_spec = lambda r, n: pl.BlockSpec((1, 2, GP, r, n), lambda l, g: (l, 0, g, 0, 0))
    return pl.pallas_call(
        _s5prep_kernel,
        grid=(depth, G // GP),
        in_specs=[dir_spec(1, P), dir_spec(1, P), dir_spec(1, 1), dir_spec(H, P), dir_spec(H, P),
                  dir_spec(H, P), dir_spec(H, P),
                  pl.BlockSpec((1, GP, TH), lambda l, g: (l, g, 0)),
                  pl.BlockSpec((None, wr, 6 * DS), lambda l, g: (0, step(l, g), 0)),
                  pl.BlockSpec((None, gr, DS), lambda l, g: (l, g, 0)),
                  pl.BlockSpec((COND_ROWS, D), lambda l, g: (0, 0)),
                  pl.BlockSpec((None, D, mc), lambda l, g: (l, 0, g)),
                  pl.BlockSpec((None, 1, mc), lambda l, g: (l, 0, g))],
        out_specs=[pl.BlockSpec((1, GP, 2 * TH, TH), lambda l, g: (l, g, 0, 0)),
                   pl.BlockSpec((1, GP, TH, TH), lambda l, g: (l, g, 0, 0)),
                   pl.BlockSpec((1, 2, GP, 2 * P), lambda l, g: (l, 0, g, 0)),
                   pl.BlockSpec((None, wr, 4 * DS), lambda l, g: (0, step(l, g), 0)),
                   pl.BlockSpec((None, 2 * DS, wr), lambda l, g: (0, 0, step(l, g))),
                   pl.BlockSpec((None, DS, gr), lambda l, g: (l, 0, g)),
                   pl.BlockSpec((None, 2, mc), lambda l, g: (l, 0, g))],
        out_shape=[jax.ShapeDtypeStruct((depth, G, 2 * TH, TH), BF16),
                   jax.ShapeDtypeStruct((depth, G, TH, TH), BF16),
                   jax.ShapeDtypeStruct((depth, 2, G, 2 * P), F32),
                   jax.ShapeDtypeStruct((1, D, 4 * DS), BF16),
                   jax.ShapeDtypeStruct((1, 2 * DS, D), BF16),
                   jax.ShapeDtypeStruct((depth, DS, DS), BF16),
                   jax.ShapeDtypeStruct((depth, 2, n3), F32)],
        scratch_shapes=[pltpu.VMEM((2 * TH, TH), F32), pltpu.VMEM((COND_ROWS, D), BF16)],
        compiler_params=_cparams(("arbitrary", "arbitrary")),
        name="prep",
    )(a_re[:, :, :, None, :], a_im[:, :, :, None, :], log_dt[:, :, :, None, None], c_re, c_im,
      jnp.swapaxes(b_re, 3, 4), jnp.swapaxes(b_im, 3, 4),
      d_cols, w_in, w_glu, cond, w_ada, b_ada.reshape(depth, 1, n3))


def _plane_copies(hbm3, buf, sem, step, slot, *, cb, tb, ntt, to_hbm):
    b, tt = step // ntt, step % ntt
    copies = []
    for t in range(tb):
        hbm = hbm3.at[pl.ds(b * cb, cb), tt * tb + t, :]
        vm = buf.at[slot, t]
        copies.append(pltpu.make_async_copy(vm, hbm, sem.at[slot, t]) if to_hbm
                      else pltpu.make_async_copy(hbm, vm, sem.at[slot, t]))
    return copies


def _fetch_planes(x_hbm, xbuf, xsem, step, nsteps, **kw):
    slot = step % 2
    mk = functools.partial(_plane_copies, x_hbm, xbuf, xsem, to_hbm=False, **kw)

    @pl.when(step == 0)
    def _():
        for cp in mk(step, slot):
            cp.start()

    @pl.when(step + 1 < nsteps)
    def _():
        for cp in mk(step + 1, 1 - slot):
            cp.start()

    for cp in mk(step, slot):
        cp.wait()
    return slot


def _layernorm_rows(v, eps=LN_EPS):
    mu = jnp.mean(v, axis=-1, keepdims=True)
    vc = v - mu
    var = jnp.mean(vc * vc, axis=-1, keepdims=True)
    return vc * lax.rsqrt(var + eps)


def _store_short_planes(ut, ut_ref, cb, tb):
    lane = lax.broadcasted_iota(jnp.int32, (1, LANES), 1)
    for t in range(tb):
        tile = ut[:, (t * cb) // LANES * LANES:((t * cb) // LANES + 1) * LANES]
        off = (t * cb) % LANES
        piece = pltpu.roll(tile, LANES - off, 1) if off else tile
        ut_ref[t] = jnp.where(lane < cb, piece, 0.0).astype(BF16)


def _inproj_kernel(*refs, cb, tb, nb, ntt, u_only, cast_next):
    if u_only:
        x_hbm, mod_ref, wut_ref, ut_ref, xbuf, xsem, h_scr = refs
    elif cast_next:
        (x_hbm, mod_ref, wut_ref, wzt_ref, wv_ref, wb_ref, wc_ref, wzc_ref, wnext_ref,
         ut_ref, szt_ref, s_ref, bg_ref, szc_ref, wrow_next_ref, wt_next_ref, xbuf, xsem, h_scr) = refs
        _cast_in_weight(wnext_ref[...], wt_next_ref, wrow_next_ref)
    else:
        (x_hbm, mod_ref, wut_ref, wzt_ref, wv_ref, wb_ref, wc_ref, wzc_ref,
         ut_ref, szt_ref, s_ref, bg_ref, szc_ref, xbuf, xsem, h_scr) = refs
    step = pl.program_id(0) * ntt + pl.program_id(1)
    slot = _fetch_planes(x_hbm, xbuf, xsem, step, nb * ntt, cb=cb, tb=tb, ntt=ntt)
    shift, scale = mod_ref[0:1, :], mod_ref[1:2, :]
    for t in range(tb):
        hn = _layernorm_rows(xbuf[slot, t]) * (1.0 + scale) + shift
        h_scr[t * cb:(t + 1) * cb, :] = hn.astype(BF16)
    h = h_scr[...]
    ut = lax.dot_general(wut_ref[...], h, NT_DIMS, preferred_element_type=F32)
    if cb == LANES:
        ut_ref[...] = ut.astype(BF16)
    else:
        _store_short_planes(ut, ut_ref, cb, tb)
    if u_only:
        return
    szt_ref[...] = _silu_of_twice(lax.dot_general(wzt_ref[...], h, NT_DIMS,
                                                  preferred_element_type=F32)).astype(BF16)
    v = jnp.dot(h, wv_ref[...], preferred_element_type=F32)
    cg = jnp.dot(h, wc_ref[...], preferred_element_type=F32)
    s_ref[...] = (cg * v).astype(BF16)
    bg_ref[...] = jnp.dot(h, wb_ref[...], preferred_element_type=F32).astype(BF16)
    szc_ref[...] = _silu_of_twice(jnp.dot(h, wzc_ref[...], preferred_element_type=F32)).astype(BF16)


def _inproj(x3, mod, wt, wrow, mod_idx, cb, tb, u_only=False, w_next=None, next_layer=0):
    nc = x3.shape[0]
    n = nc * T
    nb, ntt = nc // cb, T // tb
    r = cb * tb
    col = lambda b, tt: (0, b * ntt + tt)
    row = lambda b, tt: (b * ntt + tt, 0)
    in_specs = [pl.BlockSpec(memory_space=pl.ANY), _layer_spec((3, D), mod_idx, 0, 0),
                _layer_spec((DS, D), 0, 0, 0)]
    if cb == LANES:
        out_specs = [pl.BlockSpec((DS, r), col)]
        out_shape = [jax.ShapeDtypeStruct((DS, n), BF16)]
    else:
        assert nb == 1 and ntt == 1 and LANES % cb == 0
        out_specs = [pl.BlockSpec((T, DS, LANES), lambda b, tt: (0, 0, 0))]
        out_shape = [jax.ShapeDtypeStruct((T, DS, LANES), BF16)]
    args = [x3, mod, wt]
    if not u_only:
        in_specs += [_layer_spec((DS, D), 0, 1, 0)] + [_layer_spec((D, DS), 0, 0, k) for k in range(4)]
        out_specs += [pl.BlockSpec((DS, r), col)] + [pl.BlockSpec((r, DS), row)] * 3
        out_shape += [jax.ShapeDtypeStruct((DS, n), BF16)] + [jax.ShapeDtypeStruct((n, DS), BF16)] * 3
        args += [wt, wrow, wrow, wrow, wrow]
    if w_next is not None:
        wr = D // (nb * ntt)
        in_specs += [pl.BlockSpec((None, wr, 6 * DS), lambda b, tt: (next_layer, b * ntt + tt, 0))]
        out_specs += [pl.BlockSpec((None, wr, 4 * DS), lambda b, tt: (0, b * ntt + tt, 0)),
                      pl.BlockSpec((None, 2 * DS, wr), lambda b, tt: (0, 0, b * ntt + tt))]
        out_shape += [jax.ShapeDtypeStruct((1, D, 4 * DS), BF16), jax.ShapeDtypeStruct((1, 2 * DS, D), BF16)]
        args += [w_next]
    return pl.pallas_call(
        functools.partial(_inproj_kernel, cb=cb, tb=tb, nb=nb, ntt=ntt, u_only=u_only,
                          cast_next=w_next is not None),
        grid=(nb, ntt),
        in_specs=in_specs,
        out_specs=out_specs,
        out_shape=out_shape,
        scratch_shapes=[pltpu.VMEM((2, tb, cb, D), F32), pltpu.SemaphoreType.DMA((2, tb)),
                        pltpu.VMEM((r, D), BF16)],
        compiler_params=_cparams(("arbitrary", "arbitrary")),
        name="inproj_u" if u_only else "inproj",
    )(*args)


GB = 8
NCL = 512
NCC = 16
NCT = NCL + NCC
NCP = 640
CBL = 128
TB_LATENT = 4


def _s5_kernel(ut_ref, utc_ref, w1t_ref, w2_ref, acoef_ref, wo_ref, yt_ref, ytc_ref, wo_bf_ref,
               sre_scr, sim_scr, hre_scr, him_scr, yloc_scr, *, ctx_out):
    nbl = NCL // CBL
    s5_rows = pl.program_id(0) < DS // wo_ref.shape[0]
    wo_bf_ref[...] = (jnp.where(s5_rows, O5_SCALE, 1.0) * wo_ref[...]).astype(BF16)
    for gl in range(GB):
        rows = slice(gl * H, (gl + 1) * H)
        planes = []
        for t in range(T):
            lat = [ut_ref[rows, (b * T + t) * CBL:(b * T + t + 1) * CBL] for b in range(nbl)]
            planes.append(jnp.concatenate(lat + [utc_ref[t, rows, :]], axis=1))
        z = jnp.concatenate(planes, axis=0)
        r1 = jnp.dot(w1t_ref[gl], z, preferred_element_type=F32)
        yloc_scr[gl] = r1[0:TH]
        st = r1[TH:2 * TH].T
        sre_scr[pl.ds(gl, NCP, stride=GB), :] = st[:, 0:LANES]
        sim_scr[pl.ds(gl, NCP, stride=GB), :] = st[:, LANES:2 * LANES]

    ar, ai = acoef_ref[0], acoef_ref[1]
    fwd_lane = lax.broadcasted_iota(jnp.int32, (GB, LANES), 1) < P
    bwd_lane = jnp.logical_not(fwd_lane)
    hre_scr[NCT * GB:NCP * GB, :] = jnp.zeros(((NCP - NCT) * GB, LANES), F32)
    him_scr[NCT * GB:NCP * GB, :] = jnp.zeros(((NCP - NCT) * GB, LANES), F32)

    def step(k, carry, fwd_shift):
        hr, hi = carry
        cf = k + fwd_shift
        cb_ = NCT - 1 - k
        rf = pl.ds(pl.multiple_of(cf * GB, GB), GB)
        rb = pl.ds(pl.multiple_of(cb_ * GB, GB), GB)
        pltpu.store(hre_scr.at[rf, :], hr, mask=fwd_lane)
        pltpu.store(hre_scr.at[rb, :], hr, mask=bwd_lane)
        pltpu.store(him_scr.at[rf, :], hi, mask=fwd_lane)
        pltpu.store(him_scr.at[rb, :], hi, mask=bwd_lane)
        sr = jnp.where(fwd_lane, sre_scr[rf, :], sre_scr[rb, :])
        si = jnp.where(fwd_lane, sim_scr[rf, :], sim_scr[rb, :])
        return ar * hr - ai * hi + sr, ar * hi + ai * hr + si

    zero = jnp.zeros((GB, LANES), F32)
    carry = lax.fori_loop(0, NCC, functools.partial(step, fwd_shift=NCL), (zero, zero))
    lax.fori_loop(NCC, NCT, functools.partial(step, fwd_shift=-NCC), carry, unroll=4)

    for gl in range(GB):
        rows = slice(gl * H, (gl + 1) * H)
        sel = pl.ds(gl, NCP, stride=GB)
        hp = jnp.concatenate([hre_scr[sel, :], him_scr[sel, :]], axis=1)
        y = yloc_scr[gl] + lax.dot_general(w2_ref[gl], hp.astype(BF16), NT_DIMS, preferred_element_type=F32)
        yb = y.astype(BF16)
        for t in range(T):
            for b in range(nbl):
                yt_ref[rows, (b * T + t) * CBL:(b * T + t + 1) * CBL] = \
                    yb[t * H:(t + 1) * H, b * CBL:(b + 1) * CBL]
            if ctx_out:
                ytc_ref[t, rows, :] = yb[t * H:(t + 1) * H, NCL:NCP]
    if not ctx_out:
        ytc_ref[...] = jnp.zeros(ytc_ref.shape, BF16)


def _s5(ut, utc, w1t, w2, acoef, w_out, layer, ctx_out):
    n = ut.shape[1]
    ncx = NCP - NCL
    wr = 2 * DS // (G // GB)
    state = pltpu.VMEM((NCP * GB, LANES), F32)
    return pl.pallas_call(
        functools.partial(_s5_kernel, ctx_out=ctx_out),
        grid=(G // GB,),
        in_specs=[pl.BlockSpec((GB * H, n), lambda i: (i, 0)),
                  pl.BlockSpec((T, GB * H, ncx), lambda i: (0, i, 0)),
                  pl.BlockSpec((None, GB, 2 * TH, TH), lambda i: (layer, i, 0, 0)),
                  pl.BlockSpec((None, GB, TH, TH), lambda i: (layer, i, 0, 0)),
                  pl.BlockSpec((None, 2, GB, LANES), lambda i: (layer, 0, i, 0)),
                  pl.BlockSpec((None, wr, D), lambda i: (layer, i, 0))],
        out_specs=[pl.BlockSpec((GB * H, n), lambda i: (i, 0)),
                   pl.BlockSpec((T, GB * H, ncx), lambda i: (0, i, 0)),
                   pl.BlockSpec((None, wr, D), lambda i: (0, i, 0))],
        out_shape=[jax.ShapeDtypeStruct((DS, n), BF16), jax.ShapeDtypeStruct((T, DS, ncx), BF16),
                   jax.ShapeDtypeStruct((1, 2 * DS, D), BF16)],
        scratch_shapes=[state, state, state, state, pltpu.VMEM((GB, TH, NCP), F32)],
        compiler_params=_cparams(("parallel",)),
        name="s5",
    )(ut, utc, w1t, w2, acoef, w_out)


def _outproj_kernel(*refs, cb, tb, nb, ntt, row_chunks, alpha, next_u):
    (x_hbm, mod_ref, yt_ref, szt_ref, s_ref, sprev_ref, snext_ref, bg_ref, szc_ref,
     wgt_ref, bglu_ref, cw_ref, cb_ref, wo_ref, lng_ref, lnb_ref) = refs[:16]
    if next_u:
        modn_ref, wutn_ref, o_hbm, utn_ref, xbuf, xsem, obuf, osem = refs[16:]
    else:
        o_hbm, xbuf, xsem, obuf, osem = refs[16:]
    tt = pl.program_id(1)
    step = pl.program_id(0) * ntt + tt
    nsteps = nb * ntt
    slot = _fetch_planes(x_hbm, xbuf, xsem, step, nsteps, cb=cb, tb=tb, ntt=ntt)
    if cb == LANES:
        y = yt_ref[...].astype(F32)
    else:
        lane = lax.broadcasted_iota(jnp.int32, (1, LANES), 1)
        per = LANES // cb
        tiles = []
        for q in range(tb // per):
            acc = jnp.where(lane < cb, yt_ref[q * per].astype(F32), 0.0)
            for k in range(1, per):
                piece = jnp.where(lane < cb, yt_ref[q * per + k].astype(F32), 0.0)
                acc = acc + pltpu.roll(piece, k * cb, 1)
            tiles.append(acc)
        y = jnp.concatenate(tiles, axis=1)
    g2 = y * jnp.tanh(y * (GELU_C1 + GELU_C2 * (y * y))) + y
    half_gate = jnp.dot(wgt_ref[...], g2.astype(BF16), preferred_element_type=F32) + bglu_ref[...]
    o5 = (g2 * szt_ref[...].astype(F32) * (jnp.tanh(half_gate) + 1.0)).T.astype(BF16)
    crow = lax.broadcasted_iota(jnp.int32, (cb, 1), 0) % row_chunks
    sp = sprev_ref[...].astype(F32)
    sn = snext_ref[...].astype(F32)
    sp_wrap = jnp.where(crow != 0, pltpu.roll(sp, 1, 0), 0.0)
    sn_wrap = jnp.where(crow != row_chunks - 1, pltpu.roll(sn, cb - 1, 0), 0.0)
    sp = jnp.where(tt == 0, sp_wrap, sp).astype(BF16)
    sn = jnp.where(tt == ntt - 1, sn_wrap, sn).astype(BF16)
    s_all = jnp.concatenate([sp, s_ref[...], sn], axis=0)
    r = cb * tb
    cw = cw_ref[...].astype(BF16)
    conv = (cw[0:1, :] * s_all[0:r] + cw[1:2, :] * s_all[cb:cb + r]
            + cw[2:3, :] * s_all[2 * cb:2 * cb + r] + cb_ref[...].astype(BF16))
    ocv = bg_ref[...] * conv * szc_ref[...]
    sub = jnp.dot(jnp.concatenate([o5, ocv], axis=1), wo_ref[...], preferred_element_type=F32)
    gate_vec = mod_ref[2:3, :] * (1.0 / alpha)
    for t in range(tb):
        v = xbuf[slot, t] + gate_vec * sub[t * cb:(t + 1) * cb]
        obuf[slot, t] = _layernorm_rows(v, LN_EPS / (alpha * alpha)) * lng_ref[...] + lnb_ref[...]
    if next_u:
        shift, scale = modn_ref[0:1, :], modn_ref[1:2, :]
        hn = jnp.concatenate([(_layernorm_rows(obuf[slot, t]) * (1.0 + scale) + shift).astype(BF16)
                              for t in range(tb)], axis=0)
        _store_short_planes(lax.dot_general(wutn_ref[...], hn, NT_DIMS, preferred_element_type=F32),
                            utn_ref, cb, tb)
    put = functools.partial(_plane_copies, o_hbm, obuf, osem, cb=cb, tb=tb, ntt=ntt, to_hbm=True)

    @pl.when(step > 0)
    def _():
        for cp in put(step - 1, 1 - slot):
            cp.wait()

    for cp in put(step, slot):
        cp.start()

    @pl.when(step == nsteps - 1)
    def _():
        for cp in put(step, slot):
            cp.wait()


def _outproj(x3, mod, yt, szt, s, bg, szc, wo, wts, layer, mod_idx, cb, tb, row_chunks, alpha, next_u=None):
    nc = x3.shape[0]
    nb, ntt = nc // cb, T // tb
    r = cb * tb
    extra_in, extra_out, extra_shape, extra_args = [], [], [], []
    if next_u is not None:
        assert cb != LANES and nb == 1 and ntt == 1
        extra_in = [_layer_spec((3, D), next_u[0], 0, 0), _layer_spec((DS, D), 0, 0, 0)]
        extra_out = [pl.BlockSpec((T, DS, LANES), lambda b, tt: (0, 0, 0))]
        extra_shape = [jax.ShapeDtypeStruct((T, DS, LANES), BF16)]
        extra_args = [mod, next_u[1]]
    col = lambda b, tt: (0, b * ntt + tt)
    row = lambda b, tt: (b * ntt + tt, 0)
    prev_plane = lambda b, tt: (b * T + (tt * tb + T - 1) % T, 0)
    next_plane = lambda b, tt: (b * T + (tt * tb + tb) % T, 0)
    wgt, bglu, cw, cbias, lng, lnb = wts
    planes = pltpu.VMEM((2, tb, cb, D), F32)
    return pl.pallas_call(
        functools.partial(_outproj_kernel, cb=cb, tb=tb, nb=nb, ntt=ntt, row_chunks=row_chunks, alpha=alpha,
                          next_u=next_u is not None),
        grid=(nb, ntt),
        in_specs=[pl.BlockSpec(memory_space=pl.ANY),
                  _layer_spec((3, D), mod_idx, 0, 0),
                  pl.BlockSpec((DS, r), col) if cb == LANES else pl.BlockSpec((T, DS, LANES), lambda b, tt: (0, 0, 0)),
                  pl.BlockSpec((DS, r), col),
                  pl.BlockSpec((r, DS), row),
                  pl.BlockSpec((cb, DS), prev_plane), pl.BlockSpec((cb, DS), next_plane),
                  pl.BlockSpec((r, DS), row), pl.BlockSpec((r, DS), row),
                  _layer_spec((DS, DS), layer, 0, 0), _layer_spec((DS, 1), layer, 0, 0),
                  _layer_spec((3, DS), layer, 0, 0), _layer_spec((1, DS), layer, 0, 0),
                  _layer_spec((2 * DS, D), 0, 0, 0),
                  _layer_spec((1, D), layer, 0, 0), _layer_spec((1, D), layer, 0, 0)] + extra_in,
        out_specs=[pl.BlockSpec(memory_space=pl.ANY)] + extra_out,
        out_shape=[jax.ShapeDtypeStruct(x3.shape, F32)] + extra_shape,
        scratch_shapes=[planes, pltpu.SemaphoreType.DMA((2, tb)), planes, pltpu.SemaphoreType.DMA((2, tb))],
        compiler_params=_cparams(("arbitrary", "arbitrary")),
        name="outproj",
    )(x3, mod, yt, szt, s, s, s, bg, szc, wgt, bglu, cw, cbias, wo, lng, lnb, *extra_args)


def kernel(x, c, ctx, c_ctx, w_ada, b_ada, w_in, s5_a_re, s5_a_im, s5_log_dt, s5_b_re, s5_b_im,
           s5_c_re, s5_c_im, s5_d, w_glu, b_glu, conv_w, conv_b, w_out, ln_g, ln_b):
    bsz, n, d = x.shape
    n_ctx = ctx.shape[1]
    depth = w_in.shape[0]
    assert bsz == 1 and d == D and n == NCL * T and n_ctx == NCC * T
    alpha = (2.0 * depth) ** 0.25

    cond = jnp.pad(jnp.stack([c[0], c_ctx]), ((0, COND_ROWS - 2), (0, 0)))
    w1t, w2, acoef, w_row, w_t, w_glu_t, mod = _s5prep(s5_a_re, s5_a_im, s5_log_dt, s5_b_re, s5_b_im,
                                                       s5_c_re, s5_c_im, s5_d, w_in, w_glu, cond, w_ada, b_ada)
    mod = mod.reshape(depth * 2, 3, D)
    out_w = (w_glu_t, GLU_B_SCALE * b_glu.reshape(depth, DS, 1), conv_w, conv_b.reshape(depth, 1, DS),
             ln_g.reshape(depth, 1, D), ln_b.reshape(depth, 1, D))

    x3 = x.reshape(n // T, T, D)
    c3 = ctx.reshape(n_ctx // T, T, D)
    utc = None
    for i in range(depth):
        last = i == depth - 1
        ut, szt, s, bg, szc, *w_next = _inproj(x3, mod, w_t, w_row, 2 * i, CBL, TB_LATENT,
                                               w_next=None if last else w_in, next_layer=i + 1)
        if utc is None:
            utc, *ctx_rest = _inproj(c3, mod, w_t, w_row, 2 * i + 1, NCC, T, u_only=last)
        yt, ytc, wo = _s5(ut, utc, w1t, w2, acoef, w_out, i, ctx_out=not last)
        utc = None
        (x3,) = _outproj(x3, mod, yt, szt, s, bg, szc, wo, out_w, i, 2 * i, CBL, TB_LATENT, GRID_W // T, alpha)
        if not last:
            w_row, w_t = w_next
            next_u = (2 * (i + 1) + 1, w_t) if i + 1 == depth - 1 else None
            c3, *next_utc = _outproj(c3, mod, ytc, *ctx_rest, wo, out_w, i, 2 * i + 1, NCC, T, NCC, alpha,
                                     next_u=next_u)
            utc = next_utc[0] if next_utc else None
    return x3.reshape(bsz, n, D)
```

```python
import functools
import math

import jax
import jax.numpy as jnp
from jax import lax
from jax.experimental import pallas as pl
from jax.experimental.pallas import tpu as pltpu

D = 2048
DS = 1024
G = 64
H = 16
P = 64
T = 16
TH = T * H
GRID_W = 64
LN_EPS = 1e-6
LANES = 128
COND_ROWS = 16
VMEM_LIMIT = 60 * 1024 * 1024

F32 = jnp.float32
BF16 = jnp.bfloat16
NT_DIMS = (((1,), (1,)), ((), ()))
TN_DIMS = (((0,), (0,)), ((), ()))


def _cparams(sem):
    return pltpu.CompilerParams(dimension_semantics=sem, vmem_limit_bytes=VMEM_LIMIT)


def _layer_spec(shape, *idx):
    return pl.BlockSpec((None,) + shape, lambda *_: idx, pipeline_mode=pl.Buffered(1))


GELU_C1 = math.sqrt(2.0 / math.pi)
GELU_C2 = 0.044715 * GELU_C1
GLU_W_SCALE = 0.25
GLU_B_SCALE = 0.5
O5_SCALE = 0.25


def _silu_of_twice(h):
    return h * jnp.tanh(h) + h


def _silu(v):
    return _silu_of_twice(0.5 * v)


def _cast_in_weight(wn, wt_ref, wrow_ref):
    wt_ref[0:DS, :] = wn[:, 0:DS].T.astype(BF16)
    wt_ref[DS:2 * DS, :] = (0.5 * wn[:, DS:2 * DS]).T.astype(BF16)
    wrow_ref[:, 0:3 * DS] = wn[:, 2 * DS:5 * DS].astype(BF16)
    wrow_ref[:, 3 * DS:4 * DS] = (0.5 * wn[:, 5 * DS:6 * DS]).astype(BF16)


def _adaln_block(cond_ref, w_ref, b_ref, o_ref, s_scr, first):
    @pl.when(first)
    def _():
        s_scr[...] = _silu(cond_ref[...]).astype(BF16)

    mod = jnp.dot(s_scr[...], w_ref[...].astype(BF16), preferred_element_type=F32)
    o_ref[...] = mod[0:2, :] + b_ref[...]


GP = 8


def _s5prep_kernel(are_ref, aim_ref, ldt_ref, cre_ref, cim_ref, bre_ref, bim_ref, d_ref,
                   win_ref, wglu_ref, cond_ref, wada_ref, bada_ref,
                   w1t_ref, w2_ref, acoef_ref, wrow_ref, wt_ref, wgt_ref, mod_ref, kt_scr, s_scr):
    first = jnp.logical_and(pl.program_id(0) == 0, pl.program_id(1) == 0)
    _adaln_block(cond_ref, wada_ref, bada_ref, mod_ref, s_scr, first)
    _cast_in_weight(win_ref[...], wt_ref, wrow_ref)
    wgt_ref[...] = (GLU_W_SCALE * wglu_ref[...].T).astype(BF16)
    lane = lax.broadcasted_iota(jnp.int32, (1, LANES), 1)
    fwd_lane = lane < P
    krow = lax.broadcasted_iota(jnp.int32, (T, 1), 0)
    fwd_row = lax.broadcasted_iota(jnp.int32, (2 * P, 1), 0) < P
    col256 = lax.broadcasted_iota(jnp.int32, (T, TH), 1)
    r16 = lax.broadcasted_iota(jnp.int32, (T, TH), 0)
    diag_mask = (col256 % H == r16).astype(F32)
    e_rep = diag_mask.astype(BF16)
    e_blk = (col256 // H == r16).astype(BF16)

    def rows16(tab):
        return jnp.broadcast_to(tab[:, None, :], (T, H, tab.shape[1])).reshape(TH, tab.shape[1])

    def tile16(mat):
        return jnp.broadcast_to(mat[None], (T, H, mat.shape[1])).reshape(TH, mat.shape[1])

    def expand(tab, e):
        return lax.dot_general(tab.astype(BF16), e, TN_DIMS, preferred_element_type=F32)

    for gi in range(GP):
        def dircat(ref):
            return jnp.concatenate([ref[0, 0, gi], ref[0, 1, gi]], axis=-1)

        ar, ai = dircat(are_ref), dircat(aim_ref)
        dt = jnp.exp(jnp.where(fwd_lane, ldt_ref[0, 0, gi], ldt_ref[0, 1, gi]))
        la, th = ar * dt, ai * dt
        cr = tile16(dircat(cre_ref).astype(BF16))
        ci = tile16(dircat(cim_ref).astype(BF16))
        b_x = [expand(dircat(b_ref), e_rep) for b_ref in (bre_ref, bim_ref)]

        mag1 = jnp.exp(la)
        a1r, a1i = mag1 * jnp.cos(th), mag1 * jnp.sin(th)
        squares = [(a1r, a1i)]
        for _ in range(4):
            sr, si = squares[-1]
            squares.append((sr * sr - si * si, 2.0 * sr * si))

        zr, zi = a1r - 1.0, a1i
        den = ar * ar + ai * ai
        fr, fi = (zr * ar + zi * ai) / den, (zi * ar - zr * ai) / den

        def powrow(kf, kb, with_f):
            k = jnp.where(fwd_lane, kf, kb)
            if with_f:
                pr, pi = jnp.broadcast_to(fr, k.shape), jnp.broadcast_to(fi, k.shape)
            else:
                pr, pi = jnp.ones(k.shape, F32), jnp.zeros(k.shape, F32)
            for b, (sr, si) in enumerate(squares):
                bit = ((k >> b) & 1) == 1
                pr, pi = jnp.where(bit, pr * sr - pi * si, pr), jnp.where(bit, pr * si + pi * sr, pi)
            return pr, pi

        def ctab(kf, kb, with_f):
            pr, pi = powrow(kf, kb, with_f)
            pr, pi = rows16(pr.astype(BF16)), rows16(pi.astype(BF16))
            return cr * pr - ci * pi, cr * pi + ci * pr

        bbr_x, bbi_x = b_x
        ckr, cki = ctab(krow, (T - 1) - krow, True)
        br, bi = bbr_x.astype(BF16), (-bbi_x).astype(BF16)
        zero = jnp.zeros_like(br)
        rhs = jnp.concatenate(
            [jnp.concatenate([jnp.where(fwd_row, b, zero), jnp.where(fwd_row, zero, b)], axis=1) for b in (br, bi)],
            axis=0)
        kfb = jnp.dot(jnp.concatenate([ckr, cki], axis=1), rhs, preferred_element_type=F32)
        kf, kb = kfb[:, 0:TH], kfb[:, TH:2 * TH]
        kt_scr[0:TH - H, :] = kb[0:TH - H]
        kt_scr[TH - H:TH, :] = kb[TH - H:TH] + kf[0:H] + diag_mask * d_ref[0, gi:gi + 1, :]
        kt_scr[TH:2 * TH - H, :] = kf[H:TH]
        for r0 in range(0, TH, LANES):
            for c0 in range(0, TH, LANES):
                blk = (lax.broadcasted_iota(jnp.int32, (LANES, LANES), 1) + c0) // H
                m = jnp.zeros((LANES, LANES), F32)
                for s in range(c0 // H, (c0 + LANES) // H):
                    w0 = (T - 1 - s) * H + r0
                    m = jnp.where(blk == s, kt_scr[w0:w0 + LANES, c0:c0 + LANES], m)
                w1t_ref[0, gi, r0:r0 + LANES, c0:c0 + LANES] = m.astype(BF16)

        pir, pii = powrow((T - 1) - krow, krow, True)
        pxr, pxi = expand(pir, e_blk), expand(pii, e_blk)
        w1t_ref[0, gi, TH:TH + 2 * P, :] = (pxr * bbr_x - pxi * bbi_x).astype(BF16)
        w1t_ref[0, gi, TH + 2 * P:2 * TH, :] = (pxr * bbi_x + pxi * bbr_x).astype(BF16)

        cor, coi = ctab(krow + 1, T - krow, False)
        w2_ref[0, gi, :, 0:LANES] = cor.astype(BF16)
        w2_ref[0, gi, :, LANES:2 * LANES] = (-coi).astype(BF16)

        acoef_ref[0, 0, gi:gi + 1, :] = squares[4][0]
        acoef_ref[0, 1, gi:gi + 1, :] = squares[4][1]


def _s5prep(a_re, a_im, log_dt, b_re, b_im, c_re, c_im, d_skip, w_in, w_glu, cond, w_ada, b_ada):
    depth = a_re.shape[0]
    nsteps = depth * (G // GP)
    wr = D // nsteps
    gr = DS // (G // GP)
    n3 = w_ada.shape[2]
    mc = n3 // (G // GP)
    step = lambda l, g: l * (G // GP) + g
    d_cols = jnp.tile(d_skip.reshape(depth, G, H), (1, 1, T))

    dir_spec = lambda r, n: pl.BlockSpec((1, 2, GP, r, n), lambda l, g: (l, 0, g, 0, 0))
    return pl.pallas_call(
        _s5prep_kernel,
        grid=(depth, G // GP),
        in_specs=[dir_spec(1, P), dir_spec(1, P), dir_spec(1, 1), dir_spec(H, P), dir_spec(H, P),
                  dir_spec(H, P), dir_spec(H, P),
                  pl.BlockSpec((1, GP, TH), lambda l, g: (l, g, 0)),
                  pl.BlockSpec((None, wr, 6 * DS), lambda l, g: (0, step(l, g), 0)),
                  pl.BlockSpec((None, gr, DS), lambda l, g: (l, g, 0)),
                  pl.BlockSpec((COND_ROWS, D), lambda l, g: (0, 0)),
                  pl.BlockSpec((None, D, mc), lambda l, g: (l, 0, g)),
                  pl.BlockSpec((None, 1, mc), lambda l, g: (l, 0, g))],
        out_specs=[pl.BlockSpec((1, GP, 2 * TH, TH), lambda l, g: (l, g, 0, 0)),
                   pl.BlockSpec((1, GP, TH, TH), lambda l, g: (l, g, 0, 0)),
                   pl.BlockSpec((1, 2, GP, 2 * P), lambda l, g: (l, 0, g, 0)),
                   pl.BlockSpec((None, wr, 4 * DS), lambda l, g: (0, step(l, g), 0)),
                   pl.BlockSpec((None, 2 * DS, wr), lambda l, g: (0, 0, step(l, g))),
                   pl.BlockSpec((None, DS, gr), lambda l, g: (l, 0, g)),
                   pl.BlockSpec((None, 2, mc), lambda l, g: (l, 0, g))],
        out_shape=[jax.ShapeDtypeStruct((depth, G, 2 * TH, TH), BF16),
                   jax.ShapeDtypeStruct((depth, G, TH, TH), BF16),
                   jax.ShapeDtypeStruct((depth, 2, G, 2 * P), F32),
                   jax.ShapeDtypeStruct((1, D, 4 * DS), BF16),
                   jax.ShapeDtypeStruct((1, 2 * DS, D), BF16),
                   jax.ShapeDtypeStruct((depth, DS, DS), BF16),
                   jax.ShapeDtypeStruct((depth, 2, n3), F32)],
        scratch_shapes=[pltpu.VMEM((2 * TH, TH), F32), pltpu.VMEM((COND_ROWS, D), BF16)],
        compiler_params=_cparams(("arbitrary", "arbitrary")),
        name="prep",
    )(a_re[:, :, :, None, :], a_im[:, :, :, None, :], log_dt[:, :, :, None, None], c_re, c_im,
      jnp.swapaxes(b_re, 3, 4), jnp.swapaxes(b_im, 3, 4),
      d_cols, w_in, w_glu, cond, w_ada, b_ada.reshape(depth, 1, n3))


def _plane_copies(hbm3, buf, sem, step, slot, *, cb, tb, ntt, to_hbm):
    b, tt = step // ntt, step % ntt
    copies = []
    for t in range(tb):
        hbm = hbm3.at[pl.ds(b * cb, cb), tt * tb + t, :]
        vm = buf.at[slot, t]
        copies.append(pltpu.make_async_copy(vm, hbm, sem.at[slot, t]) if to_hbm
                      else pltpu.make_async_copy(hbm, vm, sem.at[slot, t]))
    return copies


def _fetch_planes(x_hbm, xbuf, xsem, step, nsteps, **kw):
    slot = step % 2
    mk = functools.partial(_plane_copies, x_hbm, xbuf, xsem, to_hbm=False, **kw)

    @pl.when(step == 0)
    def _():
        for cp in mk(step, slot):
            cp.start()

    @pl.when(step + 1 < nsteps)
    def _():
        for cp in mk(step + 1, 1 - slot):
            cp.start()

    for cp in mk(step, slot):
        cp.wait()
    return slot


def _layernorm_rows(v, eps=LN_EPS):
    mu = jnp.mean(v, axis=-1, keepdims=True)
    vc = v - mu
    var = jnp.mean(vc * vc, axis=-1, keepdims=True)
    return vc * lax.rsqrt(var + eps)


def _store_short_planes(ut, ut_ref, cb, tb):
    lane = lax.broadcasted_iota(jnp.int32, (1, LANES), 1)
    for t in range(tb):
        tile = ut[:, (t * cb) // LANES * LANES:((t * cb) // LANES + 1) * LANES]
        off = (t * cb) % LANES
        piece = pltpu.roll(tile, LANES - off, 1) if off else tile
        ut_ref[t] = jnp.where(lane < cb, piece, 0.0).astype(BF16)


def _weight_slab_copies(wt_hbm, wrow_hbm, wt_buf, wrow_buf, wsem):
    copies = [pltpu.make_async_copy(wt_hbm.at[0, pl.ds(k * DS, DS), :], wt_buf.at[k], wsem.at[k])
              for k in range(2)]
    return copies + [pltpu.make_async_copy(wrow_hbm.at[0, :, pl.ds(k * DS, DS)], wrow_buf.at[k], wsem.at[2 + k])
                     for k in range(4)]


def _inproj_kernel(*refs, cb, tb, nb, ntt, u_only, cast_next, stream_w):
    weights = None
    if stream_w:
        (x_hbm, mod_ref, wt_hbm, wrow_hbm, ut_ref, szt_ref, s_ref, bg_ref, szc_ref,
         xbuf, xsem, h_scr, wt_buf, wrow_buf, wsem) = refs
        slabs = _weight_slab_copies(wt_hbm, wrow_hbm, wt_buf, wrow_buf, wsem)
        for k in (0, 1, 2, 4, 3, 5):
            slabs[k].start()

        def weights(k):
            slabs[k].wait()
            return wt_buf[k] if k < 2 else wrow_buf[k - 2]
    elif u_only:
        x_hbm, mod_ref, wut_ref, ut_ref, xbuf, xsem, h_scr = refs
    elif cast_next:
        (x_hbm, mod_ref, wut_ref, wzt_ref, wv_ref, wb_ref, wc_ref, wzc_ref, wnext_ref,
         ut_ref, szt_ref, s_ref, bg_ref, szc_ref, wrow_next_ref, wt_next_ref, xbuf, xsem, h_scr) = refs
        _cast_in_weight(wnext_ref[...], wt_next_ref, wrow_next_ref)
    else:
        (x_hbm, mod_ref, wut_ref, wzt_ref, wv_ref, wb_ref, wc_ref, wzc_ref,
         ut_ref, szt_ref, s_ref, bg_ref, szc_ref, xbuf, xsem, h_scr) = refs
    step = pl.program_id(0) * ntt + pl.program_id(1)
    slot = _fetch_planes(x_hbm, xbuf, xsem, step, nb * ntt, cb=cb, tb=tb, ntt=ntt)
    shift, scale = mod_ref[0:1, :], mod_ref[1:2, :]
    for t in range(tb):
        hn = _layernorm_rows(xbuf[slot, t]) * (1.0 + scale) + shift
        h_scr[t * cb:(t + 1) * cb, :] = hn.astype(BF16)
    h = h_scr[...]
    if weights is None:
        resident = (wut_ref,) if u_only else (wut_ref, wzt_ref, wv_ref, wb_ref, wc_ref, wzc_ref)
        weights = lambda k: resident[k][...]
    ut = lax.dot_general(weights(0), h, NT_DIMS, preferred_element_type=F32)
    if cb == LANES:
        ut_ref[...] = ut.astype(BF16)
    else:
        _store_short_planes(ut, ut_ref, cb, tb)
    if u_only:
        return
    szt_ref[...] = _silu_of_twice(lax.dot_general(weights(1), h, NT_DIMS,
                                                  preferred_element_type=F32)).astype(BF16)
    v = jnp.dot(h, weights(2), preferred_element_type=F32)
    cg = jnp.dot(h, weights(4), preferred_element_type=F32)
    s_ref[...] = (cg * v).astype(BF16)
    bg_ref[...] = jnp.dot(h, weights(3), preferred_element_type=F32).astype(BF16)
    szc_ref[...] = _silu_of_twice(jnp.dot(h, weights(5), preferred_element_type=F32)).astype(BF16)


def _inproj(x3, mod, wt, wrow, mod_idx, cb, tb, u_only=False, w_next=None, next_layer=0):
    nc = x3.shape[0]
    n = nc * T
    nb, ntt = nc // cb, T // tb
    r = cb * tb
    col = lambda b, tt: (0, b * ntt + tt)
    row = lambda b, tt: (b * ntt + tt, 0)
    in_specs = [pl.BlockSpec(memory_space=pl.ANY), _layer_spec((3, D), mod_idx, 0, 0),
                _layer_spec((DS, D), 0, 0, 0)]
    if cb == LANES:
        out_specs = [pl.BlockSpec((DS, r), col)]
        out_shape = [jax.ShapeDtypeStruct((DS, n), BF16)]
    else:
        assert nb == 1 and ntt == 1 and LANES % cb == 0
        out_specs = [pl.BlockSpec((T, DS, LANES), lambda b, tt: (0, 0, 0))]
        out_shape = [jax.ShapeDtypeStruct((T, DS, LANES), BF16)]
    args = [x3, mod, wt]
    scratch = [pltpu.VMEM((2, tb, cb, D), F32), pltpu.SemaphoreType.DMA((2, tb)), pltpu.VMEM((r, D), BF16)]
    stream_w = nb * ntt == 1 and not u_only and w_next is None
    if stream_w:
        in_specs[2:] = [pl.BlockSpec(memory_space=pl.ANY)] * 2
        args += [wrow]
        scratch += [pltpu.VMEM((2, DS, D), BF16), pltpu.VMEM((4, D, DS), BF16), pltpu.SemaphoreType.DMA((6,))]
    if not u_only:
        if not stream_w:
            in_specs += [_layer_spec((DS, D), 0, 1, 0)] + [_layer_spec((D, DS), 0, 0, k) for k in range(4)]
            args += [wt, wrow, wrow, wrow, wrow]
        out_specs += [pl.BlockSpec((DS, r), col)] + [pl.BlockSpec((r, DS), row)] * 3
        out_shape += [jax.ShapeDtypeStruct((DS, n), BF16)] + [jax.ShapeDtypeStruct((n, DS), BF16)] * 3
    if w_next is not None:
        wr = D // (nb * ntt)
        in_specs += [pl.BlockSpec((None, wr, 6 * DS), lambda b, tt: (next_layer, b * ntt + tt, 0))]
        out_specs += [pl.BlockSpec((None, wr, 4 * DS), lambda b, tt: (0, b * ntt + tt, 0)),
                      pl.BlockSpec((None, 2 * DS, wr), lambda b, tt: (0, 0, b * ntt + tt))]
        out_shape += [jax.ShapeDtypeStruct((1, D, 4 * DS), BF16), jax.ShapeDtypeStruct((1, 2 * DS, D), BF16)]
        args += [w_next]
    return pl.pallas_call(
        functools.partial(_inproj_kernel, cb=cb, tb=tb, nb=nb, ntt=ntt, u_only=u_only,
                          cast_next=w_next is not None, stream_w=stream_w),
        grid=(nb, ntt),
        in_specs=in_specs,
        out_specs=out_specs,
        out_shape=out_shape,
        scratch_shapes=scratch,
        compiler_params=_cparams(("arbitrary", "arbitrary")),
        name="inproj_u" if u_only else "inproj",
    )(*args)


GB = 8
NCL = 512
NCC = 16
NCT = NCL + NCC
NCP = 640
CBL = 128
TB_LATENT = 4


def _s5_kernel(ut_ref, utc_ref, w1t_ref, w2_ref, acoef_ref, wo_ref, yt_ref, ytc_ref, wo_bf_ref,
               sre_scr, sim_scr, hre_scr, him_scr, yloc_scr, *, ctx_out):
    nbl = NCL // CBL
    s5_rows = pl.program_id(0) < DS // wo_ref.shape[0]
    wo_bf_ref[...] = (jnp.where(s5_rows, O5_SCALE, 1.0) * wo_ref[...]).astype(BF16)
    for gl in range(GB):
        rows = slice(gl * H, (gl + 1) * H)
        planes = []
        for t in range(T):
            lat = [ut_ref[rows, (b * T + t) * CBL:(b * T + t + 1) * CBL] for b in range(nbl)]
            planes.append(jnp.concatenate(lat + [utc_ref[t, rows, :]], axis=1))
        z = jnp.concatenate(planes, axis=0)
        r1 = jnp.dot(w1t_ref[gl], z, preferred_element_type=F32)
        yloc_scr[gl] = r1[0:TH]
        st = r1[TH:2 * TH].T
        sre_scr[pl.ds(gl, NCP, stride=GB), :] = st[:, 0:LANES]
        sim_scr[pl.ds(gl, NCP, stride=GB), :] = st[:, LANES:2 * LANES]

    ar, ai = acoef_ref[0], acoef_ref[1]
    fwd_lane = lax.broadcasted_iota(jnp.int32, (GB, LANES), 1) < P
    bwd_lane = jnp.logical_not(fwd_lane)
    hre_scr[NCT * GB:NCP * GB, :] = jnp.zeros(((NCP - NCT) * GB, LANES), F32)
    him_scr[NCT * GB:NCP * GB, :] = jnp.zeros(((NCP - NCT) * GB, LANES), F32)

    def step(k, carry, fwd_shift):
        hr, hi = carry
        cf = k + fwd_shift
        cb_ = NCT - 1 - k
        rf = pl.ds(pl.multiple_of(cf * GB, GB), GB)
        rb = pl.ds(pl.multiple_of(cb_ * GB, GB), GB)
        pltpu.store(hre_scr.at[rf, :], hr, mask=fwd_lane)
        pltpu.store(hre_scr.at[rb, :], hr, mask=bwd_lane)
        pltpu.store(him_scr.at[rf, :], hi, mask=fwd_lane)
        pltpu.store(him_scr.at[rb, :], hi, mask=bwd_lane)
        sr = jnp.where(fwd_lane, sre_scr[rf, :], sre_scr[rb, :])
        si = jnp.where(fwd_lane, sim_scr[rf, :], sim_scr[rb, :])
        return ar * hr - ai * hi + sr, ar * hi + ai * hr + si

    zero = jnp.zeros((GB, LANES), F32)
    carry = lax.fori_loop(0, NCC, functools.partial(step, fwd_shift=NCL), (zero, zero))
    lax.fori_loop(NCC, NCT, functools.partial(step, fwd_shift=-NCC), carry, unroll=4)

    for gl in range(GB):
        rows = slice(gl * H, (gl + 1) * H)
        sel = pl.ds(gl, NCP, stride=GB)
        hp = jnp.concatenate([hre_scr[sel, :], him_scr[sel, :]], axis=1)
        y = yloc_scr[gl] + lax.dot_general(w2_ref[gl], hp.astype(BF16), NT_DIMS, preferred_element_type=F32)
        yb = y.astype(BF16)
        for t in range(T):
            for b in range(nbl):
                yt_ref[rows, (b * T + t) * CBL:(b * T + t + 1) * CBL] = \
                    yb[t * H:(t + 1) * H, b * CBL:(b + 1) * CBL]
            if ctx_out:
                ytc_ref[t, rows, :] = yb[t * H:(t + 1) * H, NCL:NCP]
    if not ctx_out:
        ytc_ref[...] = jnp.zeros(ytc_ref.shape, BF16)


def _s5(ut, utc, w1t, w2, acoef, w_out, layer, ctx_out):
    n = ut.shape[1]
    ncx = NCP - NCL
    wr = 2 * DS // (G // GB)
    state = pltpu.VMEM((NCP * GB, LANES), F32)
    return pl.pallas_call(
        functools.partial(_s5_kernel, ctx_out=ctx_out),
        grid=(G // GB,),
        in_specs=[pl.BlockSpec((GB * H, n), lambda i: (i, 0)),
                  pl.BlockSpec((T, GB * H, ncx), lambda i: (0, i, 0)),
                  pl.BlockSpec((None, GB, 2 * TH, TH), lambda i: (layer, i, 0, 0)),
                  pl.BlockSpec((None, GB, TH, TH), lambda i: (layer, i, 0, 0)),
                  pl.BlockSpec((None, 2, GB, LANES), lambda i: (layer, 0, i, 0)),
                  pl.BlockSpec((None, wr, D), lambda i: (layer, i, 0))],
        out_specs=[pl.BlockSpec((GB * H, n), lambda i: (i, 0)),
                   pl.BlockSpec((T, GB * H, ncx), lambda i: (0, i, 0)),
                   pl.BlockSpec((None, wr, D), lambda i: (0, i, 0))],
        out_shape=[jax.ShapeDtypeStruct((DS, n), BF16), jax.ShapeDtypeStruct((T, DS, ncx), BF16),
                   jax.ShapeDtypeStruct((1, 2 * DS, D), BF16)],
        scratch_shapes=[state, state, state, state, pltpu.VMEM((GB, TH, NCP), F32)],
        compiler_params=_cparams(("parallel",)),
        name="s5",
    )(ut, utc, w1t, w2, acoef, w_out)


def _outproj_kernel(*refs, cb, tb, nb, ntt, row_chunks, alpha, next_u):
    (x_hbm, mod_ref, yt_ref, szt_ref, s_ref, sprev_ref, snext_ref, bg_ref, szc_ref,
     wgt_ref, bglu_ref, cw_ref, cb_ref, wo_ref, lng_ref, lnb_ref) = refs[:16]
    if next_u:
        modn_ref, wutn_ref, o_hbm, utn_ref, xbuf, xsem, obuf, osem = refs[16:]
    else:
        o_hbm, xbuf, xsem, obuf, osem = refs[16:]
    tt = pl.program_id(1)
    step = pl.program_id(0) * ntt + tt
    nsteps = nb * ntt
    slot = _fetch_planes(x_hbm, xbuf, xsem, step, nsteps, cb=cb, tb=tb, ntt=ntt)
    if cb == LANES:
        y = yt_ref[...].astype(F32)
    else:
        lane = lax.broadcasted_iota(jnp.int32, (1, LANES), 1)
        per = LANES // cb
        tiles = []
        for q in range(tb // per):
            acc = jnp.where(lane < cb, yt_ref[q * per].astype(F32), 0.0)
            for k in range(1, per):
                piece = jnp.where(lane < cb, yt_ref[q * per + k].astype(F32), 0.0)
                acc = acc + pltpu.roll(piece, k * cb, 1)
            tiles.append(acc)
        y = jnp.concatenate(tiles, axis=1)
    g2 = y * jnp.tanh(y * (GELU_C1 + GELU_C2 * (y * y))) + y
    half_gate = jnp.dot(wgt_ref[...], g2.astype(BF16), preferred_element_type=F32) + bglu_ref[...]
    o5 = (g2 * szt_ref[...].astype(F32) * (jnp.tanh(half_gate) + 1.0)).T.astype(BF16)
    crow = lax.broadcasted_iota(jnp.int32, (cb, 1), 0) % row_chunks
    sp = sprev_ref[...].astype(F32)
    sn = snext_ref[...].astype(F32)
    sp_wrap = jnp.where(crow != 0, pltpu.roll(sp, 1, 0), 0.0)
    sn_wrap = jnp.where(crow != row_chunks - 1, pltpu.roll(sn, cb - 1, 0), 0.0)
    sp = jnp.where(tt == 0, sp_wrap, sp).astype(BF16)
    sn = jnp.where(tt == ntt - 1, sn_wrap, sn).astype(BF16)
    s_all = jnp.concatenate([sp, s_ref[...], sn], axis=0)
    r = cb * tb
    cw = cw_ref[...].astype(BF16)
    conv = (cw[0:1, :] * s_all[0:r] + cw[1:2, :] * s_all[cb:cb + r]
            + cw[2:3, :] * s_all[2 * cb:2 * cb + r] + cb_ref[...].astype(BF16))
    ocv = bg_ref[...] * conv * szc_ref[...]
    sub = jnp.dot(jnp.concatenate([o5, ocv], axis=1), wo_ref[...], preferred_element_type=F32)
    gate_vec = mod_ref[2:3, :] * (1.0 / alpha)
    for t in range(tb):
        v = xbuf[slot, t] + gate_vec * sub[t * cb:(t + 1) * cb]
        obuf[slot, t] = _layernorm_rows(v, LN_EPS / (alpha * alpha)) * lng_ref[...] + lnb_ref[...]
    if next_u:
        shift, scale = modn_ref[0:1, :], modn_ref[1:2, :]
        hn = jnp.concatenate([(_layernorm_rows(obuf[slot, t]) * (1.0 + scale) + shift).astype(BF16)
                              for t in range(tb)], axis=0)
        _store_short_planes(lax.dot_general(wutn_ref[...], hn, NT_DIMS, preferred_element_type=F32),
                            utn_ref, cb, tb)
    put = functools.partial(_plane_copies, o_hbm, obuf, osem, cb=cb, tb=tb, ntt=ntt, to_hbm=True)

    @pl.when(step > 0)
    def _():
        for cp in put(step - 1, 1 - slot):
            cp.wait()

    for cp in put(step, slot):
        cp.start()

    @pl.when(step == nsteps - 1)
    def _():
        for cp in put(step, slot):
            cp.wait()


def _outproj(x3, mod, yt, szt, s, bg, szc, wo, wts, layer, mod_idx, cb, tb, row_chunks, alpha, next_u=None):
    nc = x3.shape[0]
    nb, ntt = nc // cb, T // tb
    r = cb * tb
    extra_in, extra_out, extra_shape, extra_args = [], [], [], []
    if next_u is not None:
        assert cb != LANES and nb == 1 and ntt == 1
        extra_in = [_layer_spec((3, D), next_u[0], 0, 0), _layer_spec((DS, D), 0, 0, 0)]
        extra_out = [pl.BlockSpec((T, DS, LANES), lambda b, tt: (0, 0, 0))]
        extra_shape = [jax.ShapeDtypeStruct((T, DS, LANES), BF16)]
        extra_args = [mod, next_u[1]]
    col = lambda b, tt: (0, b * ntt + tt)
    row = lambda b, tt: (b * ntt + tt, 0)
    prev_plane = lambda b, tt: (b * T + (tt * tb + T - 1) % T, 0)
    next_plane = lambda b, tt: (b * T + (tt * tb + tb) % T, 0)
    wgt, bglu, cw, cbias, lng, lnb = wts
    planes = pltpu.VMEM((2, tb, cb, D), F32)
    return pl.pallas_call(
        functools.partial(_outproj_kernel, cb=cb, tb=tb, nb=nb, ntt=ntt, row_chunks=row_chunks, alpha=alpha,
                          next_u=next_u is not None),
        grid=(nb, ntt),
        in_specs=[pl.BlockSpec(memory_space=pl.ANY),
                  _layer_spec((3, D), mod_idx, 0, 0),
                  pl.BlockSpec((DS, r), col) if cb == LANES else pl.BlockSpec((T, DS, LANES), lambda b, tt: (0, 0, 0)),
                  pl.BlockSpec((DS, r), col),
                  pl.BlockSpec((r, DS), row),
                  pl.BlockSpec((cb, DS), prev_plane), pl.BlockSpec((cb, DS), next_plane),
                  pl.BlockSpec((r, DS), row), pl.BlockSpec((r, DS), row),
                  _layer_spec((DS, DS), layer, 0, 0), _layer_spec((DS, 1), layer, 0, 0),
                  _layer_spec((3, DS), layer, 0, 0), _layer_spec((1, DS), layer, 0, 0),
                  _layer_spec((2 * DS, D), 0, 0, 0),
                  _layer_spec((1, D), layer, 0, 0), _layer_spec((1, D), layer, 0, 0)] + extra_in,
        out_specs=[pl.BlockSpec(memory_space=pl.ANY)] + extra_out,
        out_shape=[jax.ShapeDtypeStruct(x3.shape, F32)] + extra_shape,
        scratch_shapes=[planes, pltpu.SemaphoreType.DMA((2, tb)), planes, pltpu.SemaphoreType.DMA((2, tb))],
        compiler_params=_cparams(("arbitrary", "arbitrary")),
        name="outproj",
    )(x3, mod, yt, szt, s, s, s, bg, szc, wgt, bglu, cw, cbias, wo, lng, lnb, *extra_args)


def kernel(x, c, ctx, c_ctx, w_ada, b_ada, w_in, s5_a_re, s5_a_im, s5_log_dt, s5_b_re, s5_b_im,
           s5_c_re, s5_c_im, s5_d, w_glu, b_glu, conv_w, conv_b, w_out, ln_g, ln_b):
    bsz, n, d = x.shape
    n_ctx = ctx.shape[1]
    depth = w_in.shape[0]
    assert bsz == 1 and d == D and n == NCL * T and n_ctx == NCC * T
    alpha = (2.0 * depth) ** 0.25

    cond = jnp.pad(jnp.stack([c[0], c_ctx]), ((0, COND_ROWS - 2), (0, 0)))
    w1t, w2, acoef, w_row, w_t, w_glu_t, mod = _s5prep(s5_a_re, s5_a_im, s5_log_dt, s5_b_re, s5_b_im,
                                                       s5_c_re, s5_c_im, s5_d, w_in, w_glu, cond, w_ada, b_ada)
    mod = mod.reshape(depth * 2, 3, D)
    out_w = (w_glu_t, GLU_B_SCALE * b_glu.reshape(depth, DS, 1), conv_w, conv_b.reshape(depth, 1, DS),
             ln_g.reshape(depth, 1, D), ln_b.reshape(depth, 1, D))

    x3 = x.reshape(n // T, T, D)
    c3 = ctx.reshape(n_ctx // T, T, D)
    utc = None
    for i in range(depth):
        last = i == depth - 1
        ut, szt, s, bg, szc, *w_next = _inproj(x3, mod, w_t, w_row, 2 * i, CBL, TB_LATENT,
                                               w_next=None if last else w_in, next_layer=i + 1)
        if utc is None:
            utc, *ctx_rest = _inproj(c3, mod, w_t, w_row, 2 * i + 1, NCC, T, u_only=last)
        yt, ytc, wo = _s5(ut, utc, w1t, w2, acoef, w_out, i, ctx_out=not last)
        utc = None
        (x3,) = _outproj(x3, mod, yt, szt, s, bg, szc, wo, out_w, i, 2 * i, CBL, TB_LATENT, GRID_W // T, alpha)
        if not last:
            w_row, w_t = w_next
            next_u = (2 * (i + 1) + 1, w_t) if i + 1 == depth - 1 else None
            c3, *next_utc = _outproj(c3, mod, ytc, *ctx_rest, wo, out_w, i, 2 * i + 1, NCC, T, NCC, alpha,
                                     next_u=next_u)
            utc = next_utc[0] if next_utc else None
    return x3.reshape(bsz, n, D)
```

```python
import functools
import math

import jax
import jax.numpy as jnp
from jax import lax
from jax.experimental import pallas as pl
from jax.experimental.pallas import tpu as pltpu

D = 2048
DS = 1024
G = 64
H = 16
P = 64
T = 16
TH = T * H
GRID_W = 64
LN_EPS = 1e-6
LANES = 128
COND_ROWS = 16
VMEM_LIMIT = 60 * 1024 * 1024

F32 = jnp.float32
BF16 = jnp.bfloat16
NT_DIMS = (((1,), (1,)), ((), ()))
TN_DIMS = (((0,), (0,)), ((), ()))


def _cparams(sem):
    return pltpu.CompilerParams(dimension_semantics=sem, vmem_limit_bytes=VMEM_LIMIT)


def _layer_spec(shape, *idx):
    return pl.BlockSpec((None,) + shape, lambda *_: idx, pipeline_mode=pl.Buffered(1))


GELU_C1 = math.sqrt(2.0 / math.pi)
GELU_C2 = 0.044715 * GELU_C1
GLU_W_SCALE = 0.25
GLU_B_SCALE = 0.5
O5_SCALE = 0.25


def _silu_of_twice(h):
    return h * jnp.tanh(h) + h


def _silu(v):
    return _silu_of_twice(0.5 * v)


def _cast_in_weight(wn, wt_ref, wrow_ref):
    wt_ref[0:DS, :] = wn[:, 0:DS].T.astype(BF16)
    wt_ref[DS:2 * DS, :] = (0.5 * wn[:, DS:2 * DS]).T.astype(BF16)
    wrow_ref[:, 0:3 * DS] = wn[:, 2 * DS:5 * DS].astype(BF16)
    wrow_ref[:, 3 * DS:4 * DS] = (0.5 * wn[:, 5 * DS:6 * DS]).astype(BF16)


def _adaln_block(cond_ref, w_ref, b_ref, o_ref, s_scr, first):
    @pl.when(first)
    def _():
        s_scr[...] = _silu(cond_ref[...]).astype(BF16)

    mod = jnp.dot(s_scr[...], w_ref[...].astype(BF16), preferred_element_type=F32)
    o_ref[...] = mod[0:2, :] + b_ref[...]


GP = 8


def _s5prep_kernel(are_ref, aim_ref, ldt_ref, cre_ref, cim_ref, bre_ref, bim_ref, d_ref,
                   win_ref, wglu_ref, cond_ref, wada_ref, bada_ref,
                   w1t_ref, w2_ref, acoef_ref, wrow_ref, wt_ref, wgt_ref, mod_ref, kt_scr, s_scr):
    first = jnp.logical_and(pl.program_id(0) == 0, pl.program_id(1) == 0)
    _adaln_block(cond_ref, wada_ref, bada_ref, mod_ref, s_scr, first)
    _cast_in_weight(win_ref[...], wt_ref, wrow_ref)
    wgt_ref[...] = (GLU_W_SCALE * wglu_ref[...].T).astype(BF16)
    lane = lax.broadcasted_iota(jnp.int32, (1, LANES), 1)
    fwd_lane = lane < P
    krow = lax.broadcasted_iota(jnp.int32, (T, 1), 0)
    fwd_row = lax.broadcasted_iota(jnp.int32, (2 * P, 1), 0) < P
    col256 = lax.broadcasted_iota(jnp.int32, (T, TH), 1)
    r16 = lax.broadcasted_iota(jnp.int32, (T, TH), 0)
    diag_mask = (col256 % H == r16).astype(F32)
    e_rep = diag_mask.astype(BF16)
    e_blk = (col256 // H == r16).astype(BF16)

    def rows16(tab):
        return jnp.broadcast_to(tab[:, None, :], (T, H, tab.shape[1])).reshape(TH, tab.shape[1])

    def tile16(mat):
        return jnp.broadcast_to(mat[None], (T, H, mat.shape[1])).reshape(TH, mat.shape[1])

    def expand(tab, e):
        return lax.dot_general(tab.astype(BF16), e, TN_DIMS, preferred_element_type=F32)

    for gi in range(GP):
        def dircat(ref):
            return jnp.concatenate([ref[0, 0, gi], ref[0, 1, gi]], axis=-1)

        ar, ai = dircat(are_ref), dircat(aim_ref)
        dt = jnp.exp(jnp.where(fwd_lane, ldt_ref[0, 0, gi], ldt_ref[0, 1, gi]))
        la, th = ar * dt, ai * dt
        cr = tile16(dircat(cre_ref).astype(BF16))
        ci = tile16(dircat(cim_ref).astype(BF16))
        b_x = [expand(dircat(b_ref), e_rep) for b_ref in (bre_ref, bim_ref)]

        mag1 = jnp.exp(la)
        a1r, a1i = mag1 * jnp.cos(th), mag1 * jnp.sin(th)
        squares = [(a1r, a1i)]
        for _ in range(4):
            sr, si = squares[-1]
            squares.append((sr * sr - si * si, 2.0 * sr * si))

        zr, zi = a1r - 1.0, a1i
        den = ar * ar + ai * ai
        fr, fi = (zr * ar + zi * ai) / den, (zi * ar - zr * ai) / den

        def powrow(kf, kb, with_f):
            k = jnp.where(fwd_lane, kf, kb)
            if with_f:
                pr, pi = jnp.broadcast_to(fr, k.shape), jnp.broadcast_to(fi, k.shape)
            else:
                pr, pi = jnp.ones(k.shape, F32), jnp.zeros(k.shape, F32)
            for b, (sr, si) in enumerate(squares):
                bit = ((k >> b) & 1) == 1
                pr, pi = jnp.where(bit, pr * sr - pi * si, pr), jnp.where(bit, pr * si + pi * sr, pi)
            return pr, pi

        def ctab(kf, kb, with_f):
            pr, pi = powrow(kf, kb, with_f)
            pr, pi = rows16(pr.astype(BF16)), rows16(pi.astype(BF16))
            return cr * pr - ci * pi, cr * pi + ci * pr

        bbr_x, bbi_x = b_x
        ckr, cki = ctab(krow, (T - 1) - krow, True)
        br, bi = bbr_x.astype(BF16), (-bbi_x).astype(BF16)
        zero = jnp.zeros_like(br)
        rhs = jnp.concatenate(
            [jnp.concatenate([jnp.where(fwd_row, b, zero), jnp.where(fwd_row, zero, b)], axis=1) for b in (br, bi)],
            axis=0)
        kfb = jnp.dot(jnp.concatenate([ckr, cki], axis=1), rhs, preferred_element_type=F32)
        kf, kb = kfb[:, 0:TH], kfb[:, TH:2 * TH]
        kt_scr[0:TH - H, :] = kb[0:TH - H]
        kt_scr[TH - H:TH, :] = kb[TH - H:TH] + kf[0:H] + diag_mask * d_ref[0, gi:gi + 1, :]
        kt_scr[TH:2 * TH - H, :] = kf[H:TH]
        for r0 in range(0, TH, LANES):
            for c0 in range(0, TH, LANES):
                blk = (lax.broadcasted_iota(jnp.int32, (LANES, LANES), 1) + c0) // H
                m = jnp.zeros((LANES, LANES), F32)
                for s in range(c0 // H, (c0 + LANES) // H):
                    w0 = (T - 1 - s) * H + r0
                    m = jnp.where(blk == s, kt_scr[w0:w0 + LANES, c0:c0 + LANES], m)
                w1t_ref[0, gi, r0:r0 + LANES, c0:c0 + LANES] = m.astype(BF16)

        pir, pii = powrow((T - 1) - krow, krow, True)
        pxr, pxi = expand(pir, e_blk), expand(pii, e_blk)
        w1t_ref[0, gi, TH:TH + 2 * P, :] = (pxr * bbr_x - pxi * bbi_x).astype(BF16)
        w1t_ref[0, gi, TH + 2 * P:2 * TH, :] = (pxr * bbi_x + pxi * bbr_x).astype(BF16)

        cor, coi = ctab(krow + 1, T - krow, False)
        w2_ref[0, gi, :, 0:LANES] = cor.astype(BF16)
        w2_ref[0, gi, :, LANES:2 * LANES] = (-coi).astype(BF16)

        acoef_ref[0, 0, gi:gi + 1, :] = squares[4][0]
        acoef_ref[0, 1, gi:gi + 1, :] = squares[4][1]


def _s5prep(a_re, a_im, log_dt, b_re, b_im, c_re, c_im, d_skip, w_in, w_glu, cond, w_ada, b_ada):
    depth = a_re.shape[0]
    nsteps = depth * (G // GP)
    wr = D // nsteps
    gr = DS // (G // GP)
    n3 = w_ada.shape[2]
    mc = n3 // (G // GP)
    step = lambda l, g: l * (G // GP) + g
    d_cols = jnp.tile(d_skip.reshape(depth, G, H), (1, 1, T))

    dir_spec = lambda r, n: pl.BlockSpec((1, 2, GP, r, n), lambda l, g: (l, 0, g, 0, 0))
    return pl.pallas_call(
        _s5prep_kernel,
        grid=(depth, G // GP),
        in_specs=[dir_spec(1, P), dir_spec(1, P), dir_spec(1, 1), dir_spec(H, P), dir_spec(H, P),
                  dir_spec(H, P), dir_spec(H, P),
                  pl.BlockSpec((1, GP, TH), lambda l, g: (l, g, 0)),
                  pl.BlockSpec((None, wr, 6 * DS), lambda l, g: (0, step(l, g), 0)),
                  pl.BlockSpec((None, gr, DS), lambda l, g: (l, g, 0)),
                  pl.BlockSpec((COND_ROWS, D), lambda l, g: (0, 0)),
                  pl.BlockSpec((None, D, mc), lambda l, g: (l, 0, g)),
                  pl.BlockSpec((None, 1, mc), lambda l, g: (l, 0, g))],
        out_specs=[pl.BlockSpec((1, GP, 2 * TH, TH), lambda l, g: (l, g, 0, 0)),
                   pl.BlockSpec((1, GP, TH, TH), lambda l, g: (l, g, 0, 0)),
                   pl.BlockSpec((1, 2, GP, 2 * P), lambda l, g: (l, 0, g, 0)),
                   pl.BlockSpec((None, wr, 4 * DS), lambda l, g: (0, step(l, g), 0)),
                   pl.BlockSpec((None, 2 * DS, wr), lambda l, g: (0, 0, step(l, g))),
                   pl.BlockSpec((None, DS, gr), lambda l, g: (l, 0, g)),
                   pl.BlockSpec((None, 2, mc), lambda l, g: (l, 0, g))],
        out_shape=[jax.ShapeDtypeStruct((depth, G, 2 * TH, TH), BF16),
                   jax.ShapeDtypeStruct((depth, G, TH, TH), BF16),
                   jax.ShapeDtypeStruct((depth, 2, G, 2 * P), F32),
                   jax.ShapeDtypeStruct((1, D, 4 * DS), BF16),
                   jax.ShapeDtypeStruct((1, 2 * DS, D), BF16),
                   jax.ShapeDtypeStruct((depth, DS, DS), BF16),
                   jax.ShapeDtypeStruct((depth, 2, n3), F32)],
        scratch_shapes=[pltpu.VMEM((2 * TH, TH), F32), pltpu.VMEM((COND_ROWS, D), BF16)],
        compiler_params=_cparams(("arbitrary", "arbitrary")),
        name="prep",
    )(a_re[:, :, :, None, :], a_im[:, :, :, None, :], log_dt[:, :, :, None, None], c_re, c_im,
      jnp.swapaxes(b_re, 3, 4), jnp.swapaxes(b_im, 3, 4),
      d_cols, w_in, w_glu, cond, w_ada, b_ada.reshape(depth, 1, n3))


def _plane_copies(hbm3, buf, sem, step, slot, *, cb, tb, ntt, to_hbm):
    b, tt = step // ntt, step % ntt
    copies = []
    for t in range(tb):
        hbm = hbm3.at[pl.ds(b * cb, cb), tt * tb + t, :]
        vm = buf.at[slot, t]
        copies.append(pltpu.make_async_copy(vm, hbm, sem.at[slot, t]) if to_hbm
                      else pltpu.make_async_copy(hbm, vm, sem.at[slot, t]))
    return copies


def _fetch_planes(x_hbm, xbuf, xsem, step, nsteps, **kw):
    slot = step % 2
    mk = functools.partial(_plane_copies, x_hbm, xbuf, xsem, to_hbm=False, **kw)

    @pl.when(step == 0)
    def _():
        for cp in mk(step, slot):
            cp.start()

    @pl.when(step + 1 < nsteps)
    def _():
        for cp in mk(step + 1, 1 - slot):
            cp.start()

    for cp in mk(step, slot):
        cp.wait()
    return slot


def _layernorm_rows(v, eps=LN_EPS):
    mu = jnp.mean(v, axis=-1, keepdims=True)
    vc = v - mu
    var = jnp.mean(vc * vc, axis=-1, keepdims=True)
    return vc * lax.rsqrt(var + eps)


def _store_short_planes(ut, ut_ref, cb, tb):
    lane = lax.broadcasted_iota(jnp.int32, (1, LANES), 1)
    for t in range(tb):
        tile = ut[:, (t * cb) // LANES * LANES:((t * cb) // LANES + 1) * LANES]
        off = (t * cb) % LANES
        piece = pltpu.roll(tile, LANES - off, 1) if off else tile
        ut_ref[t] = jnp.where(lane < cb, piece, 0.0).astype(BF16)


def _weight_slab_copies(wt_hbm, wrow_hbm, wt_buf, wrow_buf, wsem):
    copies = [pltpu.make_async_copy(wt_hbm.at[0, pl.ds(k * DS, DS), :], wt_buf.at[k], wsem.at[k])
              for k in range(2)]
    return copies + [pltpu.make_async_copy(wrow_hbm.at[0, :, pl.ds(k * DS, DS)], wrow_buf.at[k], wsem.at[2 + k])
                     for k in range(4)]


SLAB_ORDER = (0, 1, 2, 4, 3, 5)
SLABS_IN_FLIGHT = 2


def _inproj_kernel(*refs, cb, tb, nb, ntt, u_only, cast_next, stream_w):
    weights = None
    if stream_w:
        (x_hbm, mod_ref, wt_hbm, wrow_hbm, ut_ref, szt_ref, s_ref, bg_ref, szc_ref,
         xbuf, xsem, h_scr, wt_buf, wrow_buf, wsem) = refs
        slabs = _weight_slab_copies(wt_hbm, wrow_hbm, wt_buf, wrow_buf, wsem)
        for k in SLAB_ORDER[:SLABS_IN_FLIGHT]:
            slabs[k].start()

        def weights(k):
            slabs[k].wait()
            nxt = SLAB_ORDER.index(k) + SLABS_IN_FLIGHT
            if nxt < len(SLAB_ORDER):
                slabs[SLAB_ORDER[nxt]].start()
            return wt_buf[k] if k < 2 else wrow_buf[k - 2]
    elif u_only:
        x_hbm, mod_ref, wut_ref, ut_ref, xbuf, xsem, h_scr = refs
    elif cast_next:
        (x_hbm, mod_ref, wut_ref, wzt_ref, wv_ref, wb_ref, wc_ref, wzc_ref, wnext_ref,
         ut_ref, szt_ref, s_ref, bg_ref, szc_ref, wrow_next_ref, wt_next_ref, xbuf, xsem, h_scr) = refs
        _cast_in_weight(wnext_ref[...], wt_next_ref, wrow_next_ref)
    else:
        (x_hbm, mod_ref, wut_ref, wzt_ref, wv_ref, wb_ref, wc_ref, wzc_ref,
         ut_ref, szt_ref, s_ref, bg_ref, szc_ref, xbuf, xsem, h_scr) = refs
    step = pl.program_id(0) * ntt + pl.program_id(1)
    slot = _fetch_planes(x_hbm, xbuf, xsem, step, nb * ntt, cb=cb, tb=tb, ntt=ntt)
    shift, scale = mod_ref[0:1, :], mod_ref[1:2, :]
    for t in range(tb):
        hn = _layernorm_rows(xbuf[slot, t]) * (1.0 + scale) + shift
        h_scr[t * cb:(t + 1) * cb, :] = hn.astype(BF16)
    h = h_scr[...]
    if weights is None:
        resident = (wut_ref,) if u_only else (wut_ref, wzt_ref, wv_ref, wb_ref, wc_ref, wzc_ref)
        weights = lambda k: resident[k][...]
    ut = lax.dot_general(weights(0), h, NT_DIMS, preferred_element_type=F32)
    if cb == LANES:
        ut_ref[...] = ut.astype(BF16)
    else:
        _store_short_planes(ut, ut_ref, cb, tb)
    if u_only:
        return
    szt_ref[...] = _silu_of_twice(lax.dot_general(weights(1), h, NT_DIMS,
                                                  preferred_element_type=F32)).astype(BF16)
    v = jnp.dot(h, weights(2), preferred_element_type=F32)
    cg = jnp.dot(h, weights(4), preferred_element_type=F32)
    s_ref[...] = (cg * v).astype(BF16)
    bg_ref[...] = jnp.dot(h, weights(3), preferred_element_type=F32).astype(BF16)
    szc_ref[...] = _silu_of_twice(jnp.dot(h, weights(5), preferred_element_type=F32)).astype(BF16)


def _inproj(x3, mod, wt, wrow, mod_idx, cb, tb, u_only=False, w_next=None, next_layer=0):
    nc = x3.shape[0]
    n = nc * T
    nb, ntt = nc // cb, T // tb
    r = cb * tb
    col = lambda b, tt: (0, b * ntt + tt)
    row = lambda b, tt: (b * ntt + tt, 0)
    in_specs = [pl.BlockSpec(memory_space=pl.ANY), _layer_spec((3, D), mod_idx, 0, 0),
                _layer_spec((DS, D), 0, 0, 0)]
    if cb == LANES:
        out_specs = [pl.BlockSpec((DS, r), col)]
        out_shape = [jax.ShapeDtypeStruct((DS, n), BF16)]
    else:
        assert nb == 1 and ntt == 1 and LANES % cb == 0
        out_specs = [pl.BlockSpec((T, DS, LANES), lambda b, tt: (0, 0, 0))]
        out_shape = [jax.ShapeDtypeStruct((T, DS, LANES), BF16)]
    args = [x3, mod, wt]
    scratch = [pltpu.VMEM((2, tb, cb, D), F32), pltpu.SemaphoreType.DMA((2, tb)), pltpu.VMEM((r, D), BF16)]
    stream_w = nb * ntt == 1 and not u_only and w_next is None
    if stream_w:
        in_specs[2:] = [pl.BlockSpec(memory_space=pl.ANY)] * 2
        args += [wrow]
        scratch += [pltpu.VMEM((2, DS, D), BF16), pltpu.VMEM((4, D, DS), BF16), pltpu.SemaphoreType.DMA((6,))]
    if not u_only:
        if not stream_w:
            in_specs += [_layer_spec((DS, D), 0, 1, 0)] + [_layer_spec((D, DS), 0, 0, k) for k in range(4)]
            args += [wt, wrow, wrow, wrow, wrow]
        out_specs += [pl.BlockSpec((DS, r), col)] + [pl.BlockSpec((r, DS), row)] * 3
        out_shape += [jax.ShapeDtypeStruct((DS, n), BF16)] + [jax.ShapeDtypeStruct((n, DS), BF16)] * 3
    if w_next is not None:
        wr = D // (nb * ntt)
        in_specs += [pl.BlockSpec((None, wr, 6 * DS), lambda b, tt: (next_layer, b * ntt + tt, 0))]
        out_specs += [pl.BlockSpec((None, wr, 4 * DS), lambda b, tt: (0, b * ntt + tt, 0)),
                      pl.BlockSpec((None, 2 * DS, wr), lambda b, tt: (0, 0, b * ntt + tt))]
        out_shape += [jax.ShapeDtypeStruct((1, D, 4 * DS), BF16), jax.ShapeDtypeStruct((1, 2 * DS, D), BF16)]
        args += [w_next]
    return pl.pallas_call(
        functools.partial(_inproj_kernel, cb=cb, tb=tb, nb=nb, ntt=ntt, u_only=u_only,
                          cast_next=w_next is not None, stream_w=stream_w),
        grid=(nb, ntt),
        in_specs=in_specs,
        out_specs=out_specs,
        out_shape=out_shape,
        scratch_shapes=scratch,
        compiler_params=_cparams(("arbitrary", "arbitrary")),
        name="inproj_u" if u_only else "inproj",
    )(*args)


GB = 8
NCL = 512
NCC = 16
NCT = NCL + NCC
NCP = 640
CBL = 128
TB_LATENT = 4


def _s5_kernel(ut_ref, utc_ref, w1t_ref, w2_ref, acoef_ref, wo_ref, yt_ref, ytc_ref, wo_bf_ref,
               sre_scr, sim_scr, hre_scr, him_scr, yloc_scr, *, ctx_out):
    nbl = NCL // CBL
    s5_rows = pl.program_id(0) < DS // wo_ref.shape[0]
    wo_bf_ref[...] = (jnp.where(s5_rows, O5_SCALE, 1.0) * wo_ref[...]).astype(BF16)
    for gl in range(GB):
        rows = slice(gl * H, (gl + 1) * H)
        planes = []
        for t in range(T):
            lat = [ut_ref[rows, (b * T + t) * CBL:(b * T + t + 1) * CBL] for b in range(nbl)]
            planes.append(jnp.concatenate(lat + [utc_ref[t, rows, :]], axis=1))
        z = jnp.concatenate(planes, axis=0)
        r1 = jnp.dot(w1t_ref[gl], z, preferred_element_type=F32)
        yloc_scr[gl] = r1[0:TH]
        st = r1[TH:2 * TH].T
        sre_scr[pl.ds(gl, NCP, stride=GB), :] = st[:, 0:LANES]
        sim_scr[pl.ds(gl, NCP, stride=GB), :] = st[:, LANES:2 * LANES]

    ar, ai = acoef_ref[0], acoef_ref[1]
    fwd_lane = lax.broadcasted_iota(jnp.int32, (GB, LANES), 1) < P
    bwd_lane = jnp.logical_not(fwd_lane)
    hre_scr[NCT * GB:NCP * GB, :] = jnp.zeros(((NCP - NCT) * GB, LANES), F32)
    him_scr[NCT * GB:NCP * GB, :] = jnp.zeros(((NCP - NCT) * GB, LANES), F32)

    def step(k, carry, fwd_shift):
        hr, hi = carry
        cf = k + fwd_shift
        cb_ = NCT - 1 - k
        rf = pl.ds(pl.multiple_of(cf * GB, GB), GB)
        rb = pl.ds(pl.multiple_of(cb_ * GB, GB), GB)
        pltpu.store(hre_scr.at[rf, :], hr, mask=fwd_lane)
        pltpu.store(hre_scr.at[rb, :], hr, mask=bwd_lane)
        pltpu.store(him_scr.at[rf, :], hi, mask=fwd_lane)
        pltpu.store(him_scr.at[rb, :], hi, mask=bwd_lane)
        sr = jnp.where(fwd_lane, sre_scr[rf, :], sre_scr[rb, :])
        si = jnp.where(fwd_lane, sim_scr[rf, :], sim_scr[rb, :])
        return ar * hr - ai * hi + sr, ar * hi + ai * hr + si

    zero = jnp.zeros((GB, LANES), F32)
    carry = lax.fori_loop(0, NCC, functools.partial(step, fwd_shift=NCL), (zero, zero))
    lax.fori_loop(NCC, NCT, functools.partial(step, fwd_shift=-NCC), carry, unroll=4)

    for gl in range(GB):
        rows = slice(gl * H, (gl + 1) * H)
        sel = pl.ds(gl, NCP, stride=GB)
        hp = jnp.concatenate([hre_scr[sel, :], him_scr[sel, :]], axis=1)
        y = yloc_scr[gl] + lax.dot_general(w2_ref[gl], hp.astype(BF16), NT_DIMS, preferred_element_type=F32)
        yb = y.astype(BF16)
        for t in range(T):
            for b in range(nbl):
                yt_ref[rows, (b * T + t) * CBL:(b * T + t + 1) * CBL] = \
                    yb[t * H:(t + 1) * H, b * CBL:(b + 1) * CBL]
            if ctx_out:
                ytc_ref[t, rows, :] = yb[t * H:(t + 1) * H, NCL:NCP]
    if not ctx_out:
        ytc_ref[...] = jnp.zeros(ytc_ref.shape, BF16)


def _s5(ut, utc, w1t, w2, acoef, w_out, layer, ctx_out):
    n = ut.shape[1]
    ncx = NCP - NCL
    wr = 2 * DS // (G // GB)
    state = pltpu.VMEM((NCP * GB, LANES), F32)
    return pl.pallas_call(
        functools.partial(_s5_kernel, ctx_out=ctx_out),
        grid=(G // GB,),
        in_specs=[pl.BlockSpec((GB * H, n), lambda i: (i, 0)),
                  pl.BlockSpec((T, GB * H, ncx), lambda i: (0, i, 0)),
                  pl.BlockSpec((None, GB, 2 * TH, TH), lambda i: (layer, i, 0, 0)),
                  pl.BlockSpec((None, GB, TH, TH), lambda i: (layer, i, 0, 0)),
                  pl.BlockSpec((None, 2, GB, LANES), lambda i: (layer, 0, i, 0)),
                  pl.BlockSpec((None, wr, D), lambda i: (layer, i, 0))],
        out_specs=[pl.BlockSpec((GB * H, n), lambda i: (i, 0)),
                   pl.BlockSpec((T, GB * H, ncx), lambda i: (0, i, 0)),
                   pl.BlockSpec((None, wr, D), lambda i: (0, i, 0))],
        out_shape=[jax.ShapeDtypeStruct((DS, n), BF16), jax.ShapeDtypeStruct((T, DS, ncx), BF16),
                   jax.ShapeDtypeStruct((1, 2 * DS, D), BF16)],
        scratch_shapes=[state, state, state, state, pltpu.VMEM((GB, TH, NCP), F32)],
        compiler_params=_cparams(("parallel",)),
        name="s5",
    )(ut, utc, w1t, w2, acoef, w_out)


def _outproj_kernel(*refs, cb, tb, nb, ntt, row_chunks, alpha, next_u):
    (x_hbm, mod_ref, yt_ref, szt_ref, s_ref, sprev_ref, snext_ref, bg_ref, szc_ref,
     wgt_ref, bglu_ref, cw_ref, cb_ref, wo_ref, lng_ref, lnb_ref) = refs[:16]
    if next_u:
        modn_ref, wutn_ref, o_hbm, utn_ref, xbuf, xsem, obuf, osem = refs[16:]
    else:
        o_hbm, xbuf, xsem, obuf, osem = refs[16:]
    tt = pl.program_id(1)
    step = pl.program_id(0) * ntt + tt
    nsteps = nb * ntt
    slot = _fetch_planes(x_hbm, xbuf, xsem, step, nsteps, cb=cb, tb=tb, ntt=ntt)
    if cb == LANES:
        y = yt_ref[...].astype(F32)
    else:
        lane = lax.broadcasted_iota(jnp.int32, (1, LANES), 1)
        per = LANES // cb
        tiles = []
        for q in range(tb // per):
            acc = jnp.where(lane < cb, yt_ref[q * per].astype(F32), 0.0)
            for k in range(1, per):
                piece = jnp.where(lane < cb, yt_ref[q * per + k].astype(F32), 0.0)
                acc = acc + pltpu.roll(piece, k * cb, 1)
            tiles.append(acc)
        y = jnp.concatenate(tiles, axis=1)
    g2 = y * jnp.tanh(y * (GELU_C1 + GELU_C2 * (y * y))) + y
    half_gate = jnp.dot(wgt_ref[...], g2.astype(BF16), preferred_element_type=F32) + bglu_ref[...]
    o5 = (g2 * szt_ref[...].astype(F32) * (jnp.tanh(half_gate) + 1.0)).T.astype(BF16)
    crow = lax.broadcasted_iota(jnp.int32, (cb, 1), 0) % row_chunks
    sp = sprev_ref[...].astype(F32)
    sn = snext_ref[...].astype(F32)
    sp_wrap = jnp.where(crow != 0, pltpu.roll(sp, 1, 0), 0.0)
    sn_wrap = jnp.where(crow != row_chunks - 1, pltpu.roll(sn, cb - 1, 0), 0.0)
    sp = jnp.where(tt == 0, sp_wrap, sp).astype(BF16)
    sn = jnp.where(tt == ntt - 1, sn_wrap, sn).astype(BF16)
    s_all = jnp.concatenate([sp, s_ref[...], sn], axis=0)
    r = cb * tb
    cw = cw_ref[...].astype(BF16)
    conv = (cw[0:1, :] * s_all[0:r] + cw[1:2, :] * s_all[cb:cb + r]
            + cw[2:3, :] * s_all[2 * cb:2 * cb + r] + cb_ref[...].astype(BF16))
    ocv = bg_ref[...] * conv * szc_ref[...]
    sub = jnp.dot(jnp.concatenate([o5, ocv], axis=1), wo_ref[...], preferred_element_type=F32)
    gate_vec = mod_ref[2:3, :] * (1.0 / alpha)
    for t in range(tb):
        v = xbuf[slot, t] + gate_vec * sub[t * cb:(t + 1) * cb]
        obuf[slot, t] = _layernorm_rows(v, LN_EPS / (alpha * alpha)) * lng_ref[...] + lnb_ref[...]
    if next_u:
        shift, scale = modn_ref[0:1, :], modn_ref[1:2, :]
        hn = jnp.concatenate([(_layernorm_rows(obuf[slot, t]) * (1.0 + scale) + shift).astype(BF16)
                              for t in range(tb)], axis=0)
        _store_short_planes(lax.dot_general(wutn_ref[...], hn, NT_DIMS, preferred_element_type=F32),
                            utn_ref, cb, tb)
    put = functools.partial(_plane_copies, o_hbm, obuf, osem, cb=cb, tb=tb, ntt=ntt, to_hbm=True)

    @pl.when(step > 0)
    def _():
        for cp in put(step - 1, 1 - slot):
            cp.wait()

    for cp in put(step, slot):
        cp.start()

    @pl.when(step == nsteps - 1)
    def _():
        for cp in put(step, slot):
            cp.wait()


def _outproj(x3, mod, yt, szt, s, bg, szc, wo, wts, layer, mod_idx, cb, tb, row_chunks, alpha, next_u=None):
    nc = x3.shape[0]
    nb, ntt = nc // cb, T // tb
    r = cb * tb
    extra_in, extra_out, extra_shape, extra_args = [], [], [], []
    if next_u is not None:
        assert cb != LANES and nb == 1 and ntt == 1
        extra_in = [_layer_spec((3, D), next_u[0], 0, 0), _layer_spec((DS, D), 0, 0, 0)]
        extra_out = [pl.BlockSpec((T, DS, LANES), lambda b, tt: (0, 0, 0))]
        extra_shape = [jax.ShapeDtypeStruct((T, DS, LANES), BF16)]
        extra_args = [mod, next_u[1]]
    col = lambda b, tt: (0, b * ntt + tt)
    row = lambda b, tt: (b * ntt + tt, 0)
    prev_plane = lambda b, tt: (b * T + (tt * tb + T - 1) % T, 0)
    next_plane = lambda b, tt: (b * T + (tt * tb + tb) % T, 0)
    wgt, bglu, cw, cbias, lng, lnb = wts
    planes = pltpu.VMEM((2, tb, cb, D), F32)
    return pl.pallas_call(
        functools.partial(_outproj_kernel, cb=cb, tb=tb, nb=nb, ntt=ntt, row_chunks=row_chunks, alpha=alpha,
                          next_u=next_u is not None),
        grid=(nb, ntt),
        in_specs=[pl.BlockSpec(memory_space=pl.ANY),
                  _layer_spec((3, D), mod_idx, 0, 0),
                  pl.BlockSpec((DS, r), col) if cb == LANES else pl.BlockSpec((T, DS, LANES), lambda b, tt: (0, 0, 0)),
                  pl.BlockSpec((DS, r), col),
                  pl.BlockSpec((r, DS), row),
                  pl.BlockSpec((cb, DS), prev_plane), pl.BlockSpec((cb, DS), next_plane),
                  pl.BlockSpec((r, DS), row), pl.BlockSpec((r, DS), row),
                  _layer_spec((DS, DS), layer, 0, 0), _layer_spec((DS, 1), layer, 0, 0),
                  _layer_spec((3, DS), layer, 0, 0), _layer_spec((1, DS), layer, 0, 0),
                  _layer_spec((2 * DS, D), 0, 0, 0),
                  _layer_spec((1, D), layer, 0, 0), _layer_spec((1, D), layer, 0, 0)] + extra_in,
        out_specs=[pl.BlockSpec(memory_space=pl.ANY)] + extra_out,
        out_shape=[jax.ShapeDtypeStruct(x3.shape, F32)] + extra_shape,
        scratch_shapes=[planes, pltpu.SemaphoreType.DMA((2, tb)), planes, pltpu.SemaphoreType.DMA((2, tb))],
        compiler_params=_cparams(("arbitrary", "arbitrary")),
        name="outproj",
    )(x3, mod, yt, szt, s, s, s, bg, szc, wgt, bglu, cw, cbias, wo, lng, lnb, *extra_args)


def kernel(x, c, ctx, c_ctx, w_ada, b_ada, w_in, s5_a_re, s5_a_im, s5_log_dt, s5_b_re, s5_b_im,
           s5_c_re, s5_c_im, s5_d, w_glu, b_glu, conv_w, conv_b, w_out, ln_g, ln_b):
    bsz, n, d = x.shape
    n_ctx = ctx.shape[1]
    depth = w_in.shape[0]
    assert bsz == 1 and d == D and n == NCL * T and n_ctx == NCC * T
    alpha = (2.0 * depth) ** 0.25

    cond = jnp.pad(jnp.stack([c[0], c_ctx]), ((0, COND_ROWS - 2), (0, 0)))
    w1t, w2, acoef, w_row, w_t, w_glu_t, mod = _s5prep(s5_a_re, s5_a_im, s5_log_dt, s5_b_re, s5_b_im,
                                                       s5_c_re, s5_c_im, s5_d, w_in, w_glu, cond, w_ada, b_ada)
    mod = mod.reshape(depth * 2, 3, D)
    out_w = (w_glu_t, GLU_B_SCALE * b_glu.reshape(depth, DS, 1), conv_w, conv_b.reshape(depth, 1, DS),
             ln_g.reshape(depth, 1, D), ln_b.reshape(depth, 1, D))

    x3 = x.reshape(n // T, T, D)
    c3 = ctx.reshape(n_ctx // T, T, D)
    utc = None
    for i in range(depth):
        last = i == depth - 1
        ut, szt, s, bg, szc, *w_next = _inproj(x3, mod, w_t, w_row, 2 * i, CBL, TB_LATENT,
                                               w_next=None if last else w_in, next_layer=i + 1)
        if utc is None:
            utc, *ctx_rest = _inproj(c3, mod, w_t, w_row, 2 * i + 1, NCC, T, u_only=last)
        yt, ytc, wo = _s5(ut, utc, w1t, w2, acoef, w_out, i, ctx_out=not last)
        utc = None
        (x3,) = _outproj(x3, mod, yt, szt, s, bg, szc, wo, out_w, i, 2 * i, CBL, TB_LATENT, GRID_W // T, alpha)
        if not last:
            w_row, w_t = w_next
            next_u = (2 * (i + 1) + 1, w_t) if i + 1 == depth - 1 else None
            c3, *next_utc = _outproj(c3, mod, ytc, *ctx_rest, wo, out_w, i, 2 * i + 1, NCC, T, NCC, alpha,
                                     next_u=next_u)
            utc = next_utc[0] if next_utc else None
    return x3.reshape(bsz, n, D)
```
